```python
import math
import jax, jax.numpy as jnp
from jax import lax
import numpy as np

D_MODEL = 1024
BATCH = 4
SEQ = 4096
DEPTH = 1

CHUNK = 64
MIX_WIDTH = D_MODEL
CONV_WIDTH = MIX_WIDTH // 2
CONV_GROUPS = 8
CONV_K = 31
FOX_WIDTH = MIX_WIDTH - CONV_WIDTH
FOX_HEADS = 8
FOX_HEAD_DIM = FOX_WIDTH // FOX_HEADS
QBLOCK = 128
MEM_LEN = 256
MEM_HEADS = 4
MEM_HEAD_DIM = D_MODEL // MEM_HEADS
D_FF = ((8 * D_MODEL // 3 + 255) // 256) * 256
IN_COLS = 2 * CONV_WIDTH + 3 * FOX_WIDTH + FOX_HEADS
LN_EPS = 1e-5
NEG_INF = -1e30
DEEPNORM_ALPHA = (2.0 * DEPTH) ** 0.25
DEEPNORM_BETA = (8.0 * DEPTH) ** -0.25

kernel_name = "deepnorm_hybrid_conformer_fox_block"


def layer_norm(x, g, b):
    xf = x.astype(jnp.float32)
    mu = jnp.mean(xf, axis=-1, keepdims=True)
    var = jnp.mean(jnp.square(xf - mu), axis=-1, keepdims=True)
    y = (xf - mu) * lax.rsqrt(var + LN_EPS)
    return (y * g.astype(jnp.float32) + b.astype(jnp.float32)).astype(x.dtype)


def causal_depthwise_conv(u, w, b):
    kernel = w[:, None, :].astype(u.dtype)
    y = lax.conv_general_dilated(
        u, kernel, window_strides=(1,), padding=[(CONV_K - 1, 0)],
        dimension_numbers=("NWC", "WIO", "NWC"), feature_group_count=u.shape[-1])
    return y + b.astype(u.dtype)


def forgetting_attention(q, k, v, f_logit):
    bsz, seq, _ = q.shape
    nb = seq // QBLOCK
    q = q.reshape(bsz, seq, FOX_HEADS, FOX_HEAD_DIM)
    k = k.reshape(bsz, seq, FOX_HEADS, FOX_HEAD_DIM)
    v = v.reshape(bsz, seq, FOX_HEADS, FOX_HEAD_DIM)
    log_f = jax.nn.log_sigmoid(f_logit.astype(jnp.float32))
    cum = jnp.cumsum(log_f, axis=1).transpose(0, 2, 1)
    scale = 1.0 / math.sqrt(FOX_HEAD_DIM)
    kpos = jnp.arange(seq)

    def block(args):
        qb, cq, i = args
        qpos = i * QBLOCK + jnp.arange(QBLOCK)
        s = jnp.einsum("bqhd,bkhd->bhqk", qb, k, preferred_element_type=jnp.float32) * scale
        s = s + cq[..., None] - cum[:, :, None, :]
        s = jnp.where(kpos[None, :] <= qpos[:, None], s, NEG_INF)
        p = jax.nn.softmax(s, axis=-1)
        return jnp.einsum("bhqk,bkhd->bqhd", p.astype(v.dtype), v)

    qs = q.reshape(bsz, nb, QBLOCK, FOX_HEADS, FOX_HEAD_DIM).transpose(1, 0, 2, 3, 4)
    cs = cum.reshape(bsz, FOX_HEADS, nb, QBLOCK).transpose(2, 0, 1, 3)
    out = lax.map(block, (qs, cs, jnp.arange(nb)))
    return out.transpose(1, 0, 2, 3, 4).reshape(bsz, seq, FOX_WIDTH)


def hybrid_mixer(h, w_in, b_forget, conv_w, conv_b, conv_ln_g, conv_ln_b, w_out):
    proj = h @ w_in.astype(h.dtype)
    c0 = CONV_WIDTH
    c1 = c0 + CONV_WIDTH
    c2 = c1 + FOX_WIDTH
    c3 = c2 + FOX_WIDTH
    c4 = c3 + FOX_WIDTH
    glu_a, glu_b = proj[..., :c0], proj[..., c0:c1]
    q, k, v = proj[..., c1:c2], proj[..., c2:c3], proj[..., c3:c4]
    f_logit = proj[..., c4:] + b_forget.astype(h.dtype)
    u = glu_a * jax.nn.sigmoid(glu_b)
    u = causal_depthwise_conv(u, conv_w, conv_b)
    u = jax.nn.silu(layer_norm(u, conv_ln_g, conv_ln_b))
    o = forgetting_attention(q, k, v, f_logit)
    return jnp.concatenate([u, o], axis=-1) @ w_out.astype(h.dtype)


def memory_cross_attention(h, mem, w_cq, w_ck, w_cv, w_co):
    bsz, seq, _ = h.shape
    q = (h @ w_cq.astype(h.dtype)).reshape(bsz, seq, MEM_HEADS, MEM_HEAD_DIM)
    k = (mem @ w_ck.astype(h.dtype)).reshape(bsz, MEM_LEN, MEM_HEADS, MEM_HEAD_DIM)
    v = (mem @ w_cv.astype(h.dtype)).reshape(bsz, MEM_LEN, MEM_HEADS, MEM_HEAD_DIM)
    s = jnp.einsum("bqhd,bmhd->bhqm", q, k, preferred_element_type=jnp.float32) / math.sqrt(MEM_HEAD_DIM)
    p = jax.nn.softmax(s, axis=-1)
    o = jnp.einsum("bhqm,bmhd->bqhd", p.astype(v.dtype), v).reshape(bsz, seq, D_MODEL)
    return o @ w_co.astype(h.dtype)


def swiglu(h, w_gate, w_up, w_down):
    dt = h.dtype
    return (jax.nn.silu(h @ w_gate.astype(dt)) * (h @ w_up.astype(dt))) @ w_down.astype(dt)


def setup_inputs(seed: int = 0) -> dict:
    key = jax.random.key(seed)
    ks = jax.random.split(key, 24)
    f32 = jnp.float32
    L, D = DEPTH, D_MODEL
    nrm = lambda k, shape, s: jax.random.normal(k, shape, f32) * s
    gain = lambda k, shape: 1.0 + 0.02 * jax.random.normal(k, shape, f32)
    bias = lambda k, shape: 0.02 * jax.random.normal(k, shape, f32)
    col_scale = jnp.concatenate([
        jnp.ones((2 * CONV_WIDTH + 2 * FOX_WIDTH,), f32),
        jnp.full((FOX_WIDTH,), DEEPNORM_BETA, f32),
        jnp.ones((FOX_HEADS,), f32)])
    return {
        "x": jax.random.normal(ks[0], (BATCH, SEQ, D), f32),
        "mem": jax.random.normal(ks[1], (BATCH, MEM_LEN, D), f32),
        "w_in": nrm(ks[2], (L, D, IN_COLS), D ** -0.5) * col_scale,
        "b_forget": 2.0 + 0.5 * jax.random.normal(ks[3], (L, FOX_HEADS), f32),
        "conv_w": nrm(ks[4], (L, CONV_K, CONV_WIDTH), CONV_K ** -0.5),
        "conv_b": bias(ks[5], (L, CONV_WIDTH)),
        "conv_ln_g": gain(ks[6], (L, CONV_WIDTH)),
        "conv_ln_b": bias(ks[7], (L, CONV_WIDTH)),
        "w_out": nrm(ks[8], (L, MIX_WIDTH, D), MIX_WIDTH ** -0.5 * DEEPNORM_BETA),
        "ln_mix_g": gain(ks[9], (L, D)),
        "ln_mix_b": bias(ks[10], (L, D)),
        "w_cq": nrm(ks[11], (L, D, D), D ** -0.5),
        "w_ck": nrm(ks[12], (L, D, D), D ** -0.5),
        "w_cv": nrm(ks[13], (L, D, D), D ** -0.5 * DEEPNORM_BETA),
        "w_co": nrm(ks[14], (L, D, D), D ** -0.5 * DEEPNORM_BETA),
        "ln_cross_g": gain(ks[15], (L, D)),
        "ln_cross_b": bias(ks[16], (L, D)),
        "w_gate": nrm(ks[17], (L, D, D_FF), D ** -0.5),
        "w_up": nrm(ks[18], (L, D, D_FF), D ** -0.5),
        "w_down": nrm(ks[19], (L, D_FF, D), D_FF ** -0.5 * DEEPNORM_BETA),
        "ln_ffn_g": gain(ks[20], (L, D)),
        "ln_ffn_b": bias(ks[21], (L, D)),
    }


def reference(x, mem, w_in, b_forget, conv_w, conv_b, conv_ln_g, conv_ln_b, w_out,
              ln_mix_g, ln_mix_b, w_cq, w_ck, w_cv, w_co, ln_cross_g, ln_cross_b,
              w_gate, w_up, w_down, ln_ffn_g, ln_ffn_b):
    h = x
    for l in range(DEPTH):
        y = hybrid_mixer(h, w_in[l], b_forget[l], conv_w[l], conv_b[l],
                         conv_ln_g[l], conv_ln_b[l], w_out[l])
        h = layer_norm(DEEPNORM_ALPHA * h + y, ln_mix_g[l], ln_mix_b[l])
        y = memory_cross_attention(h, mem, w_cq[l], w_ck[l], w_cv[l], w_co[l])
        h = layer_norm(DEEPNORM_ALPHA * h + y, ln_cross_g[l], ln_cross_b[l])
        y = swiglu(h, w_gate[l], w_up[l], w_down[l])
        h = layer_norm(DEEPNORM_ALPHA * h + y, ln_ffn_g[l], ln_ffn_b[l])
    return h
```

```python
import functools
import math

import jax
import jax.numpy as jnp
from jax import lax
from jax.experimental import pallas as pl
from jax.experimental.pallas import tpu as pltpu

D_MODEL = 1024
CONV_WIDTH = 512
CONV_K = 31
FOX_WIDTH = 512
FOX_HEADS = 8
FOX_HEAD_DIM = 64
MEM_LEN = 256
MEM_HEADS = 4
MEM_HEAD_DIM = 256
D_FF = 2816
LN_EPS = 1e-5
NEG_INF = -1e30
DEEPNORM_ALPHA = 2.0 ** 0.25

LANES = 128
HALO = 32
ROW_TILE = 512
Q_TILE = 512
K_TILE = 512
FF_CHUNK = 256
CONV_ROWS = 32
VMEM_LIMIT = 56 * 1024 * 1024

BF16 = jnp.bfloat16
F32 = jnp.float32

C_GLU_A = 0
C_GLU_B = CONV_WIDTH
C_Q = 2 * CONV_WIDTH
C_K = C_Q + FOX_WIDTH
C_V = C_K + FOX_WIDTH
C_F = C_V + FOX_WIDTH


def _mm(a, b):
    return jnp.dot(a, b, preferred_element_type=F32)


def _mm_nt(a, b):
    return lax.dot_general(a, b, (((1,), (1,)), ((), ())), preferred_element_type=F32)


def _layer_norm(x, g, b):
    mu = jnp.mean(x, axis=-1, keepdims=True)
    xc = x - mu
    var = jnp.mean(xc * xc, axis=-1, keepdims=True)
    return xc * lax.rsqrt(var + LN_EPS) * g + b


def _sigmoid(x):
    return 1.0 / (1.0 + jnp.exp(-x))


def _split3(x):
    hi = x.astype(BF16).astype(F32)
    r = x - hi
    mid = r.astype(BF16).astype(F32)
    lo = (r - mid).astype(BF16).astype(F32)
    return hi, mid, lo


def _pack3(hi, mid, lo):
    return hi + pltpu.roll(mid, 8, axis=1) + pltpu.roll(lo, 16, axis=1)


def _inproj_kernel(x_ref, w_ref, wf_ref, bf_ref, tri_ref, cw_ref, cb_ref, cg_ref, cbeta_ref,
                   uc_ref, q_ref, k_ref, v_ref, ubuf, carry):
    t = pl.program_id(1)
    rows = x_ref.shape[1]
    xb = x_ref[0].astype(BF16)

    @pl.when(t == 0)
    def _():
        ubuf[0:HALO, :] = jnp.zeros((HALO, CONV_WIDTH), F32)
        carry[...] = jnp.zeros_like(carry)

    @pl.when(t > 0)
    def _():
        ubuf[0:HALO, :] = ubuf[rows:rows + HALO, :]

    glu_a = _mm(xb, w_ref[:, C_GLU_A:C_GLU_A + CONV_WIDTH])
    glu_b = _mm(xb, w_ref[:, C_GLU_B:C_GLU_B + CONV_WIDTH])
    ubuf[HALO:HALO + rows, :] = glu_a * _sigmoid(glu_b)

    cw = cw_ref[...]
    cbias = cb_ref[...]
    gam = cg_ref[...]
    beta = cbeta_ref[...]
    first = HALO - (CONV_K - 1)
    for r in range(rows // CONV_ROWS):
        base = r * CONV_ROWS + first
        acc = jnp.broadcast_to(cbias, (CONV_ROWS, CONV_WIDTH))
        for j in range(CONV_K):
            acc = acc + cw[j:j + 1, :] * ubuf[base + j:base + j + CONV_ROWS, :]
        y = _layer_norm(acc, gam, beta)
        uc_ref[0, r * CONV_ROWS:(r + 1) * CONV_ROWS, :] = (y * _sigmoid(y)).astype(BF16)

    lane = lax.broadcasted_iota(jnp.int32, (rows, LANES), 1)
    f = _mm(xb, wf_ref[...]) + bf_ref[...]
    logf = jnp.minimum(f, 0.0) - jnp.log(1.0 + jnp.exp(-jnp.abs(f)))
    logf = jnp.where(lane < FOX_HEADS, logf, 0.0)
    packed = _pack3(*_split3(logf)).astype(BF16)
    res = _mm(tri_ref[...], packed)
    cum = res + pltpu.roll(res, LANES - 8, axis=1) + pltpu.roll(res, LANES - 16, axis=1)
    cum = jnp.where(lane < FOX_HEADS, cum + carry[0:1, :], 0.0)
    carry[0:1, :] = cum[rows - 1:rows, :]
    cpack = _pack3(*_split3(cum))

    scale = 1.0 / math.sqrt(FOX_HEAD_DIM)
    for hp in range(FOX_HEADS // 2):
        qpair = _mm(xb, w_ref[:, C_Q + hp * LANES:C_Q + (hp + 1) * LANES]) * scale
        kpair = _mm(xb, w_ref[:, C_K + hp * LANES:C_K + (hp + 1) * LANES])
        for sub in range(2):
            h = 2 * hp + sub
            aug0 = FOX_HEAD_DIM if sub == 0 else 0
            data = (lane < FOX_HEAD_DIM) if sub == 0 else (lane >= FOX_HEAD_DIM)
            slot_a = (lane == aug0) | (lane == aug0 + 8) | (lane == aug0 + 16)
            slot_b = (lane == aug0 + 1) | (lane == aug0 + 9) | (lane == aug0 + 17)
            ra = pltpu.roll(cpack, (aug0 - h) % LANES, axis=1)
            rb = pltpu.roll(cpack, (aug0 + 1 - h) % LANES, axis=1)
            qa = jnp.where(data, qpair, jnp.where(slot_a, ra, jnp.where(slot_b, 1.0, 0.0)))
            ka = jnp.where(data, kpair, jnp.where(slot_a, 1.0, jnp.where(slot_b, -rb, 0.0)))
            q_ref[0, h] = qa.astype(BF16)
            k_ref[0, h] = ka.astype(BF16)
    v_ref[0] = _mm(xb, w_ref[:, C_V:C_V + FOX_WIDTH]).astype(BF16)


def _inproj(x, w_main, w_f, b_f, tri, conv_w, conv_b, conv_g, conv_beta):
    bsz, seq, _ = x.shape
    nt = seq // ROW_TILE
    const2 = lambda b, t: (0, 0)
    return pl.pallas_call(
        _inproj_kernel,
        grid=(bsz, nt),
        in_specs=[
            pl.BlockSpec((1, ROW_TILE, D_MODEL), lambda b, t: (b, t, 0)),
            pl.BlockSpec(w_main.shape, const2),
            pl.BlockSpec(w_f.shape, const2),
            pl.BlockSpec(b_f.shape, const2),
            pl.BlockSpec(tri.shape, const2),
            pl.BlockSpec(conv_w.shape, const2),
            pl.BlockSpec(conv_b.shape, const2),
            pl.BlockSpec(conv_g.shape, const2),
            pl.BlockSpec(conv_beta.shape, const2),
        ],
        out_specs=[
            pl.BlockSpec((1, ROW_TILE, CONV_WIDTH), lambda b, t: (b, t, 0)),
            pl.BlockSpec((1, FOX_HEADS, ROW_TILE, LANES), lambda b, t: (b, 0, t, 0)),
            pl.BlockSpec((1, FOX_HEADS, ROW_TILE, LANES), lambda b, t: (b, 0, t, 0)),
            pl.BlockSpec((1, ROW_TILE, FOX_WIDTH), lambda b, t: (b, t, 0)),
        ],
        out_shape=[
            jax.ShapeDtypeStruct((bsz, seq, CONV_WIDTH), BF16),
            jax.ShapeDtypeStruct((bsz, FOX_HEADS, seq, LANES), BF16),
            jax.ShapeDtypeStruct((bsz, FOX_HEADS, seq, LANES), BF16),
            jax.ShapeDtypeStruct((bsz, seq, FOX_WIDTH), BF16),
        ],
        scratch_shapes=[
            pltpu.VMEM((ROW_TILE + HALO, CONV_WIDTH), F32),
            pltpu.VMEM((8, LANES), F32),
        ],
        compiler_params=pltpu.CompilerParams(
            dimension_semantics=("arbitrary", "arbitrary"), vmem_limit_bytes=VMEM_LIMIT),
        name="inproj_conv",
    )(x, w_main, w_f, b_f, tri, conv_w, conv_b, conv_g, conv_beta)


def _fox_kernel(q_ref, k_ref, v_ref, o_ref, vt_scr, m_scr, l_scr, acc_scr):
    i = pl.program_id(2)
    nk = vt_scr.shape[0]

    @pl.when(i == 0)
    def _():
        for c in range(nk):
            blk = v_ref[0, c * K_TILE:(c + 1) * K_TILE, :].astype(F32)
            vt_scr[c] = blk.T.astype(BF16)

    m_scr[...] = jnp.full(m_scr.shape, NEG_INF, F32)
    l_scr[...] = jnp.zeros(l_scr.shape, F32)
    acc_scr[...] = jnp.zeros(acc_scr.shape, F32)

    def step(j, masked):
        for h in range(2):
            kt = k_ref[0, h, pl.ds(pl.multiple_of(j * K_TILE, K_TILE), K_TILE), :]
            st = _mm_nt(kt, q_ref[0, h])
            if masked:
                kpos = j * K_TILE + lax.broadcasted_iota(jnp.int32, st.shape, 0)
                qpos = i * Q_TILE + lax.broadcasted_iota(jnp.int32, st.shape, 1)
                st = jnp.where(kpos <= qpos, st, NEG_INF)
            m_old = m_scr[h]
            m_new = jnp.maximum(m_old, jnp.max(st, axis=0, keepdims=True))
            alpha = jnp.exp(m_old - m_new)
            pt = jnp.exp(st - m_new)
            l_scr[h] = alpha * l_scr[h] + jnp.sum(pt, axis=0, keepdims=True)
            vt = vt_scr[j, h * FOX_HEAD_DIM:(h + 1) * FOX_HEAD_DIM, :]
            acc_scr[h] = alpha * acc_scr[h] + _mm(vt, pt.astype(BF16))
            m_scr[h] = m_new

    ratio = Q_TILE // K_TILE

    def body(j, c):
        step(j, False)
        return c

    lax.fori_loop(0, i * ratio, body, 0)
    for d in range(ratio):
        step(i * ratio + d, True)

    ot = jnp.concatenate([acc_scr[0] / l_scr[0], acc_scr[1] / l_scr[1]], axis=0)
    o_ref[0] = ot.T.astype(BF16)


def _fox_attention(q_aug, k_aug, v):
    bsz, _, seq, _ = q_aug.shape
    nq = seq // Q_TILE
    nk = seq // K_TILE
    return pl.pallas_call(
        _fox_kernel,
        grid=(bsz, FOX_HEADS // 2, nq),
        in_specs=[
            pl.BlockSpec((1, 2, Q_TILE, LANES), lambda b, hp, i: (b, hp, i, 0)),
            pl.BlockSpec((1, 2, seq, LANES), lambda b, hp, i: (b, hp, 0, 0)),
            pl.BlockSpec((1, seq, LANES), lambda b, hp, i: (b, 0, hp)),
        ],
        out_specs=pl.BlockSpec((1, Q_TILE, LANES), lambda b, hp, i: (b, i, hp)),
        out_shape=jax.ShapeDtypeStruct((bsz, seq, FOX_WIDTH), BF16),
        scratch_shapes=[
            pltpu.VMEM((nk, LANES, K_TILE), BF16),
            pltpu.VMEM((2, 1, Q_TILE), F32),
            pltpu.VMEM((2, 1, Q_TILE), F32),
            pltpu.VMEM((2, FOX_HEAD_DIM, Q_TILE), F32),
        ],
        compiler_params=pltpu.CompilerParams(
            dimension_semantics=("arbitrary", "arbitrary", "arbitrary"), vmem_limit_bytes=VMEM_LIMIT),
        name="fox_attention",
    )(q_aug, k_aug, v)


def _memkv_kernel(mem_ref, wk_ref, wv_ref, k_ref, v_ref):
    mb = mem_ref[0].astype(BF16)
    k_ref[0] = _mm(mb, wk_ref[...]).astype(BF16)
    v_ref[0] = _mm(mb, wv_ref[...]).astype(BF16)


def _mem_kv(mem, w_ck, w_cv):
    bsz = mem.shape[0]
    const2 = lambda b: (0, 0)
    blk = pl.BlockSpec((1, MEM_LEN, D_MODEL), lambda b: (b, 0, 0))
    return pl.pallas_call(
        _memkv_kernel,
        grid=(bsz,),
        in_specs=[blk, pl.BlockSpec(w_ck.shape, const2), pl.BlockSpec(w_cv.shape, const2)],
        out_specs=[blk, blk],
        out_shape=[jax.ShapeDtypeStruct((bsz, MEM_LEN, D_MODEL), BF16)] * 2,
        compiler_params=pltpu.CompilerParams(
            dimension_semantics=("arbitrary",), vmem_limit_bytes=VMEM_LIMIT),
        name="mem_kv",
    )(mem, w_ck, w_cv)


def _mix_cross_kernel(x_ref, uc_ref, o_ref, wout_ref, g1_ref, b1_ref, wcq_ref, kc_ref, vc_ref, wco_ref,
                      g2_ref, b2_ref, h_ref):
    mix = _mm(uc_ref[0], wout_ref[0:CONV_WIDTH, :]) + _mm(o_ref[0], wout_ref[CONV_WIDTH:, :])
    h1 = _layer_norm(DEEPNORM_ALPHA * x_ref[0] + mix, g1_ref[...], b1_ref[...])

    q = (_mm(h1.astype(BF16), wcq_ref[...]) * (1.0 / math.sqrt(MEM_HEAD_DIM))).astype(BF16)
    outs = []
    for hh in range(MEM_HEADS):
        cols = slice(hh * MEM_HEAD_DIM, (hh + 1) * MEM_HEAD_DIM)
        s = _mm_nt(q[:, cols], kc_ref[0, :, cols])
        p = jnp.exp(s - jnp.max(s, axis=-1, keepdims=True))
        l = jnp.sum(p, axis=-1, keepdims=True)
        outs.append((_mm(p.astype(BF16), vc_ref[0, :, cols]) / l).astype(BF16))
    o = jnp.concatenate(outs, axis=-1)
    y = _mm(o, wco_ref[...])
    h_ref[0] = _layer_norm(DEEPNORM_ALPHA * h1 + y, g2_ref[...], b2_ref[...])


def _mix_cross(x, uc, o, w_out, g1, b1, w_cq, kc, vc, w_co, g2, b2):
    bsz, seq, _ = x.shape
    nt = seq // ROW_TILE
    const2 = lambda b, t: (0, 0)
    tok = lambda width: pl.BlockSpec((1, ROW_TILE, width), lambda b, t: (b, t, 0))
    memblk = pl.BlockSpec((1, MEM_LEN, D_MODEL), lambda b, t: (b, 0, 0))
    vec = pl.BlockSpec((1, D_MODEL), const2)
    mat = pl.BlockSpec((D_MODEL, D_MODEL), const2)
    return pl.pallas_call(
        _mix_cross_kernel,
        grid=(bsz, nt),
        in_specs=[tok(D_MODEL), tok(CONV_WIDTH), tok(FOX_WIDTH), mat, vec, vec, mat, memblk, memblk, mat,
                  vec, vec],
        out_specs=tok(D_MODEL),
        out_shape=jax.ShapeDtypeStruct((bsz, seq, D_MODEL), F32),
        compiler_params=pltpu.CompilerParams(
            dimension_semantics=("arbitrary", "arbitrary"), vmem_limit_bytes=VMEM_LIMIT),
        name="mix_cross",
    )(x, uc, o, w_out, g1, b1, w_cq, kc, vc, w_co, g2, b2)


def _swiglu_kernel(h_ref, wg_ref, wu_ref, wd_ref, g_ref, b_ref, out_ref):
    h = h_ref[...]
    hb = h.astype(BF16)
    acc = DEEPNORM_ALPHA * h
    for c in range(D_FF // FF_CHUNK):
        cols = slice(c * FF_CHUNK, (c + 1) * FF_CHUNK)
        gate = _mm(hb, wg_ref[:, cols])
        up = _mm(hb, wu_ref[:, cols])
        act = (gate * _sigmoid(gate) * up).astype(BF16)
        acc = acc + _mm(act, wd_ref[cols, :])
    out_ref[...] = _layer_norm(acc, g_ref[...], b_ref[...])


def _swiglu(h, w_gate, w_up, w_down, g, b):
    n = h.shape[0]
    const2 = lambda t: (0, 0)
    tok = pl.BlockSpec((ROW_TILE, D_MODEL), lambda t: (t, 0))
    return pl.pallas_call(
        _swiglu_kernel,
        grid=(n // ROW_TILE,),
        in_specs=[tok, pl.BlockSpec(w_gate.shape, const2), pl.BlockSpec(w_up.shape, const2),
                  pl.BlockSpec(w_down.shape, const2), pl.BlockSpec((1, D_MODEL), const2),
                  pl.BlockSpec((1, D_MODEL), const2)],
        out_specs=tok,
        out_shape=jax.ShapeDtypeStruct((n, D_MODEL), F32),
        compiler_params=pltpu.CompilerParams(
            dimension_semantics=("arbitrary",), vmem_limit_bytes=VMEM_LIMIT),
        name="swiglu",
    )(h, w_gate, w_up, w_down, g, b)


def _layer(h, mem, w_in, b_forget, conv_w, conv_b, conv_ln_g, conv_ln_b, w_out, ln_mix_g, ln_mix_b,
           w_cq, w_ck, w_cv, w_co, ln_cross_g, ln_cross_b, w_gate, w_up, w_down, ln_ffn_g, ln_ffn_b):
    bsz, seq, _ = h.shape
    row = lambda v: v.reshape(1, -1).astype(F32)
    w_main = w_in[:, :C_F].astype(BF16)
    w_f = jnp.pad(w_in[:, C_F:], ((0, 0), (0, LANES - FOX_HEADS))).astype(BF16)
    b_f = jnp.pad(b_forget.astype(F32), (0, LANES - FOX_HEADS)).reshape(1, LANES)
    tri = jnp.tri(ROW_TILE, dtype=BF16)

    uc, q_aug, k_aug, v = _inproj(h, w_main, w_f, b_f, tri, conv_w.astype(F32), row(conv_b),
                                  row(conv_ln_g), row(conv_ln_b))
    o = _fox_attention(q_aug, k_aug, v)
    kc, vc = _mem_kv(mem, w_ck.astype(BF16), w_cv.astype(BF16))
    h2 = _mix_cross(h, uc, o, w_out.astype(BF16), row(ln_mix_g), row(ln_mix_b), w_cq.astype(BF16), kc, vc,
                    w_co.astype(BF16), row(ln_cross_g), row(ln_cross_b))
    h3 = _swiglu(h2.reshape(bsz * seq, D_MODEL), w_gate.astype(BF16), w_up.astype(BF16),
                 w_down.astype(BF16), row(ln_ffn_g), row(ln_ffn_b))
    return h3.reshape(bsz, seq, D_MODEL)


def kernel(x, mem, w_in, b_forget, conv_w, conv_b, conv_ln_g, conv_ln_b, w_out, ln_mix_g, ln_mix_b,
           w_cq, w_ck, w_cv, w_co, ln_cross_g, ln_cross_b, w_gate, w_up, w_down, ln_ffn_g, ln_ffn_b):
    depth = w_in.shape[0]
    assert depth == 1, "DEEPNORM_ALPHA is fixed for a single layer"
    h = x
    for l in range(depth):
        h = _layer(h, mem, w_in[l], b_forget[l], conv_w[l], conv_b[l], conv_ln_g[l], conv_ln_b[l],
                   w_out[l], ln_mix_g[l], ln_mix_b[l], w_cq[l], w_ck[l], w_cv[l], w_co[l],
                   ln_cross_g[l], ln_cross_b[l], w_gate[l], w_up[l], w_down[l], ln_ffn_g[l], ln_ffn_b[l])
    return h
```

```python
import functools
import math

import jax
import jax.numpy as jnp
from jax import lax
from jax.experimental import pallas as pl
from jax.experimental.pallas import tpu as pltpu

D_MODEL = 1024
CONV_WIDTH = 512
CONV_K = 31
FOX_WIDTH = 512
FOX_HEADS = 8
FOX_HEAD_DIM = 64
MEM_LEN = 256
MEM_HEADS = 4
MEM_HEAD_DIM = 256
D_FF = 2816
LN_EPS = 1e-5
NEG_INF = -1e30
DEEPNORM_ALPHA = 2.0 ** 0.25
LOG2E = math.log2(math.e)

LANES = 128
HALO = 32
ROW_TILE = 512
Q_TILE = 512
K_TILE = 512
Q_SUB = 256
EXP_ROWS = 64
VT_ROWS = FOX_HEAD_DIM + 16
FF_CHUNK = 256
CONV_ROWS = 32
VMEM_LIMIT = 56 * 1024 * 1024

BF16 = jnp.bfloat16
F32 = jnp.float32

C_GLU_A = 0
C_GLU_B = CONV_WIDTH
C_Q = 2 * CONV_WIDTH
C_K = C_Q + FOX_WIDTH
C_V = C_K + FOX_WIDTH
C_F = C_V + FOX_WIDTH


def _mm(a, b):
    return jnp.dot(a, b, preferred_element_type=F32)


def _mm_nt(a, b):
    return lax.dot_general(a, b, (((1,), (1,)), ((), ())), preferred_element_type=F32)


def _layer_norm(x, g, b):
    mu = jnp.mean(x, axis=-1, keepdims=True)
    xc = x - mu
    var = jnp.mean(xc * xc, axis=-1, keepdims=True)
    return xc * lax.rsqrt(var + LN_EPS) * g + b


def _sigmoid(x):
    return 1.0 / (1.0 + jnp.exp(-x))


def _split3(x):
    hi = x.astype(BF16).astype(F32)
    r = x - hi
    mid = r.astype(BF16).astype(F32)
    lo = (r - mid).astype(BF16).astype(F32)
    return hi, mid, lo


def _pack3(hi, mid, lo):
    return hi + pltpu.roll(mid, 8, axis=1) + pltpu.roll(lo, 16, axis=1)


def _inproj_kernel(x_ref, w_ref, wf_ref, bf_ref, tri_ref, cw_ref, cb_ref, cg_ref, cbeta_ref,
                   uc_ref, q_ref, k_ref, v_ref, ubuf, carry):
    t = pl.program_id(1)
    rows = x_ref.shape[1]
    xb = x_ref[0].astype(BF16)

    @pl.when(t == 0)
    def _():
        ubuf[0:HALO, :] = jnp.zeros((HALO, CONV_WIDTH), F32)
        carry[...] = jnp.zeros_like(carry)

    @pl.when(t > 0)
    def _():
        ubuf[0:HALO, :] = ubuf[rows:rows + HALO, :]

    glu_a = _mm(xb, w_ref[:, C_GLU_A:C_GLU_A + CONV_WIDTH])
    glu_b = _mm(xb, w_ref[:, C_GLU_B:C_GLU_B + CONV_WIDTH])
    ubuf[HALO:HALO + rows, :] = glu_a * _sigmoid(glu_b)

    cw = cw_ref[...]
    cbias = cb_ref[...]
    gam = cg_ref[...]
    beta = cbeta_ref[...]
    first = HALO - (CONV_K - 1)
    for r in range(rows // CONV_ROWS):
        base = r * CONV_ROWS + first
        acc = jnp.broadcast_to(cbias, (CONV_ROWS, CONV_WIDTH))
        for j in range(CONV_K):
            acc = acc + cw[j:j + 1, :] * ubuf[base + j:base + j + CONV_ROWS, :]
        y = _layer_norm(acc, gam, beta)
        uc_ref[0, r * CONV_ROWS:(r + 1) * CONV_ROWS, :] = (y * _sigmoid(y)).astype(BF16)

    lane = lax.broadcasted_iota(jnp.int32, (rows, LANES), 1)
    f = _mm(xb, wf_ref[...]) + bf_ref[...]
    logf = jnp.minimum(f, 0.0) - jnp.log(1.0 + jnp.exp(-jnp.abs(f)))
    logf = jnp.where(lane < FOX_HEADS, logf, 0.0)
    packed = _pack3(*_split3(logf)).astype(BF16)
    res = _mm(tri_ref[...], packed)
    cum = res + pltpu.roll(res, LANES - 8, axis=1) + pltpu.roll(res, LANES - 16, axis=1)
    cum = jnp.where(lane < FOX_HEADS, cum + carry[0:1, :], 0.0)
    carry[0:1, :] = cum[rows - 1:rows, :]
    cpack = _pack3(*_split3(cum * LOG2E))

    scale = LOG2E / math.sqrt(FOX_HEAD_DIM)
    for hp in range(FOX_HEADS // 2):
        qpair = _mm(xb, w_ref[:, C_Q + hp * LANES:C_Q + (hp + 1) * LANES]) * scale
        kpair = _mm(xb, w_ref[:, C_K + hp * LANES:C_K + (hp + 1) * LANES])
        for sub in range(2):
            h = 2 * hp + sub
            aug0 = FOX_HEAD_DIM if sub == 0 else 0
            data = (lane < FOX_HEAD_DIM) if sub == 0 else (lane >= FOX_HEAD_DIM)
            slot_a = (lane == aug0) | (lane == aug0 + 8) | (lane == aug0 + 16)
            slot_b = (lane == aug0 + 1) | (lane == aug0 + 9) | (lane == aug0 + 17)
            ra = pltpu.roll(cpack, (aug0 - h) % LANES, axis=1)
            rb = pltpu.roll(cpack, (aug0 + 1 - h) % LANES, axis=1)
            qa = jnp.where(data, qpair, jnp.where(slot_a, ra, jnp.where(slot_b, 1.0, 0.0)))
            ka = jnp.where(data, kpair, jnp.where(slot_a, 1.0, jnp.where(slot_b, -rb, 0.0)))
            q_ref[0, h] = qa.astype(BF16)
            k_ref[0, h] = ka.astype(BF16)
    v_ref[0] = _mm(xb, w_ref[:, C_V:C_V + FOX_WIDTH]).astype(BF16)


def _inproj(x, w_main, w_f, b_f, tri, conv_w, conv_b, conv_g, conv_beta):
    bsz, seq, _ = x.shape
    nt = seq // ROW_TILE
    const2 = lambda b, t: (0, 0)
    return pl.pallas_call(
        _inproj_kernel,
        grid=(bsz, nt),
        in_specs=[
            pl.BlockSpec((1, ROW_TILE, D_MODEL), lambda b, t: (b, t, 0)),
            pl.BlockSpec(w_main.shape, const2),
            pl.BlockSpec(w_f.shape, const2),
            pl.BlockSpec(b_f.shape, const2),
            pl.BlockSpec(tri.shape, const2),
            pl.BlockSpec(conv_w.shape, const2),
            pl.BlockSpec(conv_b.shape, const2),
            pl.BlockSpec(conv_g.shape, const2),
            pl.BlockSpec(conv_beta.shape, const2),
        ],
        out_specs=[
            pl.BlockSpec((1, ROW_TILE, CONV_WIDTH), lambda b, t: (b, t, 0)),
            pl.BlockSpec((1, FOX_HEADS, ROW_TILE, LANES), lambda b, t: (b, 0, t, 0)),
            pl.BlockSpec((1, FOX_HEADS, ROW_TILE, LANES), lambda b, t: (b, 0, t, 0)),
            pl.BlockSpec((1, ROW_TILE, FOX_WIDTH), lambda b, t: (b, t, 0)),
        ],
        out_shape=[
            jax.ShapeDtypeStruct((bsz, seq, CONV_WIDTH), BF16),
            jax.ShapeDtypeStruct((bsz, FOX_HEADS, seq, LANES), BF16),
            jax.ShapeDtypeStruct((bsz, FOX_HEADS, seq, LANES), BF16),
            jax.ShapeDtypeStruct((bsz, seq, FOX_WIDTH), BF16),
        ],
        scratch_shapes=[
            pltpu.VMEM((ROW_TILE + HALO, CONV_WIDTH), F32),
            pltpu.VMEM((8, LANES), F32),
        ],
        compiler_params=pltpu.CompilerParams(
            dimension_semantics=("arbitrary", "arbitrary"), vmem_limit_bytes=VMEM_LIMIT),
        name="inproj_conv",
    )(x, w_main, w_f, b_f, tri, conv_w, conv_b, conv_g, conv_beta)


def _fox_kernel(q_ref, k_ref, v_ref, o_ref, vt_scr, qt_scr, acc_scr, s_scr, m_scr):
    i = pl.program_id(2)
    nk = vt_scr.shape[0]
    units = [(h, qb) for h in range(2) for qb in range(Q_TILE // Q_SUB)]

    @pl.when(i == 0)
    def _():
        tail = (lax.broadcasted_iota(jnp.int32, (VT_ROWS - FOX_HEAD_DIM, K_TILE), 0) == 0).astype(BF16)
        for c in range(nk):
            blk = v_ref[0, c * K_TILE:(c + 1) * K_TILE, :].astype(F32).T.astype(BF16)
            for h in range(2):
                vt_scr[c, h, 0:FOX_HEAD_DIM, :] = blk[h * FOX_HEAD_DIM:(h + 1) * FOX_HEAD_DIM, :]
                vt_scr[c, h, FOX_HEAD_DIM:VT_ROWS, :] = tail

    for h in range(2):
        qt_scr[h] = q_ref[0, h].astype(F32).T.astype(BF16)
    acc_scr[...] = jnp.zeros(acc_scr.shape, F32)

    m_scr[...] = jnp.full(m_scr.shape, NEG_INF, F32)

    def score_unit(j, u, diag):
        h, qb = units[u]
        kt = k_ref[0, h, pl.ds(pl.multiple_of(j * K_TILE, K_TILE), K_TILE), :]
        st = _mm(kt, qt_scr[h, :, qb * Q_SUB:(qb + 1) * Q_SUB])
        if diag:
            kpos = lax.broadcasted_iota(jnp.int32, st.shape, 0)
            qpos = lax.broadcasted_iota(jnp.int32, st.shape, 1) + qb * Q_SUB
            st = jnp.where(kpos <= qpos, st, NEG_INF)
        s_scr[u] = st
        m_prev = m_scr[u, 1]
        m_scr[u, 0] = m_prev
        m_scr[u, 1] = jnp.maximum(m_prev, jnp.max(st, axis=0, keepdims=True))

    def value_unit(j, u):
        h, _ = units[u]
        m_new = m_scr[u, 1]
        alpha = jnp.exp2(m_scr[u, 0] - m_new)
        pt = jnp.exp2((s_scr[u] - m_new).astype(BF16))
        acc_scr[u] = alpha * acc_scr[u] + _mm(vt_scr[j, h], pt)

    def stages(value_tile, score_tile, diag):
        for u in range(len(units)):
            if value_tile is not None:
                value_unit(value_tile, u)
            if score_tile is not None:
                score_unit(score_tile, u, diag)

    @pl.when(i == 0)
    def _():
        stages(None, 0, True)

    @pl.when(i > 0)
    def _():
        stages(None, 0, False)

    def trip(t, c):
        stages(t, t + 1, False)
        return c

    lax.fori_loop(0, i - 1, trip, 0)

    @pl.when(i > 0)
    def _():
        stages(i - 1, i, True)

    stages(i, None, False)

    rows = []
    for h in range(2):
        blocks = [acc_scr[u] for u in range(len(units)) if units[u][0] == h]
        rows.append(jnp.concatenate(
            [a[0:FOX_HEAD_DIM] / a[FOX_HEAD_DIM:FOX_HEAD_DIM + 1] for a in blocks], axis=1))
    o_ref[0] = jnp.concatenate(rows, axis=0).T.astype(BF16)


def _fox_attention(q_aug, k_aug, v):
    bsz, _, seq, _ = q_aug.shape
    nq = seq // Q_TILE
    nk = seq // K_TILE
    return pl.pallas_call(
        _fox_kernel,
        grid=(bsz, FOX_HEADS // 2, nq),
        in_specs=[
            pl.BlockSpec((1, 2, Q_TILE, LANES), lambda b, hp, i: (b, hp, i, 0)),
            pl.BlockSpec((1, 2, seq, LANES), lambda b, hp, i: (b, hp, 0, 0)),
            pl.BlockSpec((1, seq, LANES), lambda b, hp, i: (b, 0, hp)),
        ],
        out_specs=pl.BlockSpec((1, Q_TILE, LANES), lambda b, hp, i: (b, i, hp)),
        out_shape=jax.ShapeDtypeStruct((bsz, seq, FOX_WIDTH), BF16),
        scratch_shapes=[
            pltpu.VMEM((nk, 2, VT_ROWS, K_TILE), BF16),
            pltpu.VMEM((2, LANES, Q_TILE), BF16),
            pltpu.VMEM((2 * (Q_TILE // Q_SUB), VT_ROWS, Q_SUB), F32),
            pltpu.VMEM((2 * (Q_TILE // Q_SUB), K_TILE, Q_SUB), F32),
            pltpu.VMEM((2 * (Q_TILE // Q_SUB), 2, 1, Q_SUB), F32),
        ],
        compiler_params=pltpu.CompilerParams(
            dimension_semantics=("arbitrary", "arbitrary", "arbitrary"), vmem_limit_bytes=VMEM_LIMIT),
        name="fox_attention",
    )(q_aug, k_aug, v)


def _memkv_kernel(mem_ref, wk_ref, wv_ref, k_ref, v_ref):
    mb = mem_ref[0].astype(BF16)
    k_ref[0] = _mm(mb, wk_ref[...]).astype(BF16)
    v_ref[0] = _mm(mb, wv_ref[...]).astype(BF16)


def _mem_kv(mem, w_ck, w_cv):
    bsz = mem.shape[0]
    const2 = lambda b: (0, 0)
    blk = pl.BlockSpec((1, MEM_LEN, D_MODEL), lambda b: (b, 0, 0))
    return pl.pallas_call(
        _memkv_kernel,
        grid=(bsz,),
        in_specs=[blk, pl.BlockSpec(w_ck.shape, const2), pl.BlockSpec(w_cv.shape, const2)],
        out_specs=[blk, blk],
        out_shape=[jax.ShapeDtypeStruct((bsz, MEM_LEN, D_MODEL), BF16)] * 2,
        compiler_params=pltpu.CompilerParams(
            dimension_semantics=("arbitrary",), vmem_limit_bytes=VMEM_LIMIT),
        name="mem_kv",
    )(mem, w_ck, w_cv)


def _mix_cross_kernel(x_ref, uc_ref, o_ref, wout_ref, g1_ref, b1_ref, wcq_ref, kc_ref, vc_ref, wco_ref,
                      g2_ref, b2_ref, h_ref):
    mix = _mm(uc_ref[0], wout_ref[0:CONV_WIDTH, :]) + _mm(o_ref[0], wout_ref[CONV_WIDTH:, :])
    h1 = _layer_norm(DEEPNORM_ALPHA * x_ref[0] + mix, g1_ref[...], b1_ref[...])

    q = (_mm(h1.astype(BF16), wcq_ref[...]) * (1.0 / math.sqrt(MEM_HEAD_DIM))).astype(BF16)
    outs = []
    for hh in range(MEM_HEADS):
        cols = slice(hh * MEM_HEAD_DIM, (hh + 1) * MEM_HEAD_DIM)
        s = _mm_nt(q[:, cols], kc_ref[0, :, cols])
        p = jnp.exp(s - jnp.max(s, axis=-1, keepdims=True))
        l = jnp.sum(p, axis=-1, keepdims=True)
        outs.append((_mm(p.astype(BF16), vc_ref[0, :, cols]) / l).astype(BF16))
    o = jnp.concatenate(outs, axis=-1)
    y = _mm(o, wco_ref[...])
    h_ref[0] = _layer_norm(DEEPNORM_ALPHA * h1 + y, g2_ref[...], b2_ref[...])


def _mix_cross(x, uc, o, w_out, g1, b1, w_cq, kc, vc, w_co, g2, b2):
    bsz, seq, _ = x.shape
    nt = seq // ROW_TILE
    const2 = lambda b, t: (0, 0)
    tok = lambda width: pl.BlockSpec((1, ROW_TILE, width), lambda b, t: (b, t, 0))
    memblk = pl.BlockSpec((1, MEM_LEN, D_MODEL), lambda b, t: (b, 0, 0))
    vec = pl.BlockSpec((1, D_MODEL), const2)
    mat = pl.BlockSpec((D_MODEL, D_MODEL), const2)
    return pl.pallas_call(
        _mix_cross_kernel,
        grid=(bsz, nt),
        in_specs=[tok(D_MODEL), tok(CONV_WIDTH), tok(FOX_WIDTH), mat, vec, vec, mat, memblk, memblk, mat,
                  vec, vec],
        out_specs=tok(D_MODEL),
        out_shape=jax.ShapeDtypeStruct((bsz, seq, D_MODEL), F32),
        compiler_params=pltpu.CompilerParams(
            dimension_semantics=("arbitrary", "arbitrary"), vmem_limit_bytes=VMEM_LIMIT),
        name="mix_cross",
    )(x, uc, o, w_out, g1, b1, w_cq, kc, vc, w_co, g2, b2)


def _swiglu_kernel(h_ref, wg_ref, wu_ref, wd_ref, g_ref, b_ref, out_ref):
    h = h_ref[...]
    hb = h.astype(BF16)
    acc = DEEPNORM_ALPHA * h
    for c in range(D_FF // FF_CHUNK):
        cols = slice(c * FF_CHUNK, (c + 1) * FF_CHUNK)
        gate = _mm(hb, wg_ref[:, cols])
        up = _mm(hb, wu_ref[:, cols])
        act = (gate * _sigmoid(gate) * up).astype(BF16)
        acc = acc + _mm(act, wd_ref[cols, :])
    out_ref[...] = _layer_norm(acc, g_ref[...], b_ref[...])


def _swiglu(h, w_gate, w_up, w_down, g, b):
    n = h.shape[0]
    const2 = lambda t: (0, 0)
    tok = pl.BlockSpec((ROW_TILE, D_MODEL), lambda t: (t, 0))
    return pl.pallas_call(
        _swiglu_kernel,
        grid=(n // ROW_TILE,),
        in_specs=[tok, pl.BlockSpec(w_gate.shape, const2), pl.BlockSpec(w_up.shape, const2),
                  pl.BlockSpec(w_down.shape, const2), pl.BlockSpec((1, D_MODEL), const2),
                  pl.BlockSpec((1, D_MODEL), const2)],
        out_specs=tok,
        out_shape=jax.ShapeDtypeStruct((n, D_MODEL), F32),
        compiler_params=pltpu.CompilerParams(
            dimension_semantics=("arbitrary",), vmem_limit_bytes=VMEM_LIMIT),
        name="swiglu",
    )(h, w_gate, w_up, w_down, g, b)


def _layer(h, mem, w_in, b_forget, conv_w, conv_b, conv_ln_g, conv_ln_b, w_out, ln_mix_g, ln_mix_b,
           w_cq, w_ck, w_cv, w_co, ln_cross_g, ln_cross_b, w_gate, w_up, w_down, ln_ffn_g, ln_ffn_b):
    bsz, seq, _ = h.shape
    row = lambda v: v.reshape(1, -1).astype(F32)
    w_main = w_in[:, :C_F].astype(BF16)
    w_f = jnp.pad(w_in[:, C_F:], ((0, 0), (0, LANES - FOX_HEADS))).astype(BF16)
    b_f = jnp.pad(b_forget.astype(F32), (0, LANES - FOX_HEADS)).reshape(1, LANES)
    tri = jnp.tri(ROW_TILE, dtype=BF16)

    uc, q_aug, k_aug, v = _inproj(h, w_main, w_f, b_f, tri, conv_w.astype(F32), row(conv_b),
                                  row(conv_ln_g), row(conv_ln_b))
    o = _fox_attention(q_aug, k_aug, v)
    kc, vc = _mem_kv(mem, w_ck.astype(BF16), w_cv.astype(BF16))
    h2 = _mix_cross(h, uc, o, w_out.astype(BF16), row(ln_mix_g), row(ln_mix_b), w_cq.astype(BF16), kc, vc,
                    w_co.astype(BF16), row(ln_cross_g), row(ln_cross_b))
    h3 = _swiglu(h2.reshape(bsz * seq, D_MODEL), w_gate.astype(BF16), w_up.astype(BF16),
                 w_down.astype(BF16), row(ln_ffn_g), row(ln_ffn_b))
    return h3.reshape(bsz, seq, D_MODEL)


def kernel(x, mem, w_in, b_forget, conv_w, conv_b, conv_ln_g, conv_ln_b, w_out, ln_mix_g, ln_mix_b,
           w_cq, w_ck, w_cv, w_co, ln_cross_g, ln_cross_b, w_gate, w_up, w_down, ln_ffn_g, ln_ffn_b):
    depth = w_in.shape[0]
    assert depth == 1, "DEEPNORM_ALPHA is fixed for a single layer"
    h = x
    for l in range(depth):
        h = _layer(h, mem, w_in[l], b_forget[l], conv_w[l], conv_b[l], conv_ln_g[l], conv_ln_b[l],
                   w_out[l], ln_mix_g[l], ln_mix_b[l], w_cq[l], w_ck[l], w_cv[l], w_co[l],
                   ln_cross_g[l], ln_cross_b[l], w_gate[l], w_up[l], w_down[l], ln_ffn_g[l], ln_ffn_b[l])
    return h
```

```python
import functools
import math

import jax
import jax.numpy as jnp
from jax import lax
from jax.experimental import pallas as pl
from jax.experimental.pallas import tpu as pltpu

D_MODEL = 1024
CONV_WIDTH = 512
CONV_K = 31
FOX_WIDTH = 512
FOX_HEADS = 8
FOX_HEAD_DIM = 64
MEM_LEN = 256
MEM_HEADS = 4
MEM_HEAD_DIM = 256
D_FF = 2816
LN_EPS = 1e-5
NEG_INF = -1e30
DEEPNORM_ALPHA = 2.0 ** 0.25
LOG2E = math.log2(math.e)

LANES = 128
SUBLANES = 8
HALO = 32
ROW_TILE = 512
Q_TILE = 512
K_TILE = 512
Q_SUB = 256
EXP_ROWS = 64
VT_ROWS = FOX_HEAD_DIM + 16
FF_CHUNK = 256
CONV_ROWS = 32
VMEM_LIMIT = 56 * 1024 * 1024

BF16 = jnp.bfloat16
F32 = jnp.float32

C_GLU_A = 0
C_GLU_B = CONV_WIDTH
C_Q = 2 * CONV_WIDTH
C_K = C_Q + FOX_WIDTH
C_V = C_K + FOX_WIDTH
C_F = C_V + FOX_WIDTH


def _mm(a, b):
    return jnp.dot(a, b, preferred_element_type=F32)


def _mm_nt(a, b):
    return lax.dot_general(a, b, (((1,), (1,)), ((), ())), preferred_element_type=F32)


def _layer_norm(x, g, b):
    mu = jnp.mean(x, axis=-1, keepdims=True)
    xc = x - mu
    var = jnp.mean(xc * xc, axis=-1, keepdims=True)
    return xc * lax.rsqrt(var + LN_EPS) * g + b


def _sigmoid(x):
    return 1.0 / (1.0 + jnp.exp(-x))


def _split3(x):
    hi = x.astype(BF16).astype(F32)
    r = x - hi
    mid = r.astype(BF16).astype(F32)
    lo = (r - mid).astype(BF16).astype(F32)
    return hi, mid, lo


def _pack3(hi, mid, lo):
    return hi + pltpu.roll(mid, 8, axis=1) + pltpu.roll(lo, 16, axis=1)


def _inproj_kernel(x_ref, w_ref, wf_ref, bf_ref, tri_ref, u_ref, q_ref, k_ref, v_ref, carry):
    t = pl.program_id(1)
    rows = x_ref.shape[1]
    xb = x_ref[0].astype(BF16)

    @pl.when(t == 0)
    def _():
        carry[...] = jnp.zeros_like(carry)

    glu_a = _mm(xb, w_ref[:, C_GLU_A:C_GLU_A + CONV_WIDTH])
    glu_b = _mm(xb, w_ref[:, C_GLU_B:C_GLU_B + CONV_WIDTH])
    u_ref[0] = glu_a * _sigmoid(glu_b)

    lane = lax.broadcasted_iota(jnp.int32, (rows, LANES), 1)
    f = _mm(xb, wf_ref[...]) + bf_ref[...]
    logf = jnp.minimum(f, 0.0) - jnp.log(1.0 + jnp.exp(-jnp.abs(f)))
    logf = jnp.where(lane < FOX_HEADS, logf, 0.0)
    packed = _pack3(*_split3(logf)).astype(BF16)
    res = _mm(tri_ref[...], packed)
    cum = res + pltpu.roll(res, LANES - 8, axis=1) + pltpu.roll(res, LANES - 16, axis=1)
    cum = jnp.where(lane < FOX_HEADS, cum + carry[0:1, :], 0.0)
    carry[0:1, :] = cum[rows - 1:rows, :]
    cpack = _pack3(*_split3(cum * LOG2E))

    scale = LOG2E / math.sqrt(FOX_HEAD_DIM)
    for hp in range(FOX_HEADS // 2):
        qpair = _mm(xb, w_ref[:, C_Q + hp * LANES:C_Q + (hp + 1) * LANES]) * scale
        kpair = _mm(xb, w_ref[:, C_K + hp * LANES:C_K + (hp + 1) * LANES])
        for sub in range(2):
            h = 2 * hp + sub
            aug0 = FOX_HEAD_DIM if sub == 0 else 0
            data = (lane < FOX_HEAD_DIM) if sub == 0 else (lane >= FOX_HEAD_DIM)
            slot_a = (lane == aug0) | (lane == aug0 + 8) | (lane == aug0 + 16)
            slot_b = (lane == aug0 + 1) | (lane == aug0 + 9) | (lane == aug0 + 17)
            ra = pltpu.roll(cpack, (aug0 - h) % LANES, axis=1)
            rb = pltpu.roll(cpack, (aug0 + 1 - h) % LANES, axis=1)
            qa = jnp.where(data, qpair, jnp.where(slot_a, ra, jnp.where(slot_b, 1.0, 0.0)))
            ka = jnp.where(data, kpair, jnp.where(slot_a, 1.0, jnp.where(slot_b, -rb, 0.0)))
            q_ref[0, h] = qa.astype(BF16)
            k_ref[0, h] = ka.astype(BF16)
    v_ref[0] = _mm(xb, w_ref[:, C_V:C_V + FOX_WIDTH]).astype(BF16)


def _inproj(x, w_main, w_f, b_f, tri):
    bsz, seq, _ = x.shape
    nt = seq // ROW_TILE
    const2 = lambda b, t: (0, 0)
    return pl.pallas_call(
        _inproj_kernel,
        grid=(bsz, nt),
        in_specs=[
            pl.BlockSpec((1, ROW_TILE, D_MODEL), lambda b, t: (b, t, 0)),
            pl.BlockSpec(w_main.shape, const2),
            pl.BlockSpec(w_f.shape, const2),
            pl.BlockSpec(b_f.shape, const2),
            pl.BlockSpec(tri.shape, const2),
        ],
        out_specs=[
            pl.BlockSpec((1, ROW_TILE, CONV_WIDTH), lambda b, t: (b, t, 0)),
            pl.BlockSpec((1, FOX_HEADS, ROW_TILE, LANES), lambda b, t: (b, 0, t, 0)),
            pl.BlockSpec((1, FOX_HEADS, ROW_TILE, LANES), lambda b, t: (b, 0, t, 0)),
            pl.BlockSpec((1, ROW_TILE, FOX_WIDTH), lambda b, t: (b, t, 0)),
        ],
        out_shape=[
            jax.ShapeDtypeStruct((bsz, seq, CONV_WIDTH), F32),
            jax.ShapeDtypeStruct((bsz, FOX_HEADS, seq, LANES), BF16),
            jax.ShapeDtypeStruct((bsz, FOX_HEADS, seq, LANES), BF16),
            jax.ShapeDtypeStruct((bsz, seq, FOX_WIDTH), BF16),
        ],
        scratch_shapes=[pltpu.VMEM((SUBLANES, LANES), F32)],
        compiler_params=pltpu.CompilerParams(
            dimension_semantics=("arbitrary", "arbitrary"), vmem_limit_bytes=VMEM_LIMIT),
        name="inproj",
    )(x, w_main, w_f, b_f, tri)


def _fox_kernel(q_ref, k_ref, v_ref, o_ref, vt_scr, qt_scr, acc_scr, s_scr, m_scr):
    i = pl.program_id(2)
    nk = vt_scr.shape[0]
    units = [(h, qb) for h in range(2) for qb in range(Q_TILE // Q_SUB)]

    @pl.when(i == 0)
    def _():
        tail = (lax.broadcasted_iota(jnp.int32, (VT_ROWS - FOX_HEAD_DIM, K_TILE), 0) == 0).astype(BF16)
        for c in range(nk):
            blk = v_ref[0, c * K_TILE:(c + 1) * K_TILE, :].astype(F32).T.astype(BF16)
            for h in range(2):
                vt_scr[c, h, 0:FOX_HEAD_DIM, :] = blk[h * FOX_HEAD_DIM:(h + 1) * FOX_HEAD_DIM, :]
                vt_scr[c, h, FOX_HEAD_DIM:VT_ROWS, :] = tail

    for h in range(2):
        qt_scr[h] = q_ref[0, h].astype(F32).T.astype(BF16)
    acc_scr[...] = jnp.zeros(acc_scr.shape, F32)

    m_scr[...] = jnp.full(m_scr.shape, NEG_INF, F32)

    def score_unit(j, u, diag):
        h, qb = units[u]
        kt = k_ref[0, h, pl.ds(pl.multiple_of(j * K_TILE, K_TILE), K_TILE), :]
        st = _mm(kt, qt_scr[h, :, qb * Q_SUB:(qb + 1) * Q_SUB])
        if diag:
            kpos = lax.broadcasted_iota(jnp.int32, st.shape, 0)
            qpos = lax.broadcasted_iota(jnp.int32, st.shape, 1) + qb * Q_SUB
            st = jnp.where(kpos <= qpos, st, NEG_INF)
        s_scr[u] = st
        m_prev = m_scr[u, 1]
        m_scr[u, 0] = m_prev
        m_scr[u, 1] = jnp.maximum(m_prev, jnp.max(st, axis=0, keepdims=True))

    def value_unit(j, u):
        h, _ = units[u]
        m_new = m_scr[u, 1]
        alpha = jnp.exp2(m_scr[u, 0] - m_new)
        pt = jnp.exp2((s_scr[u] - m_new).astype(BF16))
        acc_scr[u] = alpha * acc_scr[u] + _mm(vt_scr[j, h], pt)

    def stages(value_tile, score_tile, diag):
        for u in range(len(units)):
            if value_tile is not None:
                value_unit(value_tile, u)
            if score_tile is not None:
                score_unit(score_tile, u, diag)

    @pl.when(i == 0)
    def _():
        stages(None, 0, True)

    @pl.when(i > 0)
    def _():
        stages(None, 0, False)

    def trip(t, c):
        stages(t, t + 1, False)
        return c

    lax.fori_loop(0, i - 1, trip, 0)

    @pl.when(i > 0)
    def _():
        stages(i - 1, i, True)

    stages(i, None, False)

    rows = []
    for h in range(2):
        blocks = [acc_scr[u] for u in range(len(units)) if units[u][0] == h]
        rows.append(jnp.concatenate(
            [a[0:FOX_HEAD_DIM] / a[FOX_HEAD_DIM:FOX_HEAD_DIM + 1] for a in blocks], axis=1))
    o_ref[0] = jnp.concatenate(rows, axis=0).T.astype(BF16)


def _fox_attention(q_aug, k_aug, v):
    bsz, _, seq, _ = q_aug.shape
    nq = seq // Q_TILE
    nk = seq // K_TILE
    return pl.pallas_call(
        _fox_kernel,
        grid=(bsz, FOX_HEADS // 2, nq),
        in_specs=[
            pl.BlockSpec((1, 2, Q_TILE, LANES), lambda b, hp, i: (b, hp, i, 0)),
            pl.BlockSpec((1, 2, seq, LANES), lambda b, hp, i: (b, hp, 0, 0)),
            pl.BlockSpec((1, seq, LANES), lambda b, hp, i: (b, 0, hp)),
        ],
        out_specs=pl.BlockSpec((1, Q_TILE, LANES), lambda b, hp, i: (b, i, hp)),
        out_shape=jax.ShapeDtypeStruct((bsz, seq, FOX_WIDTH), BF16),
        scratch_shapes=[
            pltpu.VMEM((nk, 2, VT_ROWS, K_TILE), BF16),
            pltpu.VMEM((2, LANES, Q_TILE), BF16),
            pltpu.VMEM((2 * (Q_TILE // Q_SUB), VT_ROWS, Q_SUB), F32),
            pltpu.VMEM((2 * (Q_TILE // Q_SUB), K_TILE, Q_SUB), F32),
            pltpu.VMEM((2 * (Q_TILE // Q_SUB), 2, 1, Q_SUB), F32),
        ],
        compiler_params=pltpu.CompilerParams(
            dimension_semantics=("arbitrary", "arbitrary", "arbitrary"), vmem_limit_bytes=VMEM_LIMIT),
        name="fox_attention",
    )(q_aug, k_aug, v)


def _memkv_kernel(mem_ref, wk_ref, wv_ref, k_ref, v_ref):
    mb = mem_ref[0].astype(BF16)
    k_ref[0] = _mm(mb, wk_ref[...]).astype(BF16)
    v_ref[0] = _mm(mb, wv_ref[...]).astype(BF16)


def _mem_kv(mem, w_ck, w_cv):
    bsz = mem.shape[0]
    const2 = lambda b: (0, 0)
    blk = pl.BlockSpec((1, MEM_LEN, D_MODEL), lambda b: (b, 0, 0))
    return pl.pallas_call(
        _memkv_kernel,
        grid=(bsz,),
        in_specs=[blk, pl.BlockSpec(w_ck.shape, const2), pl.BlockSpec(w_cv.shape, const2)],
        out_specs=[blk, blk],
        out_shape=[jax.ShapeDtypeStruct((bsz, MEM_LEN, D_MODEL), BF16)] * 2,
        compiler_params=pltpu.CompilerParams(
            dimension_semantics=("arbitrary",), vmem_limit_bytes=VMEM_LIMIT),
        name="mem_kv",
    )(mem, w_ck, w_cv)


def _conv_branch(u_ref, cw_ref, cb_ref, cg_ref, cbeta_ref, ubuf, shifted):
    t = pl.program_id(1)
    rows = u_ref.shape[1]
    first = HALO - (CONV_K - 1)
    span = shifted.shape[1]

    @pl.when(t == 0)
    def _():
        ubuf[0:HALO, :] = jnp.zeros((HALO, CONV_WIDTH), F32)

    @pl.when(t > 0)
    def _():
        ubuf[0:HALO, :] = ubuf[rows:rows + HALO, :]

    ubuf[HALO:HALO + rows, :] = u_ref[0]
    for p in range(1, SUBLANES):
        shifted[p - 1] = ubuf[p:p + span, :]

    cw = cw_ref[...]
    cbias = cb_ref[...]
    gam = cg_ref[...]
    beta = cbeta_ref[...]
    outs = []
    for r in range(rows // CONV_ROWS):
        base = r * CONV_ROWS
        acc = jnp.broadcast_to(cbias, (CONV_ROWS, CONV_WIDTH))
        for j in range(CONV_K):
            a, p = divmod(first + j, SUBLANES)
            lo = base + SUBLANES * a
            src = ubuf[lo:lo + CONV_ROWS, :] if p == 0 else shifted[p - 1, lo:lo + CONV_ROWS, :]
            acc = acc + cw[j:j + 1, :] * src
        y = _layer_norm(acc, gam, beta)
        outs.append((y * _sigmoid(y)).astype(BF16))
    return jnp.concatenate(outs, axis=0)


def _mix_cross_kernel(x_ref, u_ref, o_ref, cw_ref, cb_ref, cg_ref, cbeta_ref, wout_ref, g1_ref, b1_ref,
                      wcq_ref, kc_ref, vc_ref, wco_ref, g2_ref, b2_ref, h_ref, ubuf, shifted):
    uc = _conv_branch(u_ref, cw_ref, cb_ref, cg_ref, cbeta_ref, ubuf, shifted)
    mix = _mm(uc, wout_ref[0:CONV_WIDTH, :]) + _mm(o_ref[0], wout_ref[CONV_WIDTH:, :])
    h1 = _layer_norm(DEEPNORM_ALPHA * x_ref[0] + mix, g1_ref[...], b1_ref[...])

    q = (_mm(h1.astype(BF16), wcq_ref[...]) * (1.0 / math.sqrt(MEM_HEAD_DIM))).astype(BF16)
    outs = []
    for hh in range(MEM_HEADS):
        cols = slice(hh * MEM_HEAD_DIM, (hh + 1) * MEM_HEAD_DIM)
        s = _mm_nt(q[:, cols], kc_ref[0, :, cols])
        p = jnp.exp(s - jnp.max(s, axis=-1, keepdims=True))
        l = jnp.sum(p, axis=-1, keepdims=True)
        outs.append((_mm(p.astype(BF16), vc_ref[0, :, cols]) / l).astype(BF16))
    o = jnp.concatenate(outs, axis=-1)
    y = _mm(o, wco_ref[...])
    h_ref[0] = _layer_norm(DEEPNORM_ALPHA * h1 + y, g2_ref[...], b2_ref[...])


def _mix_cross(x, u, o, conv_w, conv_b, conv_g, conv_beta, w_out, g1, b1, w_cq, kc, vc, w_co, g2, b2):
    bsz, seq, _ = x.shape
    nt = seq // ROW_TILE
    const2 = lambda b, t: (0, 0)
    tok = lambda width: pl.BlockSpec((1, ROW_TILE, width), lambda b, t: (b, t, 0))
    memblk = pl.BlockSpec((1, MEM_LEN, D_MODEL), lambda b, t: (b, 0, 0))
    vec = pl.BlockSpec((1, D_MODEL), const2)
    cvec = pl.BlockSpec((1, CONV_WIDTH), const2)
    mat = pl.BlockSpec((D_MODEL, D_MODEL), const2)
    return pl.pallas_call(
        _mix_cross_kernel,
        grid=(bsz, nt),
        in_specs=[tok(D_MODEL), tok(CONV_WIDTH), tok(FOX_WIDTH), pl.BlockSpec(conv_w.shape, const2), cvec, cvec,
                  cvec, mat, vec, vec, mat, memblk, memblk, mat, vec, vec],
        out_specs=tok(D_MODEL),
        out_shape=jax.ShapeDtypeStruct((bsz, seq, D_MODEL), F32),
        scratch_shapes=[
            pltpu.VMEM((ROW_TILE + HALO, CONV_WIDTH), F32),
            pltpu.VMEM((SUBLANES - 1, ROW_TILE + HALO - SUBLANES, CONV_WIDTH), F32),
        ],
        compiler_params=pltpu.CompilerParams(
            dimension_semantics=("arbitrary", "arbitrary"), vmem_limit_bytes=VMEM_LIMIT),
        name="mix_cross",
    )(x, u, o, conv_w, conv_b, conv_g, conv_beta, w_out, g1, b1, w_cq, kc, vc, w_co, g2, b2)


def _swiglu_kernel(h_ref, wg_ref, wu_ref, wd_ref, g_ref, b_ref, out_ref):
    h = h_ref[...]
    hb = h.astype(BF16)
    acc = DEEPNORM_ALPHA * h
    for c in range(D_FF // FF_CHUNK):
        cols = slice(c * FF_CHUNK, (c + 1) * FF_CHUNK)
        gate = _mm(hb, wg_ref[:, cols])
        up = _mm(hb, wu_ref[:, cols])
        act = (gate * _sigmoid(gate) * up).astype(BF16)
        acc = acc + _mm(act, wd_ref[cols, :])
    out_ref[...] = _layer_norm(acc, g_ref[...], b_ref[...])


def _swiglu(h, w_gate, w_up, w_down, g, b):
    n = h.shape[0]
    const2 = lambda t: (0, 0)
    tok = pl.BlockSpec((ROW_TILE, D_MODEL), lambda t: (t, 0))
    return pl.pallas_call(
        _swiglu_kernel,
        grid=(n // ROW_TILE,),
        in_specs=[tok, pl.BlockSpec(w_gate.shape, const2), pl.BlockSpec(w_up.shape, const2),
                  pl.BlockSpec(w_down.shape, const2), pl.BlockSpec((1, D_MODEL), const2),
                  pl.BlockSpec((1, D_MODEL), const2)],
        out_specs=tok,
        out_shape=jax.ShapeDtypeStruct((n, D_MODEL), F32),
        compiler_params=pltpu.CompilerParams(
            dimension_semantics=("arbitrary",), vmem_limit_bytes=VMEM_LIMIT),
        name="swiglu",
    )(h, w_gate, w_up, w_down, g, b)


def _layer(h, mem, w_in, b_forget, conv_w, conv_b, conv_ln_g, conv_ln_b, w_out, ln_mix_g, ln_mix_b,
           w_cq, w_ck, w_cv, w_co, ln_cross_g, ln_cross_b, w_gate, w_up, w_down, ln_ffn_g, ln_ffn_b):
    bsz, seq, _ = h.shape
    row = lambda v: v.reshape(1, -1).astype(F32)
    w_main = w_in[:, :C_F].astype(BF16)
    w_f = jnp.pad(w_in[:, C_F:], ((0, 0), (0, LANES - FOX_HEADS))).astype(BF16)
    b_f = jnp.pad(b_forget.astype(F32), (0, LANES - FOX_HEADS)).reshape(1, LANES)
    tri = jnp.tri(ROW_TILE, dtype=BF16)

    u, q_aug, k_aug, v = _inproj(h, w_main, w_f, b_f, tri)
    o = _fox_attention(q_aug, k_aug, v)
    kc, vc = _mem_kv(mem, w_ck.astype(BF16), w_cv.astype(BF16))
    h2 = _mix_cross(h, u, o, conv_w.astype(F32), row(conv_b), row(conv_ln_g), row(conv_ln_b),
                    w_out.astype(BF16), row(ln_mix_g), row(ln_mix_b), w_cq.astype(BF16), kc, vc,
                    w_co.astype(BF16), row(ln_cross_g), row(ln_cross_b))
    h3 = _swiglu(h2.reshape(bsz * seq, D_MODEL), w_gate.astype(BF16), w_up.astype(BF16),
                 w_down.astype(BF16), row(ln_ffn_g), row(ln_ffn_b))
    return h3.reshape(bsz, seq, D_MODEL)


def kernel(x, mem, w_in, b_forget, conv_w, conv_b, conv_ln_g, conv_ln_b, w_out, ln_mix_g, ln_mix_b,
           w_cq, w_ck, w_cv, w_co, ln_cross_g, ln_cross_b, w_gate, w_up, w_down, ln_ffn_g, ln_ffn_b):
    depth = w_in.shape[0]
    assert depth == 1, "DEEPNORM_ALPHA is fixed for a single layer"
    h = x
    for l in range(depth):
        h = _layer(h, mem, w_in[l], b_forget[l], conv_w[l], conv_b[l], conv_ln_g[l], conv_ln_b[l],
                   w_out[l], ln_mix_g[l], ln_mix_b[l], w_cq[l], w_ck[l], w_cv[l], w_co[l],
                   ln_cross_g[l], ln_cross_b[l], w_gate[l], w_up[l], w_down[l], ln_ffn_g[l], ln_ffn_b[l])
    return h
```

```python
import functools
import math

import jax
import jax.numpy as jnp
from jax import lax
from jax.experimental import pallas as pl
from jax.experimental.pallas import tpu as pltpu

D_MODEL = 1024
CONV_WIDTH = 512
CONV_K = 31
FOX_WIDTH = 512
FOX_HEADS = 8
FOX_HEAD_DIM = 64
MEM_LEN = 256
MEM_HEADS = 4
MEM_HEAD_DIM = 256
D_FF = 2816
LN_EPS = 1e-5
NEG_INF = -1e30
DEEPNORM_ALPHA = 2.0 ** 0.25
LOG2E = math.log2(math.e)

LANES = 128
SUBLANES = 8
HALO = 32
ROW_TILE = 512
Q_TILE = 512
K_TILE = 512
Q_SUB = 256
EXP_ROWS = 64
VT_ROWS = FOX_HEAD_DIM + 16
FF_CHUNK = 256
CONV_ROWS = 32
VMEM_LIMIT = 56 * 1024 * 1024

BF16 = jnp.bfloat16
F32 = jnp.float32

C_GLU_A = 0
C_GLU_B = CONV_WIDTH
C_Q = 2 * CONV_WIDTH
C_K = C_Q + FOX_WIDTH
C_V = C_K + FOX_WIDTH
C_F = C_V + FOX_WIDTH


def _mm(a, b):
    return jnp.dot(a, b, preferred_element_type=F32)


def _mm_nt(a, b):
    return lax.dot_general(a, b, (((1,), (1,)), ((), ())), preferred_element_type=F32)


def _layer_norm(x, g, b):
    mu = jnp.mean(x, axis=-1, keepdims=True)
    xc = x - mu
    var = jnp.mean(xc * xc, axis=-1, keepdims=True)
    return xc * lax.rsqrt(var + LN_EPS) * g + b


def _sigmoid(x):
    return 1.0 / (1.0 + jnp.exp(-x))


def _split3(x):
    hi = x.astype(BF16).astype(F32)
    r = x - hi
    mid = r.astype(BF16).astype(F32)
    lo = (r - mid).astype(BF16).astype(F32)
    return hi, mid, lo


def _pack3(hi, mid, lo):
    return hi + pltpu.roll(mid, 8, axis=1) + pltpu.roll(lo, 16, axis=1)


def _inproj_kernel(x_ref, w_ref, wf_ref, bf_ref, tri_ref, u_ref, q_ref, k_ref, v_ref, carry):
    t = pl.program_id(1)
    rows = x_ref.shape[1]
    xb = x_ref[0].astype(BF16)

    @pl.when(t == 0)
    def _():
        carry[...] = jnp.zeros_like(carry)

    glu_a = _mm(xb, w_ref[:, C_GLU_A:C_GLU_A + CONV_WIDTH])
    glu_b = _mm(xb, w_ref[:, C_GLU_B:C_GLU_B + CONV_WIDTH])
    u_ref[0] = glu_a * _sigmoid(glu_b)

    lane = lax.broadcasted_iota(jnp.int32, (rows, LANES), 1)
    f = _mm(xb, wf_ref[...]) + bf_ref[...]
    logf = jnp.minimum(f, 0.0) - jnp.log(1.0 + jnp.exp(-jnp.abs(f)))
    logf = jnp.where(lane < FOX_HEADS, logf, 0.0)
    packed = _pack3(*_split3(logf)).astype(BF16)
    res = _mm(tri_ref[...], packed)
    cum = res + pltpu.roll(res, LANES - 8, axis=1) + pltpu.roll(res, LANES - 16, axis=1)
    cum = jnp.where(lane < FOX_HEADS, cum + carry[0:1, :], 0.0)
    carry[0:1, :] = cum[rows - 1:rows, :]
    cpack = _pack3(*_split3(cum * LOG2E))

    scale = LOG2E / math.sqrt(FOX_HEAD_DIM)
    for hp in range(FOX_HEADS // 2):
        qpair = _mm(xb, w_ref[:, C_Q + hp * LANES:C_Q + (hp + 1) * LANES]) * scale
        kpair = _mm(xb, w_ref[:, C_K + hp * LANES:C_K + (hp + 1) * LANES])
        for sub in range(2):
            h = 2 * hp + sub
            aug0 = FOX_HEAD_DIM if sub == 0 else 0
            data = (lane < FOX_HEAD_DIM) if sub == 0 else (lane >= FOX_HEAD_DIM)
            slot_a = (lane == aug0) | (lane == aug0 + 8) | (lane == aug0 + 16)
            slot_b = (lane == aug0 + 1) | (lane == aug0 + 9) | (lane == aug0 + 17)
            ra = pltpu.roll(cpack, (aug0 - h) % LANES, axis=1)
            rb = pltpu.roll(cpack, (aug0 + 1 - h) % LANES, axis=1)
            qa = jnp.where(data, qpair, jnp.where(slot_a, ra, jnp.where(slot_b, 1.0, 0.0)))
            ka = jnp.where(data, kpair, jnp.where(slot_a, 1.0, jnp.where(slot_b, -rb, 0.0)))
            q_ref[0, h] = qa.astype(BF16)
            k_ref[0, h] = ka.astype(BF16)
    v_ref[0] = _mm(xb, w_ref[:, C_V:C_V + FOX_WIDTH]).astype(BF16)


def _inproj(x, w_main, w_f, b_f, tri):
    bsz, seq, _ = x.shape
    nt = seq // ROW_TILE
    const2 = lambda b, t: (0, 0)
    return pl.pallas_call(
        _inproj_kernel,
        grid=(bsz, nt),
        in_specs=[
            pl.BlockSpec((1, ROW_TILE, D_MODEL), lambda b, t: (b, t, 0)),
            pl.BlockSpec(w_main.shape, const2),
            pl.BlockSpec(w_f.shape, const2),
            pl.BlockSpec(b_f.shape, const2),
            pl.BlockSpec(tri.shape, const2),
        ],
        out_specs=[
            pl.BlockSpec((1, ROW_TILE, CONV_WIDTH), lambda b, t: (b, t, 0)),
            pl.BlockSpec((1, FOX_HEADS, ROW_TILE, LANES), lambda b, t: (b, 0, t, 0)),
            pl.BlockSpec((1, FOX_HEADS, ROW_TILE, LANES), lambda b, t: (b, 0, t, 0)),
            pl.BlockSpec((1, ROW_TILE, FOX_WIDTH), lambda b, t: (b, t, 0)),
        ],
        out_shape=[
            jax.ShapeDtypeStruct((bsz, seq, CONV_WIDTH), F32),
            jax.ShapeDtypeStruct((bsz, FOX_HEADS, seq, LANES), BF16),
            jax.ShapeDtypeStruct((bsz, FOX_HEADS, seq, LANES), BF16),
            jax.ShapeDtypeStruct((bsz, seq, FOX_WIDTH), BF16),
        ],
        scratch_shapes=[pltpu.VMEM((SUBLANES, LANES), F32)],
        compiler_params=pltpu.CompilerParams(
            dimension_semantics=("arbitrary", "arbitrary"), vmem_limit_bytes=VMEM_LIMIT),
        name="inproj",
    )(x, w_main, w_f, b_f, tri)


def _fox_kernel(q_ref, qn_ref, k_ref, v_ref, o_ref, vt_scr, qt_scr, qtn_scr, acc_scr, s_scr, m_scr):
    i = pl.program_id(2)
    nq = pl.num_programs(2)
    nk = vt_scr.shape[0]
    units = [(h, qb) for h in range(2) for qb in range(Q_TILE // Q_SUB)]

    def transpose_q(src_ref, dst):
        for h in range(2):
            dst[h] = src_ref[0, h].astype(F32).T.astype(BF16)

    def score_unit(j, u, diag, fresh, qt):
        h, qb = units[u]
        nkeys = (qb + 1) * Q_SUB if diag else K_TILE
        kt = k_ref[0, h, pl.ds(pl.multiple_of(j * K_TILE, K_TILE), nkeys), :]
        st = _mm(kt, qt[h, :, qb * Q_SUB:(qb + 1) * Q_SUB])
        if diag:
            kpos = lax.broadcasted_iota(jnp.int32, st.shape, 0)
            qpos = lax.broadcasted_iota(jnp.int32, st.shape, 1) + qb * Q_SUB
            st = jnp.where(kpos <= qpos, st, NEG_INF)
        s_scr[u, 0:nkeys, :] = st
        m_prev = jnp.full((1, Q_SUB), NEG_INF, F32) if fresh else m_scr[u, 1]
        m_scr[u, 0] = m_prev
        m_scr[u, 1] = jnp.maximum(m_prev, jnp.max(st, axis=0, keepdims=True))

    def value_unit(j, u, diag):
        h, qb = units[u]
        nkeys = (qb + 1) * Q_SUB if diag else K_TILE
        m_new = m_scr[u, 1]
        alpha = jnp.exp2(m_scr[u, 0] - m_new)
        pt = jnp.exp2((s_scr[u, 0:nkeys, :] - m_new).astype(BF16))
        acc_scr[u] = alpha * acc_scr[u] + _mm(vt_scr[j, h, :, 0:nkeys], pt)

    def stages(value_tile, score_tile, value_diag=False, score_diag=False, fresh=False, qt=qt_scr):
        for u in range(len(units)):
            if value_tile is not None:
                value_unit(value_tile, u, value_diag)
            if score_tile is not None:
                score_unit(score_tile, u, score_diag, fresh, qt)

    @pl.when(i == 0)
    def _():
        tail = (lax.broadcasted_iota(jnp.int32, (VT_ROWS - FOX_HEAD_DIM, K_TILE), 0) == 0).astype(BF16)
        for c in range(nk):
            blk = v_ref[0, c * K_TILE:(c + 1) * K_TILE, :].astype(F32).T.astype(BF16)
            for h in range(2):
                vt_scr[c, h, 0:FOX_HEAD_DIM, :] = blk[h * FOX_HEAD_DIM:(h + 1) * FOX_HEAD_DIM, :]
                vt_scr[c, h, FOX_HEAD_DIM:VT_ROWS, :] = tail

    transpose_q(q_ref, qt_scr)
    acc_scr[...] = jnp.zeros(acc_scr.shape, F32)

    @pl.when(i == 0)
    def _():
        stages(None, 0, score_diag=True, fresh=True)

    def trip(t, c):
        stages(t, t + 1)
        return c

    lax.fori_loop(0, i - 1, trip, 0)

    @pl.when(i > 0)
    def _():
        stages(i - 1, i, score_diag=True)

    @pl.when(i < nq - 1)
    def _():
        transpose_q(qn_ref, qtn_scr)
        stages(i, 0, value_diag=True, fresh=True, qt=qtn_scr)

    @pl.when(i == nq - 1)
    def _():
        stages(i, None, value_diag=True)

    rows = []
    for h in range(2):
        blocks = [acc_scr[u] for u in range(len(units)) if units[u][0] == h]
        rows.append(jnp.concatenate(
            [a[0:FOX_HEAD_DIM] / a[FOX_HEAD_DIM:FOX_HEAD_DIM + 1] for a in blocks], axis=1))
    o_ref[0] = jnp.concatenate(rows, axis=0).T.astype(BF16)


def _fox_attention(q_aug, k_aug, v):
    bsz, _, seq, _ = q_aug.shape
    nq = seq // Q_TILE
    nk = seq // K_TILE
    return pl.pallas_call(
        _fox_kernel,
        grid=(bsz, FOX_HEADS // 2, nq),
        in_specs=[
            pl.BlockSpec((1, 2, Q_TILE, LANES), lambda b, hp, i: (b, hp, i, 0)),
            pl.BlockSpec((1, 2, Q_TILE, LANES), lambda b, hp, i: (b, hp, jnp.minimum(i + 1, nq - 1), 0)),
            pl.BlockSpec((1, 2, seq, LANES), lambda b, hp, i: (b, hp, 0, 0)),
            pl.BlockSpec((1, seq, LANES), lambda b, hp, i: (b, 0, hp)),
        ],
        out_specs=pl.BlockSpec((1, Q_TILE, LANES), lambda b, hp, i: (b, i, hp)),
        out_shape=jax.ShapeDtypeStruct((bsz, seq, FOX_WIDTH), BF16),
        scratch_shapes=[
            pltpu.VMEM((nk, 2, VT_ROWS, K_TILE), BF16),
            pltpu.VMEM((2, LANES, Q_TILE), BF16),
            pltpu.VMEM((2, LANES, Q_TILE), BF16),
            pltpu.VMEM((2 * (Q_TILE // Q_SUB), VT_ROWS, Q_SUB), F32),
            pltpu.VMEM((2 * (Q_TILE // Q_SUB), K_TILE, Q_SUB), F32),
            pltpu.VMEM((2 * (Q_TILE // Q_SUB), 2, 1, Q_SUB), F32),
        ],
        compiler_params=pltpu.CompilerParams(
            dimension_semantics=("arbitrary", "arbitrary", "arbitrary"), vmem_limit_bytes=VMEM_LIMIT),
        name="fox_attention",
    )(q_aug, q_aug, k_aug, v)


def _memkv_kernel(mem_ref, wk_ref, wv_ref, k_ref, v_ref):
    mb = mem_ref[0].astype(BF16)
    k_ref[0] = _mm(mb, wk_ref[...]).astype(BF16)
    v_ref[0] = _mm(mb, wv_ref[...]).astype(BF16)


def _mem_kv(mem, w_ck, w_cv):
    bsz = mem.shape[0]
    const2 = lambda b: (0, 0)
    blk = pl.BlockSpec((1, MEM_LEN, D_MODEL), lambda b: (b, 0, 0))
    return pl.pallas_call(
        _memkv_kernel,
        grid=(bsz,),
        in_specs=[blk, pl.BlockSpec(w_ck.shape, const2), pl.BlockSpec(w_cv.shape, const2)],
        out_specs=[blk, blk],
        out_shape=[jax.ShapeDtypeStruct((bsz, MEM_LEN, D_MODEL), BF16)] * 2,
        compiler_params=pltpu.CompilerParams(
            dimension_semantics=("arbitrary",), vmem_limit_bytes=VMEM_LIMIT),
        name="mem_kv",
    )(mem, w_ck, w_cv)


def _conv_branch(u_ref, cw_ref, cb_ref, cg_ref, cbeta_ref, ubuf, shifted):
    t = pl.program_id(1)
    rows = u_ref.shape[1]
    first = HALO - (CONV_K - 1)
    span = shifted.shape[1]

    @pl.when(t == 0)
    def _():
        ubuf[0:HALO, :] = jnp.zeros((HALO, CONV_WIDTH), F32)

    @pl.when(t > 0)
    def _():
        ubuf[0:HALO, :] = ubuf[rows:rows + HALO, :]

    ubuf[HALO:HALO + rows, :] = u_ref[0]
    for p in range(1, SUBLANES):
        shifted[p - 1] = ubuf[p:p + span, :]

    cw = cw_ref[...]
    cbias = cb_ref[...]
    gam = cg_ref[...]
    beta = cbeta_ref[...]
    outs = []
    for r in range(rows // CONV_ROWS):
        base = r * CONV_ROWS
        acc = jnp.broadcast_to(cbias, (CONV_ROWS, CONV_WIDTH))
        for j in range(CONV_K):
            a, p = divmod(first + j, SUBLANES)
            lo = base + SUBLANES * a
            src = ubuf[lo:lo + CONV_ROWS, :] if p == 0 else shifted[p - 1, lo:lo + CONV_ROWS, :]
            acc = acc + cw[j:j + 1, :] * src
        y = _layer_norm(acc, gam, beta)
        outs.append((y * _sigmoid(y)).astype(BF16))
    return jnp.concatenate(outs, axis=0)


def _mix_cross_kernel(x_ref, u_ref, o_ref, cw_ref, cb_ref, cg_ref, cbeta_ref, wout_ref, g1_ref, b1_ref,
                      wcq_ref, kc_ref, vc_ref, wco_ref, g2_ref, b2_ref, h_ref, ubuf, shifted):
    uc = _conv_branch(u_ref, cw_ref, cb_ref, cg_ref, cbeta_ref, ubuf, shifted)
    mix = _mm(uc, wout_ref[0:CONV_WIDTH, :]) + _mm(o_ref[0], wout_ref[CONV_WIDTH:, :])
    h1 = _layer_norm(DEEPNORM_ALPHA * x_ref[0] + mix, g1_ref[...], b1_ref[...])

    q = (_mm(h1.astype(BF16), wcq_ref[...]) * (1.0 / math.sqrt(MEM_HEAD_DIM))).astype(BF16)
    outs = []
    for hh in range(MEM_HEADS):
        cols = slice(hh * MEM_HEAD_DIM, (hh + 1) * MEM_HEAD_DIM)
        s = _mm_nt(q[:, cols], kc_ref[0, :, cols])
        p = jnp.exp(s - jnp.max(s, axis=-1, keepdims=True))
        l = jnp.sum(p, axis=-1, keepdims=True)
        outs.append((_mm(p.astype(BF16), vc_ref[0, :, cols]) / l).astype(BF16))
    o = jnp.concatenate(outs, axis=-1)
    y = _mm(o, wco_ref[...])
    h_ref[0] = _layer_norm(DEEPNORM_ALPHA * h1 + y, g2_ref[...], b2_ref[...])


def _mix_cross(x, u, o, conv_w, conv_b, conv_g, conv_beta, w_out, g1, b1, w_cq, kc, vc, w_co, g2, b2):
    bsz, seq, _ = x.shape
    nt = seq // ROW_TILE
    const2 = lambda b, t: (0, 0)
    tok = lambda width: pl.BlockSpec((1, ROW_TILE, width), lambda b, t: (b, t, 0))
    memblk = pl.BlockSpec((1, MEM_LEN, D_MODEL), lambda b, t: (b, 0, 0))
    vec = pl.BlockSpec((1, D_MODEL), const2)
    cvec = pl.BlockSpec((1, CONV_WIDTH), const2)
    mat = pl.BlockSpec((D_MODEL, D_MODEL), const2)
    return pl.pallas_call(
        _mix_cross_kernel,
        grid=(bsz, nt),
        in_specs=[tok(D_MODEL), tok(CONV_WIDTH), tok(FOX_WIDTH), pl.BlockSpec(conv_w.shape, const2), cvec, cvec,
                  cvec, mat, vec, vec, mat, memblk, memblk, mat, vec, vec],
        out_specs=tok(D_MODEL),
        out_shape=jax.ShapeDtypeStruct((bsz, seq, D_MODEL), F32),
        scratch_shapes=[
            pltpu.VMEM((ROW_TILE + HALO, CONV_WIDTH), F32),
            pltpu.VMEM((SUBLANES - 1, ROW_TILE + HALO - SUBLANES, CONV_WIDTH), F32),
        ],
        compiler_params=pltpu.CompilerParams(
            dimension_semantics=("arbitrary", "arbitrary"), vmem_limit_bytes=VMEM_LIMIT),
        name="mix_cross",
    )(x, u, o, conv_w, conv_b, conv_g, conv_beta, w_out, g1, b1, w_cq, kc, vc, w_co, g2, b2)


def _swiglu_kernel(h_ref, wg_ref, wu_ref, wd_ref, g_ref, b_ref, out_ref):
    h = h_ref[...]
    hb = h.astype(BF16)
    acc = DEEPNORM_ALPHA * h
    for c in range(D_FF // FF_CHUNK):
        cols = slice(c * FF_CHUNK, (c + 1) * FF_CHUNK)
        gate = _mm(hb, wg_ref[:, cols])
        up = _mm(hb, wu_ref[:, cols])
        act = (gate * _sigmoid(gate) * up).astype(BF16)
        acc = acc + _mm(act, wd_ref[cols, :])
    out_ref[...] = _layer_norm(acc, g_ref[...], b_ref[...])


def _swiglu(h, w_gate, w_up, w_down, g, b):
    n = h.shape[0]
    const2 = lambda t: (0, 0)
    tok = pl.BlockSpec((ROW_TILE, D_MODEL), lambda t: (t, 0))
    return pl.pallas_call(
        _swiglu_kernel,
        grid=(n // ROW_TILE,),
        in_specs=[tok, pl.BlockSpec(w_gate.shape, const2), pl.BlockSpec(w_up.shape, const2),
                  pl.BlockSpec(w_down.shape, const2), pl.BlockSpec((1, D_MODEL), const2),
                  pl.BlockSpec((1, D_MODEL), const2)],
        out_specs=tok,
        out_shape=jax.ShapeDtypeStruct((n, D_MODEL), F32),
        compiler_params=pltpu.CompilerParams(
            dimension_semantics=("arbitrary",), vmem_limit_bytes=VMEM_LIMIT),
        name="swiglu",
    )(h, w_gate, w_up, w_down, g, b)


def _layer(h, mem, w_in, b_forget, conv_w, conv_b, conv_ln_g, conv_ln_b, w_out, ln_mix_g, ln_mix_b,
           w_cq, w_ck, w_cv, w_co, ln_cross_g, ln_cross_b, w_gate, w_up, w_down, ln_ffn_g, ln_ffn_b):
    bsz, seq, _ = h.shape
    row = lambda v: v.reshape(1, -1).astype(F32)
    w_main = w_in[:, :C_F].astype(BF16)
    w_f = jnp.pad(w_in[:, C_F:], ((0, 0), (0, LANES - FOX_HEADS))).astype(BF16)
    b_f = jnp.pad(b_forget.astype(F32), (0, LANES - FOX_HEADS)).reshape(1, LANES)
    tri = jnp.tri(ROW_TILE, dtype=BF16)

    u, q_aug, k_aug, v = _inproj(h, w_main, w_f, b_f, tri)
    o = _fox_attention(q_aug, k_aug, v)
    kc, vc = _mem_kv(mem, w_ck.astype(BF16), w_cv.astype(BF16))
    h2 = _mix_cross(h, u, o, conv_w.astype(F32), row(conv_b), row(conv_ln_g), row(conv_ln_b),
                    w_out.astype(BF16), row(ln_mix_g), row(ln_mix_b), w_cq.astype(BF16), kc, vc,
                    w_co.astype(BF16), row(ln_cross_g), row(ln_cross_b))
    h3 = _swiglu(h2.reshape(bsz * seq, D_MODEL), w_gate.astype(BF16), w_up.astype(BF16),
                 w_down.astype(BF16), row(ln_ffn_g), row(ln_ffn_b))
    return h3.reshape(bsz, seq, D_MODEL)


def kernel(x, mem, w_in, b_forget, conv_w, conv_b, conv_ln_g, conv_ln_b, w_out, ln_mix_g, ln_mix_b,
           w_cq, w_ck, w_cv, w_co, ln_cross_g, ln_cross_b, w_gate, w_up, w_down, ln_ffn_g, ln_ffn_b):
    depth = w_in.shape[0]
    assert depth == 1, "DEEPNORM_ALPHA is fixed for a single layer"
    h = x
    for l in range(depth):
        h = _layer(h, mem, w_in[l], b_forget[l], conv_w[l], conv_b[l], conv_ln_g[l], conv_ln_b[l],
                   w_out[l], ln_mix_g[l], ln_mix_b[l], w_cq[l], w_ck[l], w_cv[l], w_co[l],
                   ln_cross_g[l], ln_cross_b[l], w_gate[l], w_up[l], w_down[l], ln_ffn_g[l], ln_ffn_b[l])
    return h
```

```python
import functools
import math

import jax
import jax.numpy as jnp
from jax import lax
from jax.experimental import pallas as pl
from jax.experimental.pallas import tpu as pltpu

D_MODEL = 1024
CONV_WIDTH = 512
CONV_K = 31
FOX_WIDTH = 512
FOX_HEADS = 8
FOX_HEAD_DIM = 64
MEM_LEN = 256
MEM_HEADS = 4
MEM_HEAD_DIM = 256
D_FF = 2816
LN_EPS = 1e-5
NEG_INF = -1e30
DEEPNORM_ALPHA = 2.0 ** 0.25
LOG2E = math.log2(math.e)

LANES = 128
SUBLANES = 8
HALO = 32
ROW_TILE = 512
FF_ROW_TILE = 1024
Q_TILE = 512
K_TILE = 512
Q_SUB = 256
EXP_ROWS = 64
VT_ROWS = FOX_HEAD_DIM + 16
FF_CHUNK = 256
CONV_ROWS = 32
VMEM_LIMIT = 56 * 1024 * 1024

BF16 = jnp.bfloat16
F32 = jnp.float32

C_GLU_A = 0
C_GLU_B = CONV_WIDTH
C_Q = 2 * CONV_WIDTH
C_K = C_Q + FOX_WIDTH
C_V = C_K + FOX_WIDTH
C_F = C_V + FOX_WIDTH


def _mm(a, b):
    return jnp.dot(a, b, preferred_element_type=F32)


def _mm_nt(a, b):
    return lax.dot_general(a, b, (((1,), (1,)), ((), ())), preferred_element_type=F32)


def _layer_norm(x, g, b):
    mu = jnp.mean(x, axis=-1, keepdims=True)
    xc = x - mu
    var = jnp.mean(xc * xc, axis=-1, keepdims=True)
    return xc * lax.rsqrt(var + LN_EPS) * g + b


def _sigmoid(x):
    return 1.0 / (1.0 + jnp.exp(-x))


def _split3(x):
    hi = x.astype(BF16).astype(F32)
    r = x - hi
    mid = r.astype(BF16).astype(F32)
    lo = (r - mid).astype(BF16).astype(F32)
    return hi, mid, lo


def _pack3(hi, mid, lo):
    return hi + pltpu.roll(mid, 8, axis=1) + pltpu.roll(lo, 16, axis=1)


def _inproj_kernel(x_ref, w_ref, wf_ref, bf_ref, tri_ref, u_ref, q_ref, k_ref, v_ref, carry):
    t = pl.program_id(1)
    rows = x_ref.shape[1]
    xb = x_ref[0].astype(BF16)

    @pl.when(t == 0)
    def _():
        carry[...] = jnp.zeros_like(carry)

    glu_a = _mm(xb, w_ref[:, C_GLU_A:C_GLU_A + CONV_WIDTH])
    glu_b = _mm(xb, w_ref[:, C_GLU_B:C_GLU_B + CONV_WIDTH])
    u_ref[0] = glu_a * _sigmoid(glu_b)

    lane = lax.broadcasted_iota(jnp.int32, (rows, LANES), 1)
    f = _mm(xb, wf_ref[...]) + bf_ref[...]
    logf = jnp.minimum(f, 0.0) - jnp.log(1.0 + jnp.exp(-jnp.abs(f)))
    logf = jnp.where(lane < FOX_HEADS, logf, 0.0)
    packed = _pack3(*_split3(logf)).astype(BF16)
    res = _mm(tri_ref[...], packed)
    cum = res + pltpu.roll(res, LANES - 8, axis=1) + pltpu.roll(res, LANES - 16, axis=1)
    cum = jnp.where(lane < FOX_HEADS, cum + carry[0:1, :], 0.0)
    carry[0:1, :] = cum[rows - 1:rows, :]
    cpack = _pack3(*_split3(cum * LOG2E))

    scale = LOG2E / math.sqrt(FOX_HEAD_DIM)
    for hp in range(FOX_HEADS // 2):
        if hp % 2 == 0:
            qquad = _mm(xb, w_ref[:, C_Q + hp * LANES:C_Q + (hp + 2) * LANES]) * scale
            kquad = _mm(xb, w_ref[:, C_K + hp * LANES:C_K + (hp + 2) * LANES])
        qpair = qquad[:, (hp % 2) * LANES:(hp % 2 + 1) * LANES]
        kpair = kquad[:, (hp % 2) * LANES:(hp % 2 + 1) * LANES]
        for sub in range(2):
            h = 2 * hp + sub
            aug0 = FOX_HEAD_DIM if sub == 0 else 0
            data = (lane < FOX_HEAD_DIM) if sub == 0 else (lane >= FOX_HEAD_DIM)
            slot_a = (lane == aug0) | (lane == aug0 + 8) | (lane == aug0 + 16)
            slot_b = (lane == aug0 + 1) | (lane == aug0 + 9) | (lane == aug0 + 17)
            ra = pltpu.roll(cpack, (aug0 - h) % LANES, axis=1)
            rb = pltpu.roll(cpack, (aug0 + 1 - h) % LANES, axis=1)
            qa = jnp.where(data, qpair, jnp.where(slot_a, ra, jnp.where(slot_b, 1.0, 0.0)))
            ka = jnp.where(data, kpair, jnp.where(slot_a, 1.0, jnp.where(slot_b, -rb, 0.0)))
            q_ref[0, h] = qa.astype(BF16)
            k_ref[0, h] = ka.astype(BF16)
    v_ref[0] = _mm(xb, w_ref[:, C_V:C_V + FOX_WIDTH]).astype(BF16)


def _inproj(x, w_main, w_f, b_f, tri):
    bsz, seq, _ = x.shape
    nt = seq // ROW_TILE
    const2 = lambda b, t: (0, 0)
    return pl.pallas_call(
        _inproj_kernel,
        grid=(bsz, nt),
        in_specs=[
            pl.BlockSpec((1, ROW_TILE, D_MODEL), lambda b, t: (b, t, 0)),
            pl.BlockSpec(w_main.shape, const2),
            pl.BlockSpec(w_f.shape, const2),
            pl.BlockSpec(b_f.shape, const2),
            pl.BlockSpec(tri.shape, const2),
        ],
        out_specs=[
            pl.BlockSpec((1, ROW_TILE, CONV_WIDTH), lambda b, t: (b, t, 0)),
            pl.BlockSpec((1, FOX_HEADS, ROW_TILE, LANES), lambda b, t: (b, 0, t, 0)),
            pl.BlockSpec((1, FOX_HEADS, ROW_TILE, LANES), lambda b, t: (b, 0, t, 0)),
            pl.BlockSpec((1, ROW_TILE, FOX_WIDTH), lambda b, t: (b, t, 0)),
        ],
        out_shape=[
            jax.ShapeDtypeStruct((bsz, seq, CONV_WIDTH), F32),
            jax.ShapeDtypeStruct((bsz, FOX_HEADS, seq, LANES), BF16),
            jax.ShapeDtypeStruct((bsz, FOX_HEADS, seq, LANES), BF16),
            jax.ShapeDtypeStruct((bsz, seq, FOX_WIDTH), BF16),
        ],
        scratch_shapes=[pltpu.VMEM((SUBLANES, LANES), F32)],
        compiler_params=pltpu.CompilerParams(
            dimension_semantics=("arbitrary", "arbitrary"), vmem_limit_bytes=VMEM_LIMIT),
        name="inproj",
    )(x, w_main, w_f, b_f, tri)


def _fox_kernel(q_ref, qn_ref, k_ref, v_ref, o_ref, vt_scr, qt_scr, qtn_scr, acc_scr, s_scr, m_scr):
    i = pl.program_id(2)
    nq = pl.num_programs(2)
    nk = vt_scr.shape[0]
    units = [(h, qb) for h in range(2) for qb in range(Q_TILE // Q_SUB)]

    def transpose_q(src_ref, dst):
        for h in range(2):
            dst[h] = src_ref[0, h].astype(F32).T.astype(BF16)

    def score_unit(j, u, diag, fresh, qt):
        h, qb = units[u]
        nkeys = (qb + 1) * Q_SUB if diag else K_TILE
        kt = k_ref[0, h, pl.ds(pl.multiple_of(j * K_TILE, K_TILE), nkeys), :]
        st = _mm(kt, qt[h, :, qb * Q_SUB:(qb + 1) * Q_SUB])
        if diag:
            kpos = lax.broadcasted_iota(jnp.int32, st.shape, 0)
            qpos = lax.broadcasted_iota(jnp.int32, st.shape, 1) + qb * Q_SUB
            st = jnp.where(kpos <= qpos, st, NEG_INF)
        s_scr[u, 0:nkeys, :] = st
        m_prev = jnp.full((1, Q_SUB), NEG_INF, F32) if fresh else m_scr[u, 1]
        m_scr[u, 0] = m_prev
        m_scr[u, 1] = jnp.maximum(m_prev, jnp.max(st, axis=0, keepdims=True))

    def value_unit(j, u, diag):
        h, qb = units[u]
        nkeys = (qb + 1) * Q_SUB if diag else K_TILE
        m_new = m_scr[u, 1]
        alpha = jnp.exp2(m_scr[u, 0] - m_new)
        pt = jnp.exp2((s_scr[u, 0:nkeys, :] - m_new).astype(BF16))
        acc_scr[u] = alpha * acc_scr[u] + _mm(vt_scr[j, h, :, 0:nkeys], pt)

    def stages(value_tile, score_tile, value_diag=False, score_diag=False, fresh=False, qt=qt_scr):
        for u in range(len(units)):
            if value_tile is not None:
                value_unit(value_tile, u, value_diag)
            if score_tile is not None:
                score_unit(score_tile, u, score_diag, fresh, qt)

    @pl.when(i == 0)
    def _():
        tail = (lax.broadcasted_iota(jnp.int32, (VT_ROWS - FOX_HEAD_DIM, K_TILE), 0) == 0).astype(BF16)
        for c in range(nk):
            blk = v_ref[0, c * K_TILE:(c + 1) * K_TILE, :].astype(F32).T.astype(BF16)
            for h in range(2):
                vt_scr[c, h, 0:FOX_HEAD_DIM, :] = blk[h * FOX_HEAD_DIM:(h + 1) * FOX_HEAD_DIM, :]
                vt_scr[c, h, FOX_HEAD_DIM:VT_ROWS, :] = tail

    transpose_q(q_ref, qt_scr)
    acc_scr[...] = jnp.zeros(acc_scr.shape, F32)

    @pl.when(i == 0)
    def _():
        stages(None, 0, score_diag=True, fresh=True)

    def trip(t, c):
        stages(t, t + 1)
        return c

    lax.fori_loop(0, i - 1, trip, 0)

    @pl.when(i > 0)
    def _():
        stages(i - 1, i, score_diag=True)

    @pl.when(i < nq - 1)
    def _():
        transpose_q(qn_ref, qtn_scr)
        stages(i, 0, value_diag=True, fresh=True, qt=qtn_scr)

    @pl.when(i == nq - 1)
    def _():
        stages(i, None, value_diag=True)

    rows = []
    for h in range(2):
        blocks = [acc_scr[u] for u in range(len(units)) if units[u][0] == h]
        rows.append(jnp.concatenate(
            [a[0:FOX_HEAD_DIM] / a[FOX_HEAD_DIM:FOX_HEAD_DIM + 1] for a in blocks], axis=1))
    o_ref[0] = jnp.concatenate(rows, axis=0).T.astype(BF16)


def _fox_attention(q_aug, k_aug, v):
    bsz, _, seq, _ = q_aug.shape
    nq = seq // Q_TILE
    nk = seq // K_TILE
    return pl.pallas_call(
        _fox_kernel,
        grid=(bsz, FOX_HEADS // 2, nq),
        in_specs=[
            pl.BlockSpec((1, 2, Q_TILE, LANES), lambda b, hp, i: (b, hp, i, 0)),
            pl.BlockSpec((1, 2, Q_TILE, LANES), lambda b, hp, i: (b, hp, jnp.minimum(i + 1, nq - 1), 0)),
            pl.BlockSpec((1, 2, seq, LANES), lambda b, hp, i: (b, hp, 0, 0)),
            pl.BlockSpec((1, seq, LANES), lambda b, hp, i: (b, 0, hp)),
        ],
        out_specs=pl.BlockSpec((1, Q_TILE, LANES), lambda b, hp, i: (b, i, hp)),
        out_shape=jax.ShapeDtypeStruct((bsz, seq, FOX_WIDTH), BF16),
        scratch_shapes=[
            pltpu.VMEM((nk, 2, VT_ROWS, K_TILE), BF16),
            pltpu.VMEM((2, LANES, Q_TILE), BF16),
            pltpu.VMEM((2, LANES, Q_TILE), BF16),
            pltpu.VMEM((2 * (Q_TILE // Q_SUB), VT_ROWS, Q_SUB), F32),
            pltpu.VMEM((2 * (Q_TILE // Q_SUB), K_TILE, Q_SUB), F32),
            pltpu.VMEM((2 * (Q_TILE // Q_SUB), 2, 1, Q_SUB), F32),
        ],
        compiler_params=pltpu.CompilerParams(
            dimension_semantics=("arbitrary", "arbitrary", "arbitrary"), vmem_limit_bytes=VMEM_LIMIT),
        name="fox_attention",
    )(q_aug, q_aug, k_aug, v)


def _memkv_kernel(mem_ref, wk_ref, wv_ref, k_ref, v_ref):
    mb = mem_ref[0].astype(BF16)
    k_ref[0] = _mm(mb, wk_ref[...]).astype(BF16)
    v_ref[0] = _mm(mb, wv_ref[...]).astype(BF16)


def _mem_kv(mem, w_ck, w_cv):
    bsz = mem.shape[0]
    const2 = lambda b: (0, 0)
    blk = pl.BlockSpec((1, MEM_LEN, D_MODEL), lambda b: (b, 0, 0))
    return pl.pallas_call(
        _memkv_kernel,
        grid=(bsz,),
        in_specs=[blk, pl.BlockSpec(w_ck.shape, const2), pl.BlockSpec(w_cv.shape, const2)],
        out_specs=[blk, blk],
        out_shape=[jax.ShapeDtypeStruct((bsz, MEM_LEN, D_MODEL), BF16)] * 2,
        compiler_params=pltpu.CompilerParams(
            dimension_semantics=("arbitrary",), vmem_limit_bytes=VMEM_LIMIT),
        name="mem_kv",
    )(mem, w_ck, w_cv)


def _conv_branch(u_ref, cw_ref, cb_ref, cg_ref, cbeta_ref, ubuf, shifted):
    t = pl.program_id(1)
    rows = u_ref.shape[1]
    first = HALO - (CONV_K - 1)
    span = shifted.shape[1]

    @pl.when(t == 0)
    def _():
        ubuf[0:HALO, :] = jnp.zeros((HALO, CONV_WIDTH), F32)

    @pl.when(t > 0)
    def _():
        ubuf[0:HALO, :] = ubuf[rows:rows + HALO, :]

    ubuf[HALO:HALO + rows, :] = u_ref[0]
    for p in range(1, SUBLANES):
        shifted[p - 1] = ubuf[p:p + span, :]

    cw = cw_ref[...]
    cbias = cb_ref[...]
    gam = cg_ref[...]
    beta = cbeta_ref[...]
    outs = []
    for r in range(rows // CONV_ROWS):
        base = r * CONV_ROWS
        acc = jnp.broadcast_to(cbias, (CONV_ROWS, CONV_WIDTH))
        for j in range(CONV_K):
            a, p = divmod(first + j, SUBLANES)
            lo = base + SUBLANES * a
            src = ubuf[lo:lo + CONV_ROWS, :] if p == 0 else shifted[p - 1, lo:lo + CONV_ROWS, :]
            acc = acc + cw[j:j + 1, :] * src
        y = _layer_norm(acc, gam, beta)
        outs.append((y * _sigmoid(y)).astype(BF16))
    return jnp.concatenate(outs, axis=0)


def _mix_cross_kernel(x_ref, u_ref, o_ref, cw_ref, cb_ref, cg_ref, cbeta_ref, wout_ref, g1_ref, b1_ref,
                      wcq_ref, kc_ref, vc_ref, wco_ref, g2_ref, b2_ref, h_ref, ubuf, shifted):
    uc = _conv_branch(u_ref, cw_ref, cb_ref, cg_ref, cbeta_ref, ubuf, shifted)
    mix = _mm(uc, wout_ref[0:CONV_WIDTH, :]) + _mm(o_ref[0], wout_ref[CONV_WIDTH:, :])
    h1 = _layer_norm(DEEPNORM_ALPHA * x_ref[0] + mix, g1_ref[...], b1_ref[...])

    q = (_mm(h1.astype(BF16), wcq_ref[...]) * (1.0 / math.sqrt(MEM_HEAD_DIM))).astype(BF16)
    outs = []
    for hh in range(MEM_HEADS):
        cols = slice(hh * MEM_HEAD_DIM, (hh + 1) * MEM_HEAD_DIM)
        s = _mm_nt(q[:, cols], kc_ref[0, :, cols])
        p = jnp.exp(s - jnp.max(s, axis=-1, keepdims=True))
        l = jnp.sum(p, axis=-1, keepdims=True)
        outs.append((_mm(p.astype(BF16), vc_ref[0, :, cols]) / l).astype(BF16))
    o = jnp.concatenate(outs, axis=-1)
    y = _mm(o, wco_ref[...])
    h_ref[0] = _layer_norm(DEEPNORM_ALPHA * h1 + y, g2_ref[...], b2_ref[...])


def _mix_cross(x, u, o, conv_w, conv_b, conv_g, conv_beta, w_out, g1, b1, w_cq, kc, vc, w_co, g2, b2):
    bsz, seq, _ = x.shape
    nt = seq // ROW_TILE
    const2 = lambda b, t: (0, 0)
    tok = lambda width: pl.BlockSpec((1, ROW_TILE, width), lambda b, t: (b, t, 0))
    memblk = pl.BlockSpec((1, MEM_LEN, D_MODEL), lambda b, t: (b, 0, 0))
    vec = pl.BlockSpec((1, D_MODEL), const2)
    cvec = pl.BlockSpec((1, CONV_WIDTH), const2)
    mat = pl.BlockSpec((D_MODEL, D_MODEL), const2)
    return pl.pallas_call(
        _mix_cross_kernel,
        grid=(bsz, nt),
        in_specs=[tok(D_MODEL), tok(CONV_WIDTH), tok(FOX_WIDTH), pl.BlockSpec(conv_w.shape, const2), cvec, cvec,
                  cvec, mat, vec, vec, mat, memblk, memblk, mat, vec, vec],
        out_specs=tok(D_MODEL),
        out_shape=jax.ShapeDtypeStruct((bsz, seq, D_MODEL), F32),
        scratch_shapes=[
            pltpu.VMEM((ROW_TILE + HALO, CONV_WIDTH), F32),
            pltpu.VMEM((SUBLANES - 1, ROW_TILE + HALO - SUBLANES, CONV_WIDTH), F32),
        ],
        compiler_params=pltpu.CompilerParams(
            dimension_semantics=("arbitrary", "arbitrary"), vmem_limit_bytes=VMEM_LIMIT),
        name="mix_cross",
    )(x, u, o, conv_w, conv_b, conv_g, conv_beta, w_out, g1, b1, w_cq, kc, vc, w_co, g2, b2)


def _swiglu_kernel(h_ref, wg_ref, wu_ref, wd_ref, g_ref, b_ref, out_ref):
    h = h_ref[...]
    hb = h.astype(BF16)
    acc = DEEPNORM_ALPHA * h
    for c in range(D_FF // FF_CHUNK):
        cols = slice(c * FF_CHUNK, (c + 1) * FF_CHUNK)
        gate = _mm(hb, wg_ref[:, cols])
        up = _mm(hb, wu_ref[:, cols])
        act = (gate * _sigmoid(gate) * up).astype(BF16)
        acc = acc + _mm(act, wd_ref[cols, :])
    out_ref[...] = _layer_norm(acc, g_ref[...], b_ref[...])


def _swiglu(h, w_gate, w_up, w_down, g, b):
    n = h.shape[0]
    const2 = lambda t: (0, 0)
    tok = pl.BlockSpec((FF_ROW_TILE, D_MODEL), lambda t: (t, 0))
    return pl.pallas_call(
        _swiglu_kernel,
        grid=(n // FF_ROW_TILE,),
        in_specs=[tok, pl.BlockSpec(w_gate.shape, const2), pl.BlockSpec(w_up.shape, const2),
                  pl.BlockSpec(w_down.shape, const2), pl.BlockSpec((1, D_MODEL), const2),
                  pl.BlockSpec((1, D_MODEL), const2)],
        out_specs=tok,
        out_shape=jax.ShapeDtypeStruct((n, D_MODEL), F32),
        compiler_params=pltpu.CompilerParams(
            dimension_semantics=("arbitrary",), vmem_limit_bytes=VMEM_LIMIT),
        name="swiglu",
    )(h, w_gate, w_up, w_down, g, b)


def _layer(h, mem, w_in, b_forget, conv_w, conv_b, conv_ln_g, conv_ln_b, w_out, ln_mix_g, ln_mix_b,
           w_cq, w_ck, w_cv, w_co, ln_cross_g, ln_cross_b, w_gate, w_up, w_down, ln_ffn_g, ln_ffn_b):
    bsz, seq, _ = h.shape
    row = lambda v: v.reshape(1, -1).astype(F32)
    w_main = w_in[:, :C_F].astype(BF16)
    w_f = jnp.pad(w_in[:, C_F:], ((0, 0), (0, LANES - FOX_HEADS))).astype(BF16)
    b_f = jnp.pad(b_forget.astype(F32), (0, LANES - FOX_HEADS)).reshape(1, LANES)
    tri = jnp.tri(ROW_TILE, dtype=BF16)

    u, q_aug, k_aug, v = _inproj(h, w_main, w_f, b_f, tri)
    o = _fox_attention(q_aug, k_aug, v)
    kc, vc = _mem_kv(mem, w_ck.astype(BF16), w_cv.astype(BF16))
    h2 = _mix_cross(h, u, o, conv_w.astype(F32), row(conv_b), row(conv_ln_g), row(conv_ln_b),
                    w_out.astype(BF16), row(ln_mix_g), row(ln_mix_b), w_cq.astype(BF16), kc, vc,
                    w_co.astype(BF16), row(ln_cross_g), row(ln_cross_b))
    h3 = _swiglu(h2.reshape(bsz * seq, D_MODEL), w_gate.astype(BF16), w_up.astype(BF16),
                 w_down.astype(BF16), row(ln_ffn_g), row(ln_ffn_b))
    return h3.reshape(bsz, seq, D_MODEL)


def kernel(x, mem, w_in, b_forget, conv_w, conv_b, conv_ln_g, conv_ln_b, w_out, ln_mix_g, ln_mix_b,
           w_cq, w_ck, w_cv, w_co, ln_cross_g, ln_cross_b, w_gate, w_up, w_down, ln_ffn_g, ln_ffn_b):
    depth = w_in.shape[0]
    assert depth == 1, "DEEPNORM_ALPHA is fixed for a single layer"
    h = x
    for l in range(depth):
        h = _layer(h, mem, w_in[l], b_forget[l], conv_w[l], conv_b[l], conv_ln_g[l], conv_ln_b[l],
                   w_out[l], ln_mix_g[l], ln_mix_b[l], w_cq[l], w_ck[l], w_cv[l], w_co[l],
                   ln_cross_g[l], ln_cross_b[l], w_gate[l], w_up[l], w_down[l], ln_ffn_g[l], ln_ffn_b[l])
    return h
```

```python
import functools
import math

import jax
import jax.numpy as jnp
from jax import lax
from jax.experimental import pallas as pl
from jax.experimental.pallas import tpu as pltpu

D_MODEL = 1024
CONV_WIDTH = 512
CONV_K = 31
FOX_WIDTH = 512
FOX_HEADS = 8
FOX_HEAD_DIM = 64
MEM_LEN = 256
MEM_HEADS = 4
MEM_HEAD_DIM = 256
D_FF = 2816
LN_EPS = 1e-5
NEG_INF = -1e30
DEEPNORM_ALPHA = 2.0 ** 0.25
LOG2E = math.log2(math.e)

LANES = 128
SUBLANES = 8
HALO = 32
ROW_TILE = 512
FF_ROW_TILE = 1024
Q_TILE = 512
K_TILE = 512
Q_SUB = 256
EXP_ROWS = 64
VT_ROWS = FOX_HEAD_DIM + 16
FF_CHUNK = 256
CONV_ROWS = 32
VMEM_LIMIT = 56 * 1024 * 1024

BF16 = jnp.bfloat16
F32 = jnp.float32

C_GLU_A = 0
C_GLU_B = CONV_WIDTH
C_Q = 2 * CONV_WIDTH
C_K = C_Q + FOX_WIDTH
C_V = C_K + FOX_WIDTH
C_F = C_V + FOX_WIDTH


def _mm(a, b):
    return jnp.dot(a, b, preferred_element_type=F32)


def _mm_nt(a, b):
    return lax.dot_general(a, b, (((1,), (1,)), ((), ())), preferred_element_type=F32)


def _layer_norm(x, g, b):
    mu = jnp.mean(x, axis=-1, keepdims=True)
    xc = x - mu
    var = jnp.mean(xc * xc, axis=-1, keepdims=True)
    return xc * lax.rsqrt(var + LN_EPS) * g + b


def _sigmoid(x):
    return 1.0 / (1.0 + jnp.exp(-x))


def _split3(x):
    hi = x.astype(BF16).astype(F32)
    r = x - hi
    mid = r.astype(BF16).astype(F32)
    lo = (r - mid).astype(BF16).astype(F32)
    return hi, mid, lo


def _pack3(hi, mid, lo):
    return hi + pltpu.roll(mid, 8, axis=1) + pltpu.roll(lo, 16, axis=1)


def _inproj_kernel(x_ref, w_ref, wf_ref, bf_ref, tri_ref, u_ref, q_ref, k_ref, v_ref, carry):
    t = pl.program_id(1)
    rows = x_ref.shape[1]
    xb = x_ref[0].astype(BF16)

    @pl.when(t == 0)
    def _():
        carry[...] = jnp.zeros_like(carry)

    glu_a = _mm(xb, w_ref[:, C_GLU_A:C_GLU_A + CONV_WIDTH])
    glu_b = _mm(xb, w_ref[:, C_GLU_B:C_GLU_B + CONV_WIDTH])
    u_ref[0] = glu_a * _sigmoid(glu_b)

    lane = lax.broadcasted_iota(jnp.int32, (rows, LANES), 1)
    f = _mm(xb, wf_ref[...]) + bf_ref[...]
    logf = jnp.minimum(f, 0.0) - jnp.log(1.0 + jnp.exp(-jnp.abs(f)))
    logf = jnp.where(lane < FOX_HEADS, logf, 0.0)
    packed = _pack3(*_split3(logf)).astype(BF16)
    res = _mm(tri_ref[...], packed)
    cum = res + pltpu.roll(res, LANES - 8, axis=1) + pltpu.roll(res, LANES - 16, axis=1)
    cum = jnp.where(lane < FOX_HEADS, cum + carry[0:1, :], 0.0)
    carry[0:1, :] = cum[rows - 1:rows, :]
    cpack = _pack3(*_split3(cum * LOG2E))

    scale = LOG2E / math.sqrt(FOX_HEAD_DIM)
    for hp in range(FOX_HEADS // 2):
        if hp % 2 == 0:
            qquad = _mm(xb, w_ref[:, C_Q + hp * LANES:C_Q + (hp + 2) * LANES]) * scale
            kquad = _mm(xb, w_ref[:, C_K + hp * LANES:C_K + (hp + 2) * LANES])
        qpair = qquad[:, (hp % 2) * LANES:(hp % 2 + 1) * LANES]
        kpair = kquad[:, (hp % 2) * LANES:(hp % 2 + 1) * LANES]
        for sub in range(2):
            h = 2 * hp + sub
            aug0 = FOX_HEAD_DIM if sub == 0 else 0
            data = (lane < FOX_HEAD_DIM) if sub == 0 else (lane >= FOX_HEAD_DIM)
            slot_a = (lane == aug0) | (lane == aug0 + 8) | (lane == aug0 + 16)
            slot_b = (lane == aug0 + 1) | (lane == aug0 + 9) | (lane == aug0 + 17)
            ra = pltpu.roll(cpack, (aug0 - h) % LANES, axis=1)
            rb = pltpu.roll(cpack, (aug0 + 1 - h) % LANES, axis=1)
            qa = jnp.where(data, qpair, jnp.where(slot_a, ra, jnp.where(slot_b, 1.0, 0.0)))
            ka = jnp.where(data, kpair, jnp.where(slot_a, 1.0, jnp.where(slot_b, -rb, 0.0)))
            q_ref[0, h] = qa.astype(BF16)
            k_ref[0, h] = ka.astype(BF16)
    v_ref[0] = _mm(xb, w_ref[:, C_V:C_V + FOX_WIDTH]).astype(BF16)


def _inproj(x, w_main, w_f, b_f, tri):
    bsz, seq, _ = x.shape
    nt = seq // ROW_TILE
    const2 = lambda b, t: (0, 0)
    return pl.pallas_call(
        _inproj_kernel,
        grid=(bsz, nt),
        in_specs=[
            pl.BlockSpec((1, ROW_TILE, D_MODEL), lambda b, t: (b, t, 0)),
            pl.BlockSpec(w_main.shape, const2),
            pl.BlockSpec(w_f.shape, const2),
            pl.BlockSpec(b_f.shape, const2),
            pl.BlockSpec(tri.shape, const2),
        ],
        out_specs=[
            pl.BlockSpec((1, ROW_TILE, CONV_WIDTH), lambda b, t: (b, t, 0)),
            pl.BlockSpec((1, FOX_HEADS, ROW_TILE, LANES), lambda b, t: (b, 0, t, 0)),
            pl.BlockSpec((1, FOX_HEADS, ROW_TILE, LANES), lambda b, t: (b, 0, t, 0)),
            pl.BlockSpec((1, ROW_TILE, FOX_WIDTH), lambda b, t: (b, t, 0)),
        ],
        out_shape=[
            jax.ShapeDtypeStruct((bsz, seq, CONV_WIDTH), F32),
            jax.ShapeDtypeStruct((bsz, FOX_HEADS, seq, LANES), BF16),
            jax.ShapeDtypeStruct((bsz, FOX_HEADS, seq, LANES), BF16),
            jax.ShapeDtypeStruct((bsz, seq, FOX_WIDTH), BF16),
        ],
        scratch_shapes=[pltpu.VMEM((SUBLANES, LANES), F32)],
        compiler_params=pltpu.CompilerParams(
            dimension_semantics=("arbitrary", "arbitrary"), vmem_limit_bytes=VMEM_LIMIT),
        name="inproj",
    )(x, w_main, w_f, b_f, tri)


def _fox_kernel(q_ref, qn_ref, k_ref, v_ref, o_ref, vt_scr, qt_scr, qtn_scr, acc_scr, s_scr, m_scr):
    i = pl.program_id(1)
    nq = pl.num_programs(1)
    nk = vt_scr.shape[0]
    npairs = FOX_HEADS // 2
    units = [(h, qb) for h in range(2) for qb in range(Q_TILE // Q_SUB)]

    def transpose_q(src_ref, head0, dst):
        for h in range(2):
            dst[h] = src_ref[0, head0 + h].astype(F32).T.astype(BF16)

    def score_unit(pair, j, u, diag, fresh, qt):
        h, qb = units[u]
        nkeys = (qb + 1) * Q_SUB if diag else K_TILE
        kt = k_ref[0, 2 * pair + h, pl.ds(pl.multiple_of(j * K_TILE, K_TILE), nkeys), :]
        st = _mm(kt, qt[h, :, qb * Q_SUB:(qb + 1) * Q_SUB])
        if diag:
            kpos = lax.broadcasted_iota(jnp.int32, st.shape, 0)
            qpos = lax.broadcasted_iota(jnp.int32, st.shape, 1) + qb * Q_SUB
            st = jnp.where(kpos <= qpos, st, NEG_INF)
        s_scr[u, 0:nkeys, :] = st
        m_prev = jnp.full((1, Q_SUB), NEG_INF, F32) if fresh else m_scr[u, 1]
        m_scr[u, 0] = m_prev
        m_scr[u, 1] = jnp.maximum(m_prev, jnp.max(st, axis=0, keepdims=True))

    def value_unit(pair, j, u, diag):
        h, qb = units[u]
        nkeys = (qb + 1) * Q_SUB if diag else K_TILE
        m_new = m_scr[u, 1]
        alpha = jnp.exp2(m_scr[u, 0] - m_new)
        pt = jnp.exp2((s_scr[u, 0:nkeys, :] - m_new).astype(BF16))
        acc_scr[u] = alpha * acc_scr[u] + _mm(vt_scr[j, 2 * pair + h, :, 0:nkeys], pt)

    def stages(value=None, score=None, value_diag=False, score_diag=False, fresh=False, qt=None):
        for u in range(len(units)):
            if value is not None:
                value_unit(value[0], value[1], u, value_diag)
            if score is not None:
                score_unit(score[0], score[1], u, score_diag, fresh, qt_scr.at[score[0]] if qt is None else qt)

    @pl.when(i == 0)
    def _():
        tail = (lax.broadcasted_iota(jnp.int32, (VT_ROWS - FOX_HEAD_DIM, K_TILE), 0) == 0).astype(BF16)
        for c in range(nk):
            for pair in range(npairs):
                blk = v_ref[0, c * K_TILE:(c + 1) * K_TILE, pair * LANES:(pair + 1) * LANES]
                blk = blk.astype(F32).T.astype(BF16)
                for h in range(2):
                    vt_scr[c, 2 * pair + h, 0:FOX_HEAD_DIM, :] = blk[h * FOX_HEAD_DIM:(h + 1) * FOX_HEAD_DIM, :]
                    vt_scr[c, 2 * pair + h, FOX_HEAD_DIM:VT_ROWS, :] = tail

    for pair in range(npairs):
        transpose_q(q_ref, 2 * pair, qt_scr.at[pair])

    @pl.when(i == 0)
    def _():
        stages(score=(0, 0), score_diag=True, fresh=True)

    for pair in range(npairs):
        acc_scr[...] = jnp.zeros(acc_scr.shape, F32)

        def trip(t, c, pair=pair):
            stages(value=(pair, t), score=(pair, t + 1))
            return c

        lax.fori_loop(0, i - 1, trip, 0)

        @pl.when(i > 0)
        def _(pair=pair):
            stages(value=(pair, i - 1), score=(pair, i), score_diag=True)

        if pair + 1 < npairs:
            @pl.when(i == 0)
            def _(pair=pair):
                stages(value=(pair, i), score=(pair + 1, 0), value_diag=True, score_diag=True, fresh=True)

            @pl.when(i > 0)
            def _(pair=pair):
                stages(value=(pair, i), score=(pair + 1, 0), value_diag=True, fresh=True)
        else:
            @pl.when(i < nq - 1)
            def _(pair=pair):
                transpose_q(qn_ref, 0, qtn_scr)
                stages(value=(pair, i), score=(0, 0), value_diag=True, fresh=True, qt=qtn_scr)

            @pl.when(i == nq - 1)
            def _(pair=pair):
                stages(value=(pair, i), value_diag=True)

        rows = []
        for h in range(2):
            blocks = [acc_scr[u] for u in range(len(units)) if units[u][0] == h]
            rows.append(jnp.concatenate(
                [a[0:FOX_HEAD_DIM] / a[FOX_HEAD_DIM:FOX_HEAD_DIM + 1] for a in blocks], axis=1))
        pair_out = jnp.concatenate(rows, axis=0).T.astype(BF16)
        o_ref[0, :, pair * LANES:(pair + 1) * LANES] = pair_out


def _fox_attention(q_aug, k_aug, v):
    bsz, _, seq, _ = q_aug.shape
    nq = seq // Q_TILE
    nk = seq // K_TILE
    return pl.pallas_call(
        _fox_kernel,
        grid=(bsz, nq),
        in_specs=[
            pl.BlockSpec((1, FOX_HEADS, Q_TILE, LANES), lambda b, i: (b, 0, i, 0)),
            pl.BlockSpec((1, 2, Q_TILE, LANES), lambda b, i: (b, 0, jnp.minimum(i + 1, nq - 1), 0)),
            pl.BlockSpec((1, FOX_HEADS, seq, LANES), lambda b, i: (b, 0, 0, 0)),
            pl.BlockSpec((1, seq, FOX_WIDTH), lambda b, i: (b, 0, 0)),
        ],
        out_specs=pl.BlockSpec((1, Q_TILE, FOX_WIDTH), lambda b, i: (b, i, 0)),
        out_shape=jax.ShapeDtypeStruct((bsz, seq, FOX_WIDTH), BF16),
        scratch_shapes=[
            pltpu.VMEM((nk, FOX_HEADS, VT_ROWS, K_TILE), BF16),
            pltpu.VMEM((FOX_HEADS // 2, 2, LANES, Q_TILE), BF16),
            pltpu.VMEM((2, LANES, Q_TILE), BF16),
            pltpu.VMEM((2 * (Q_TILE // Q_SUB), VT_ROWS, Q_SUB), F32),
            pltpu.VMEM((2 * (Q_TILE // Q_SUB), K_TILE, Q_SUB), F32),
            pltpu.VMEM((2 * (Q_TILE // Q_SUB), 2, 1, Q_SUB), F32),
        ],
        compiler_params=pltpu.CompilerParams(
            dimension_semantics=("arbitrary", "arbitrary"), vmem_limit_bytes=VMEM_LIMIT),
        name="fox_attention",
    )(q_aug, q_aug, k_aug, v)


def _memkv_kernel(mem_ref, wk_ref, wv_ref, k_ref, v_ref):
    mb = mem_ref[0].astype(BF16)
    k_ref[0] = _mm(mb, wk_ref[...]).astype(BF16)
    v_ref[0] = _mm(mb, wv_ref[...]).astype(BF16)


def _mem_kv(mem, w_ck, w_cv):
    bsz = mem.shape[0]
    const2 = lambda b: (0, 0)
    blk = pl.BlockSpec((1, MEM_LEN, D_MODEL), lambda b: (b, 0, 0))
    return pl.pallas_call(
        _memkv_kernel,
        grid=(bsz,),
        in_specs=[blk, pl.BlockSpec(w_ck.shape, const2), pl.BlockSpec(w_cv.shape, const2)],
        out_specs=[blk, blk],
        out_shape=[jax.ShapeDtypeStruct((bsz, MEM_LEN, D_MODEL), BF16)] * 2,
        compiler_params=pltpu.CompilerParams(
            dimension_semantics=("arbitrary",), vmem_limit_bytes=VMEM_LIMIT),
        name="mem_kv",
    )(mem, w_ck, w_cv)


def _conv_branch(u_ref, cw_ref, cb_ref, cg_ref, cbeta_ref, ubuf, shifted):
    t = pl.program_id(1)
    rows = u_ref.shape[1]
    first = HALO - (CONV_K - 1)
    span = shifted.shape[1]

    @pl.when(t == 0)
    def _():
        ubuf[0:HALO, :] = jnp.zeros((HALO, CONV_WIDTH), F32)

    @pl.when(t > 0)
    def _():
        ubuf[0:HALO, :] = ubuf[rows:rows + HALO, :]

    ubuf[HALO:HALO + rows, :] = u_ref[0]
    for p in range(1, SUBLANES):
        shifted[p - 1] = ubuf[p:p + span, :]

    cw = cw_ref[...]
    cbias = cb_ref[...]
    gam = cg_ref[...]
    beta = cbeta_ref[...]
    outs = []
    for r in range(rows // CONV_ROWS):
        base = r * CONV_ROWS
        acc = jnp.broadcast_to(cbias, (CONV_ROWS, CONV_WIDTH))
        for j in range(CONV_K):
            a, p = divmod(first + j, SUBLANES)
            lo = base + SUBLANES * a
            src = ubuf[lo:lo + CONV_ROWS, :] if p == 0 else shifted[p - 1, lo:lo + CONV_ROWS, :]
            acc = acc + cw[j:j + 1, :] * src
        y = _layer_norm(acc, gam, beta)
        outs.append((y * _sigmoid(y)).astype(BF16))
    return jnp.concatenate(outs, axis=0)


def _mix_cross_kernel(x_ref, u_ref, o_ref, cw_ref, cb_ref, cg_ref, cbeta_ref, wout_ref, g1_ref, b1_ref,
                      wcq_ref, kc_ref, vc_ref, wco_ref, g2_ref, b2_ref, h_ref, ubuf, shifted):
    uc = _conv_branch(u_ref, cw_ref, cb_ref, cg_ref, cbeta_ref, ubuf, shifted)
    mix = _mm(uc, wout_ref[0:CONV_WIDTH, :]) + _mm(o_ref[0], wout_ref[CONV_WIDTH:, :])
    h1 = _layer_norm(DEEPNORM_ALPHA * x_ref[0] + mix, g1_ref[...], b1_ref[...])

    q = (_mm(h1.astype(BF16), wcq_ref[...]) * (1.0 / math.sqrt(MEM_HEAD_DIM))).astype(BF16)
    outs = []
    for hh in range(MEM_HEADS):
        cols = slice(hh * MEM_HEAD_DIM, (hh + 1) * MEM_HEAD_DIM)
        s = _mm_nt(q[:, cols], kc_ref[0, :, cols])
        p = jnp.exp(s - jnp.max(s, axis=-1, keepdims=True))
        l = jnp.sum(p, axis=-1, keepdims=True)
        outs.append((_mm(p.astype(BF16), vc_ref[0, :, cols]) / l).astype(BF16))
    o = jnp.concatenate(outs, axis=-1)
    y = _mm(o, wco_ref[...])
    h_ref[0] = _layer_norm(DEEPNORM_ALPHA * h1 + y, g2_ref[...], b2_ref[...])


def _mix_cross(x, u, o, conv_w, conv_b, conv_g, conv_beta, w_out, g1, b1, w_cq, kc, vc, w_co, g2, b2):
    bsz, seq, _ = x.shape
    nt = seq // ROW_TILE
    const2 = lambda b, t: (0, 0)
    tok = lambda width: pl.BlockSpec((1, ROW_TILE, width), lambda b, t: (b, t, 0))
    memblk = pl.BlockSpec((1, MEM_LEN, D_MODEL), lambda b, t: (b, 0, 0))
    vec = pl.BlockSpec((1, D_MODEL), const2)
    cvec = pl.BlockSpec((1, CONV_WIDTH), const2)
    mat = pl.BlockSpec((D_MODEL, D_MODEL), const2)
    return pl.pallas_call(
        _mix_cross_kernel,
        grid=(bsz, nt),
        in_specs=[tok(D_MODEL), tok(CONV_WIDTH), tok(FOX_WIDTH), pl.BlockSpec(conv_w.shape, const2), cvec, cvec,
                  cvec, mat, vec, vec, mat, memblk, memblk, mat, vec, vec],
        out_specs=tok(D_MODEL),
        out_shape=jax.ShapeDtypeStruct((bsz, seq, D_MODEL), F32),
        scratch_shapes=[
            pltpu.VMEM((ROW_TILE + HALO, CONV_WIDTH), F32),
            pltpu.VMEM((SUBLANES - 1, ROW_TILE + HALO - SUBLANES, CONV_WIDTH), F32),
        ],
        compiler_params=pltpu.CompilerParams(
            dimension_semantics=("arbitrary", "arbitrary"), vmem_limit_bytes=VMEM_LIMIT),
        name="mix_cross",
    )(x, u, o, conv_w, conv_b, conv_g, conv_beta, w_out, g1, b1, w_cq, kc, vc, w_co, g2, b2)


def _swiglu_kernel(h_ref, wg_ref, wu_ref, wd_ref, g_ref, b_ref, out_ref):
    h = h_ref[...]
    hb = h.astype(BF16)
    acc = DEEPNORM_ALPHA * h
    for c in range(D_FF // FF_CHUNK):
        cols = slice(c * FF_CHUNK, (c + 1) * FF_CHUNK)
        gate = _mm(hb, wg_ref[:, cols])
        up = _mm(hb, wu_ref[:, cols])
        act = (gate * _sigmoid(gate) * up).astype(BF16)
        acc = acc + _mm(act, wd_ref[cols, :])
    out_ref[...] = _layer_norm(acc, g_ref[...], b_ref[...])


def _swiglu(h, w_gate, w_up, w_down, g, b):
    n = h.shape[0]
    const2 = lambda t: (0, 0)
    tok = pl.BlockSpec((FF_ROW_TILE, D_MODEL), lambda t: (t, 0))
    return pl.pallas_call(
        _swiglu_kernel,
        grid=(n // FF_ROW_TILE,),
        in_specs=[tok, pl.BlockSpec(w_gate.shape, const2), pl.BlockSpec(w_up.shape, const2),
                  pl.BlockSpec(w_down.shape, const2), pl.BlockSpec((1, D_MODEL), const2),
                  pl.BlockSpec((1, D_MODEL), const2)],
        out_specs=tok,
        out_shape=jax.ShapeDtypeStruct((n, D_MODEL), F32),
        compiler_params=pltpu.CompilerParams(
            dimension_semantics=("arbitrary",), vmem_limit_bytes=VMEM_LIMIT),
        name="swiglu",
    )(h, w_gate, w_up, w_down, g, b)


def _layer(h, mem, w_in, b_forget, conv_w, conv_b, conv_ln_g, conv_ln_b, w_out, ln_mix_g, ln_mix_b,
           w_cq, w_ck, w_cv, w_co, ln_cross_g, ln_cross_b, w_gate, w_up, w_down, ln_ffn_g, ln_ffn_b):
    bsz, seq, _ = h.shape
    row = lambda v: v.reshape(1, -1).astype(F32)
    w_main = w_in[:, :C_F].astype(BF16)
    w_f = jnp.pad(w_in[:, C_F:], ((0, 0), (0, LANES - FOX_HEADS))).astype(BF16)
    b_f = jnp.pad(b_forget.astype(F32), (0, LANES - FOX_HEADS)).reshape(1, LANES)
    tri = jnp.tri(ROW_TILE, dtype=BF16)

    u, q_aug, k_aug, v = _inproj(h, w_main, w_f, b_f, tri)
    o = _fox_attention(q_aug, k_aug, v)
    kc, vc = _mem_kv(mem, w_ck.astype(BF16), w_cv.astype(BF16))
    h2 = _mix_cross(h, u, o, conv_w.astype(F32), row(conv_b), row(conv_ln_g), row(conv_ln_b),
                    w_out.astype(BF16), row(ln_mix_g), row(ln_mix_b), w_cq.astype(BF16), kc, vc,
                    w_co.astype(BF16), row(ln_cross_g), row(ln_cross_b))
    h3 = _swiglu(h2.reshape(bsz * seq, D_MODEL), w_gate.astype(BF16), w_up.astype(BF16),
                 w_down.astype(BF16), row(ln_ffn_g), row(ln_ffn_b))
    return h3.reshape(bsz, seq, D_MODEL)


def kernel(x, mem, w_in, b_forget, conv_w, conv_b, conv_ln_g, conv_ln_b, w_out, ln_mix_g, ln_mix_b,
           w_cq, w_ck, w_cv, w_co, ln_cross_g, ln_cross_b, w_gate, w_up, w_down, ln_ffn_g, ln_ffn_b):
    depth = w_in.shape[0]
    assert depth == 1, "DEEPNORM_ALPHA is fixed for a single layer"
    h = x
    for l in range(depth):
        h = _layer(h, mem, w_in[l], b_forget[l], conv_w[l], conv_b[l], conv_ln_g[l], conv_ln_b[l],
                   w_out[l], ln_mix_g[l], ln_mix_b[l], w_cq[l], w_ck[l], w_cv[l], w_co[l],
                   ln_cross_g[l], ln_cross_b[l], w_gate[l], w_up[l], w_down[l], ln_ffn_g[l], ln_ffn_b[l])
    return h
```

```python
import functools
import math

import jax
import jax.numpy as jnp
from jax import lax
from jax.experimental import pallas as pl
from jax.experimental.pallas import tpu as pltpu

D_MODEL = 1024
CONV_WIDTH = 512
CONV_K = 31
FOX_WIDTH = 512
FOX_HEADS = 8
FOX_HEAD_DIM = 64
MEM_LEN = 256
MEM_HEADS = 4
MEM_HEAD_DIM = 256
D_FF = 2816
LN_EPS = 1e-5
NEG_INF = -1e30
DEEPNORM_ALPHA = 2.0 ** 0.25
LOG2E = math.log2(math.e)

LANES = 128
SUBLANES = 8
HALO = 32
ROW_TILE = 512
FF_ROW_TILE = 1024
Q_TILE = 1024
K_TILE = 512
DIAG = Q_TILE // K_TILE
Q_SUB = 256
EXP_ROWS = 64
VT_ROWS = FOX_HEAD_DIM + 16
FF_CHUNK = 256
CONV_ROWS = 32
VMEM_LIMIT = 56 * 1024 * 1024

BF16 = jnp.bfloat16
F32 = jnp.float32

C_GLU_A = 0
C_GLU_B = CONV_WIDTH
C_Q = 2 * CONV_WIDTH
C_K = C_Q + FOX_WIDTH
C_V = C_K + FOX_WIDTH
C_F = C_V + FOX_WIDTH


def _mm(a, b):
    return jnp.dot(a, b, preferred_element_type=F32)


def _mm_nt(a, b):
    return lax.dot_general(a, b, (((1,), (1,)), ((), ())), preferred_element_type=F32)


def _layer_norm(x, g, b):
    mu = jnp.mean(x, axis=-1, keepdims=True)
    xc = x - mu
    var = jnp.mean(xc * xc, axis=-1, keepdims=True)
    return xc * lax.rsqrt(var + LN_EPS) * g + b


def _sigmoid(x):
    return 1.0 / (1.0 + jnp.exp(-x))


def _split3(x):
    hi = x.astype(BF16).astype(F32)
    r = x - hi
    mid = r.astype(BF16).astype(F32)
    lo = (r - mid).astype(BF16).astype(F32)
    return hi, mid, lo


def _pack3(hi, mid, lo):
    return hi + pltpu.roll(mid, 8, axis=1) + pltpu.roll(lo, 16, axis=1)


def _inproj_kernel(x_ref, w_ref, wf_ref, bf_ref, tri_ref, u_ref, q_ref, k_ref, v_ref, carry):
    t = pl.program_id(1)
    rows = x_ref.shape[1]
    xb = x_ref[0].astype(BF16)

    @pl.when(t == 0)
    def _():
        carry[...] = jnp.zeros_like(carry)

    glu_a = _mm(xb, w_ref[:, C_GLU_A:C_GLU_A + CONV_WIDTH])
    glu_b = _mm(xb, w_ref[:, C_GLU_B:C_GLU_B + CONV_WIDTH])
    u_ref[0] = glu_a * _sigmoid(glu_b)

    lane = lax.broadcasted_iota(jnp.int32, (rows, LANES), 1)
    f = _mm(xb, wf_ref[...]) + bf_ref[...]
    logf = jnp.minimum(f, 0.0) - jnp.log(1.0 + jnp.exp(-jnp.abs(f)))
    logf = jnp.where(lane < FOX_HEADS, logf, 0.0)
    packed = _pack3(*_split3(logf)).astype(BF16)
    res = _mm(tri_ref[...], packed)
    cum = res + pltpu.roll(res, LANES - 8, axis=1) + pltpu.roll(res, LANES - 16, axis=1)
    cum = jnp.where(lane < FOX_HEADS, cum + carry[0:1, :], 0.0)
    carry[0:1, :] = cum[rows - 1:rows, :]
    cpack = _pack3(*_split3(cum * LOG2E))

    scale = LOG2E / math.sqrt(FOX_HEAD_DIM)
    for hp in range(FOX_HEADS // 2):
        if hp % 2 == 0:
            qquad = _mm(xb, w_ref[:, C_Q + hp * LANES:C_Q + (hp + 2) * LANES]) * scale
            kquad = _mm(xb, w_ref[:, C_K + hp * LANES:C_K + (hp + 2) * LANES])
        qpair = qquad[:, (hp % 2) * LANES:(hp % 2 + 1) * LANES]
        kpair = kquad[:, (hp % 2) * LANES:(hp % 2 + 1) * LANES]
        for sub in range(2):
            h = 2 * hp + sub
            aug0 = FOX_HEAD_DIM if sub == 0 else 0
            data = (lane < FOX_HEAD_DIM) if sub == 0 else (lane >= FOX_HEAD_DIM)
            slot_a = (lane == aug0) | (lane == aug0 + 8) | (lane == aug0 + 16)
            slot_b = (lane == aug0 + 1) | (lane == aug0 + 9) | (lane == aug0 + 17)
            ra = pltpu.roll(cpack, (aug0 - h) % LANES, axis=1)
            rb = pltpu.roll(cpack, (aug0 + 1 - h) % LANES, axis=1)
            qa = jnp.where(data, qpair, jnp.where(slot_a, ra, jnp.where(slot_b, 1.0, 0.0)))
            ka = jnp.where(data, kpair, jnp.where(slot_a, 1.0, jnp.where(slot_b, -rb, 0.0)))
            q_ref[0, h] = qa.astype(BF16)
            k_ref[0, h] = ka.astype(BF16)
    v_ref[0] = _mm(xb, w_ref[:, C_V:C_V + FOX_WIDTH]).astype(BF16)


def _inproj(x, w_main, w_f, b_f, tri):
    bsz, seq, _ = x.shape
    nt = seq // ROW_TILE
    const2 = lambda b, t: (0, 0)
    return pl.pallas_call(
        _inproj_kernel,
        grid=(bsz, nt),
        in_specs=[
            pl.BlockSpec((1, ROW_TILE, D_MODEL), lambda b, t: (b, t, 0)),
            pl.BlockSpec(w_main.shape, const2),
            pl.BlockSpec(w_f.shape, const2),
            pl.BlockSpec(b_f.shape, const2),
            pl.BlockSpec(tri.shape, const2),
        ],
        out_specs=[
            pl.BlockSpec((1, ROW_TILE, CONV_WIDTH), lambda b, t: (b, t, 0)),
            pl.BlockSpec((1, FOX_HEADS, ROW_TILE, LANES), lambda b, t: (b, 0, t, 0)),
            pl.BlockSpec((1, FOX_HEADS, ROW_TILE, LANES), lambda b, t: (b, 0, t, 0)),
            pl.BlockSpec((1, ROW_TILE, FOX_WIDTH), lambda b, t: (b, t, 0)),
        ],
        out_shape=[
            jax.ShapeDtypeStruct((bsz, seq, CONV_WIDTH), F32),
            jax.ShapeDtypeStruct((bsz, FOX_HEADS, seq, LANES), BF16),
            jax.ShapeDtypeStruct((bsz, FOX_HEADS, seq, LANES), BF16),
            jax.ShapeDtypeStruct((bsz, seq, FOX_WIDTH), BF16),
        ],
        scratch_shapes=[pltpu.VMEM((SUBLANES, LANES), F32)],
        compiler_params=pltpu.CompilerParams(
            dimension_semantics=("arbitrary", "arbitrary"), vmem_limit_bytes=VMEM_LIMIT),
        name="inproj",
    )(x, w_main, w_f, b_f, tri)


def _fox_kernel(q_ref, qn_ref, k_ref, v_ref, o_ref, vt_scr, qt_scr, qtn_scr, acc_scr, s_scr, m_scr):
    i = pl.program_id(1)
    nq = pl.num_programs(1)
    nk = vt_scr.shape[0]
    npairs = FOX_HEADS // 2
    units = [(h, qb) for h in range(2) for qb in range(Q_TILE // Q_SUB)]
    first_diag = DIAG * i

    def band_keys(qb, band):
        if band is None:
            return K_TILE
        return max(0, min(K_TILE, (qb + 1) * Q_SUB - band * K_TILE))

    def transpose_q(src_ref, head0, dst):
        for h in range(2):
            dst[h] = src_ref[0, head0 + h].astype(F32).T.astype(BF16)

    def score_unit(pair, j, u, band, fresh, qt):
        h, qb = units[u]
        nkeys = band_keys(qb, band)
        if nkeys == 0:
            return
        kt = k_ref[0, 2 * pair + h, pl.ds(pl.multiple_of(j * K_TILE, K_TILE), nkeys), :]
        st = _mm(kt, qt[h, :, qb * Q_SUB:(qb + 1) * Q_SUB])
        if band is not None and band * K_TILE + nkeys - 1 > qb * Q_SUB:
            kpos = lax.broadcasted_iota(jnp.int32, st.shape, 0) + band * K_TILE
            qpos = lax.broadcasted_iota(jnp.int32, st.shape, 1) + qb * Q_SUB
            st = jnp.where(kpos <= qpos, st, NEG_INF)
        s_scr[u, 0:nkeys, :] = st
        m_prev = jnp.full((1, Q_SUB), NEG_INF, F32) if fresh else m_scr[u, 1]
        m_scr[u, 0] = m_prev
        m_scr[u, 1] = jnp.maximum(m_prev, jnp.max(st, axis=0, keepdims=True))

    def value_unit(pair, j, u, band):
        h, qb = units[u]
        nkeys = band_keys(qb, band)
        if nkeys == 0:
            return
        m_new = m_scr[u, 1]
        alpha = jnp.exp2(m_scr[u, 0] - m_new)
        pt = jnp.exp2((s_scr[u, 0:nkeys, :] - m_new).astype(BF16))
        acc_scr[u] = alpha * acc_scr[u] + _mm(vt_scr[j, 2 * pair + h, :, 0:nkeys], pt)

    def stages(value=None, score=None, value_band=None, score_band=None, fresh=False, qt=None):
        for u in range(len(units)):
            if value is not None:
                value_unit(value[0], value[1], u, value_band)
            if score is not None:
                score_unit(score[0], score[1], u, score_band, fresh, qt_scr.at[score[0]] if qt is None else qt)

    @pl.when(i == 0)
    def _():
        tail = (lax.broadcasted_iota(jnp.int32, (VT_ROWS - FOX_HEAD_DIM, K_TILE), 0) == 0).astype(BF16)
        for c in range(nk):
            for pair in range(npairs):
                blk = v_ref[0, c * K_TILE:(c + 1) * K_TILE, pair * LANES:(pair + 1) * LANES]
                blk = blk.astype(F32).T.astype(BF16)
                for h in range(2):
                    vt_scr[c, 2 * pair + h, 0:FOX_HEAD_DIM, :] = blk[h * FOX_HEAD_DIM:(h + 1) * FOX_HEAD_DIM, :]
                    vt_scr[c, 2 * pair + h, FOX_HEAD_DIM:VT_ROWS, :] = tail

    for pair in range(npairs):
        transpose_q(q_ref, 2 * pair, qt_scr.at[pair])

    @pl.when(i == 0)
    def _():
        stages(score=(0, 0), score_band=0, fresh=True)

    for pair in range(npairs):
        acc_scr[...] = jnp.zeros(acc_scr.shape, F32)

        def trip(t, c, pair=pair):
            stages(value=(pair, t), score=(pair, t + 1))
            return c

        lax.fori_loop(0, first_diag - 1, trip, 0)

        @pl.when(i > 0)
        def _(pair=pair):
            stages(value=(pair, first_diag - 1), score=(pair, first_diag), score_band=0)

        for band in range(DIAG - 1):
            stages(value=(pair, first_diag + band), score=(pair, first_diag + band + 1),
                   value_band=band, score_band=band + 1)

        last = (pair, first_diag + DIAG - 1)
        if pair + 1 < npairs:
            @pl.when(i == 0)
            def _(pair=pair, last=last):
                stages(value=last, score=(pair + 1, 0), value_band=DIAG - 1, score_band=0, fresh=True)

            @pl.when(i > 0)
            def _(pair=pair, last=last):
                stages(value=last, score=(pair + 1, 0), value_band=DIAG - 1, fresh=True)
        else:
            @pl.when(i < nq - 1)
            def _(last=last):
                transpose_q(qn_ref, 0, qtn_scr)
                stages(value=last, score=(0, 0), value_band=DIAG - 1, fresh=True, qt=qtn_scr)

            @pl.when(i == nq - 1)
            def _(last=last):
                stages(value=last, value_band=DIAG - 1)

        rows = []
        for h in range(2):
            blocks = [acc_scr[u] for u in range(len(units)) if units[u][0] == h]
            rows.append(jnp.concatenate(
                [a[0:FOX_HEAD_DIM] / a[FOX_HEAD_DIM:FOX_HEAD_DIM + 1] for a in blocks], axis=1))
        pair_out = jnp.concatenate(rows, axis=0).T.astype(BF16)
        o_ref[0, :, pair * LANES:(pair + 1) * LANES] = pair_out


def _fox_attention(q_aug, k_aug, v):
    bsz, _, seq, _ = q_aug.shape
    nq = seq // Q_TILE
    nk = seq // K_TILE
    return pl.pallas_call(
        _fox_kernel,
        grid=(bsz, nq),
        in_specs=[
            pl.BlockSpec((1, FOX_HEADS, Q_TILE, LANES), lambda b, i: (b, 0, i, 0)),
            pl.BlockSpec((1, 2, Q_TILE, LANES), lambda b, i: (b, 0, jnp.minimum(i + 1, nq - 1), 0)),
            pl.BlockSpec((1, FOX_HEADS, seq, LANES), lambda b, i: (b, 0, 0, 0)),
            pl.BlockSpec((1, seq, FOX_WIDTH), lambda b, i: (b, 0, 0)),
        ],
        out_specs=pl.BlockSpec((1, Q_TILE, FOX_WIDTH), lambda b, i: (b, i, 0)),
        out_shape=jax.ShapeDtypeStruct((bsz, seq, FOX_WIDTH), BF16),
        scratch_shapes=[
            pltpu.VMEM((nk, FOX_HEADS, VT_ROWS, K_TILE), BF16),
            pltpu.VMEM((FOX_HEADS // 2, 2, LANES, Q_TILE), BF16),
            pltpu.VMEM((2, LANES, Q_TILE), BF16),
            pltpu.VMEM((2 * (Q_TILE // Q_SUB), VT_ROWS, Q_SUB), F32),
            pltpu.VMEM((2 * (Q_TILE // Q_SUB), K_TILE, Q_SUB), F32),
            pltpu.VMEM((2 * (Q_TILE // Q_SUB), 2, 1, Q_SUB), F32),
        ],
        compiler_params=pltpu.CompilerParams(
            dimension_semantics=("arbitrary", "arbitrary"), vmem_limit_bytes=VMEM_LIMIT),
        name="fox_attention",
    )(q_aug, q_aug, k_aug, v)


def _memkv_kernel(mem_ref, wk_ref, wv_ref, k_ref, v_ref):
    mb = mem_ref[0].astype(BF16)
    k_ref[0] = _mm(mb, wk_ref[...]).astype(BF16)
    v_ref[0] = _mm(mb, wv_ref[...]).astype(BF16)


def _mem_kv(mem, w_ck, w_cv):
    bsz = mem.shape[0]
    const2 = lambda b: (0, 0)
    blk = pl.BlockSpec((1, MEM_LEN, D_MODEL), lambda b: (b, 0, 0))
    return pl.pallas_call(
        _memkv_kernel,
        grid=(bsz,),
        in_specs=[blk, pl.BlockSpec(w_ck.shape, const2), pl.BlockSpec(w_cv.shape, const2)],
        out_specs=[blk, blk],
        out_shape=[jax.ShapeDtypeStruct((bsz, MEM_LEN, D_MODEL), BF16)] * 2,
        compiler_params=pltpu.CompilerParams(
            dimension_semantics=("arbitrary",), vmem_limit_bytes=VMEM_LIMIT),
        name="mem_kv",
    )(mem, w_ck, w_cv)


def _conv_branch(u_ref, cw_ref, cb_ref, cg_ref, cbeta_ref, ubuf, shifted):
    t = pl.program_id(1)
    rows = u_ref.shape[1]
    first = HALO - (CONV_K - 1)
    span = shifted.shape[1]

    @pl.when(t == 0)
    def _():
        ubuf[0:HALO, :] = jnp.zeros((HALO, CONV_WIDTH), F32)

    @pl.when(t > 0)
    def _():
        ubuf[0:HALO, :] = ubuf[rows:rows + HALO, :]

    ubuf[HALO:HALO + rows, :] = u_ref[0]
    for p in range(1, SUBLANES):
        shifted[p - 1] = ubuf[p:p + span, :]

    cw = cw_ref[...]
    cbias = cb_ref[...]
    gam = cg_ref[...]
    beta = cbeta_ref[...]
    outs = []
    for r in range(rows // CONV_ROWS):
        base = r * CONV_ROWS
        acc = jnp.broadcast_to(cbias, (CONV_ROWS, CONV_WIDTH))
        for j in range(CONV_K):
            a, p = divmod(first + j, SUBLANES)
            lo = base + SUBLANES * a
            src = ubuf[lo:lo + CONV_ROWS, :] if p == 0 else shifted[p - 1, lo:lo + CONV_ROWS, :]
            acc = acc + cw[j:j + 1, :] * src
        y = _layer_norm(acc, gam, beta)
        outs.append((y * _sigmoid(y)).astype(BF16))
    return jnp.concatenate(outs, axis=0)


def _mix_cross_kernel(x_ref, u_ref, o_ref, cw_ref, cb_ref, cg_ref, cbeta_ref, wout_ref, g1_ref, b1_ref,
                      wcq_ref, kc_ref, vc_ref, wco_ref, g2_ref, b2_ref, h_ref, ubuf, shifted):
    uc = _conv_branch(u_ref, cw_ref, cb_ref, cg_ref, cbeta_ref, ubuf, shifted)
    mix = _mm(uc, wout_ref[0:CONV_WIDTH, :]) + _mm(o_ref[0], wout_ref[CONV_WIDTH:, :])
    h1 = _layer_norm(DEEPNORM_ALPHA * x_ref[0] + mix, g1_ref[...], b1_ref[...])

    q = (_mm(h1.astype(BF16), wcq_ref[...]) * (1.0 / math.sqrt(MEM_HEAD_DIM))).astype(BF16)
    outs = []
    for hh in range(MEM_HEADS):
        cols = slice(hh * MEM_HEAD_DIM, (hh + 1) * MEM_HEAD_DIM)
        s = _mm_nt(q[:, cols], kc_ref[0, :, cols])
        p = jnp.exp(s - jnp.max(s, axis=-1, keepdims=True))
        l = jnp.sum(p, axis=-1, keepdims=True)
        outs.append((_mm(p.astype(BF16), vc_ref[0, :, cols]) / l).astype(BF16))
    o = jnp.concatenate(outs, axis=-1)
    y = _mm(o, wco_ref[...])
    h_ref[0] = _layer_norm(DEEPNORM_ALPHA * h1 + y, g2_ref[...], b2_ref[...])


def _mix_cross(x, u, o, conv_w, conv_b, conv_g, conv_beta, w_out, g1, b1, w_cq, kc, vc, w_co, g2, b2):
    bsz, seq, _ = x.shape
    nt = seq // ROW_TILE
    const2 = lambda b, t: (0, 0)
    tok = lambda width: pl.BlockSpec((1, ROW_TILE, width), lambda b, t: (b, t, 0))
    memblk = pl.BlockSpec((1, MEM_LEN, D_MODEL), lambda b, t: (b, 0, 0))
    vec = pl.BlockSpec((1, D_MODEL), const2)
    cvec = pl.BlockSpec((1, CONV_WIDTH), const2)
    mat = pl.BlockSpec((D_MODEL, D_MODEL), const2)
    return pl.pallas_call(
        _mix_cross_kernel,
        grid=(bsz, nt),
        in_specs=[tok(D_MODEL), tok(CONV_WIDTH), tok(FOX_WIDTH), pl.BlockSpec(conv_w.shape, const2), cvec, cvec,
                  cvec, mat, vec, vec, mat, memblk, memblk, mat, vec, vec],
        out_specs=tok(D_MODEL),
        out_shape=jax.ShapeDtypeStruct((bsz, seq, D_MODEL), F32),
        scratch_shapes=[
            pltpu.VMEM((ROW_TILE + HALO, CONV_WIDTH), F32),
            pltpu.VMEM((SUBLANES - 1, ROW_TILE + HALO - SUBLANES, CONV_WIDTH), F32),
        ],
        compiler_params=pltpu.CompilerParams(
            dimension_semantics=("arbitrary", "arbitrary"), vmem_limit_bytes=VMEM_LIMIT),
        name="mix_cross",
    )(x, u, o, conv_w, conv_b, conv_g, conv_beta, w_out, g1, b1, w_cq, kc, vc, w_co, g2, b2)


def _swiglu_kernel(h_ref, wg_ref, wu_ref, wd_ref, g_ref, b_ref, out_ref):
    h = h_ref[...]
    hb = h.astype(BF16)
    acc = DEEPNORM_ALPHA * h
    for c in range(D_FF // FF_CHUNK):
        cols = slice(c * FF_CHUNK, (c + 1) * FF_CHUNK)
        gate = _mm(hb, wg_ref[:, cols])
        up = _mm(hb, wu_ref[:, cols])
        act = (gate * _sigmoid(gate) * up).astype(BF16)
        acc = acc + _mm(act, wd_ref[cols, :])
    out_ref[...] = _layer_norm(acc, g_ref[...], b_ref[...])


def _swiglu(h, w_gate, w_up, w_down, g, b):
    n = h.shape[0]
    const2 = lambda t: (0, 0)
    tok = pl.BlockSpec((FF_ROW_TILE, D_MODEL), lambda t: (t, 0))
    return pl.pallas_call(
        _swiglu_kernel,
        grid=(n // FF_ROW_TILE,),
        in_specs=[tok, pl.BlockSpec(w_gate.shape, const2), pl.BlockSpec(w_up.shape, const2),
                  pl.BlockSpec(w_down.shape, const2), pl.BlockSpec((1, D_MODEL), const2),
                  pl.BlockSpec((1, D_MODEL), const2)],
        out_specs=tok,
        out_shape=jax.ShapeDtypeStruct((n, D_MODEL), F32),
        compiler_params=pltpu.CompilerParams(
            dimension_semantics=("arbitrary",), vmem_limit_bytes=VMEM_LIMIT),
        name="swiglu",
    )(h, w_gate, w_up, w_down, g, b)


def _layer(h, mem, w_in, b_forget, conv_w, conv_b, conv_ln_g, conv_ln_b, w_out, ln_mix_g, ln_mix_b,
           w_cq, w_ck, w_cv, w_co, ln_cross_g, ln_cross_b, w_gate, w_up, w_down, ln_ffn_g, ln_ffn_b):
    bsz, seq, _ = h.shape
    row = lambda v: v.reshape(1, -1).astype(F32)
    w_main = w_in[:, :C_F].astype(BF16)
    w_f = jnp.pad(w_in[:, C_F:], ((0, 0), (0, LANES - FOX_HEADS))).astype(BF16)
    b_f = jnp.pad(b_forget.astype(F32), (0, LANES - FOX_HEADS)).reshape(1, LANES)
    tri = jnp.tri(ROW_TILE, dtype=BF16)

    u, q_aug, k_aug, v = _inproj(h, w_main, w_f, b_f, tri)
    o = _fox_attention(q_aug, k_aug, v)
    kc, vc = _mem_kv(mem, w_ck.astype(BF16), w_cv.astype(BF16))
    h2 = _mix_cross(h, u, o, conv_w.astype(F32), row(conv_b), row(conv_ln_g), row(conv_ln_b),
                    w_out.astype(BF16), row(ln_mix_g), row(ln_mix_b), w_cq.astype(BF16), kc, vc,
                    w_co.astype(BF16), row(ln_cross_g), row(ln_cross_b))
    h3 = _swiglu(h2.reshape(bsz * seq, D_MODEL), w_gate.astype(BF16), w_up.astype(BF16),
                 w_down.astype(BF16), row(ln_ffn_g), row(ln_ffn_b))
    return h3.reshape(bsz, seq, D_MODEL)


def kernel(x, mem, w_in, b_forget, conv_w, conv_b, conv_ln_g, conv_ln_b, w_out, ln_mix_g, ln_mix_b,
           w_cq, w_ck, w_cv, w_co, ln_cross_g, ln_cross_b, w_gate, w_up, w_down, ln_ffn_g, ln_ffn_b):
    depth = w_in.shape[0]
    assert depth == 1, "DEEPNORM_ALPHA is fixed for a single layer"
    h = x
    for l in range(depth):
        h = _layer(h, mem, w_in[l], b_forget[l], conv_w[l], conv_b[l], conv_ln_g[l], conv_ln_b[l],
                   w_out[l], ln_mix_g[l], ln_mix_b[l], w_cq[l], w_ck[l], w_cv[l], w_co[l],
                   ln_cross_g[l], ln_cross_b[l], w_gate[l], w_up[l], w_down[l], ln_ffn_g[l], ln_ffn_b[l])
    return h
```

```python
import functools
import math

import jax
import jax.numpy as jnp
from jax import lax
from jax.experimental import pallas as pl
from jax.experimental.pallas import tpu as pltpu

D_MODEL = 1024
CONV_WIDTH = 512
CONV_K = 31
FOX_WIDTH = 512
FOX_HEADS = 8
FOX_HEAD_DIM = 64
MEM_LEN = 256
MEM_HEADS = 4
MEM_HEAD_DIM = 256
D_FF = 2816
LN_EPS = 1e-5
NEG_INF = -1e30
DEEPNORM_ALPHA = 2.0 ** 0.25
LOG2E = math.log2(math.e)

LANES = 128
SUBLANES = 8
HALO = 32
ROW_TILE = 512
FF_ROW_TILE = 1024
Q_TILE = 1024
K_TILE = 512
DIAG = Q_TILE // K_TILE
Q_SUB = 256
EXP_ROWS = 64
VT_ROWS = FOX_HEAD_DIM + 16
FF_CHUNK = 256
CONV_ROWS = 32
VMEM_LIMIT = 56 * 1024 * 1024

BF16 = jnp.bfloat16
F32 = jnp.float32

C_GLU_A = 0
C_GLU_B = CONV_WIDTH
C_Q = 2 * CONV_WIDTH
C_K = C_Q + FOX_WIDTH
C_V = C_K + FOX_WIDTH
C_F = C_V + FOX_WIDTH


def _mm(a, b):
    return jnp.dot(a, b, preferred_element_type=F32)


def _mm_nt(a, b):
    return lax.dot_general(a, b, (((1,), (1,)), ((), ())), preferred_element_type=F32)


def _layer_norm(x, g, b):
    mu = jnp.mean(x, axis=-1, keepdims=True)
    xc = x - mu
    var = jnp.mean(xc * xc, axis=-1, keepdims=True)
    return xc * lax.rsqrt(var + LN_EPS) * g + b


def _sigmoid(x):
    return 1.0 / (1.0 + jnp.exp(-x))


def _split3(x):
    hi = x.astype(BF16).astype(F32)
    r = x - hi
    mid = r.astype(BF16).astype(F32)
    lo = (r - mid).astype(BF16).astype(F32)
    return hi, mid, lo


def _pack3(hi, mid, lo):
    return hi + pltpu.roll(mid, 8, axis=1) + pltpu.roll(lo, 16, axis=1)


def _inproj_kernel(x_ref, w_ref, wf_ref, bf_ref, tri_ref, *refs, n_cast):
    cast_in, (u_ref, q_ref, k_ref, v_ref), cast_out = refs[:n_cast], refs[n_cast:n_cast + 4], refs[n_cast + 4:-1]
    carry = refs[-1]
    t = pl.program_id(1)
    rows = x_ref.shape[1]
    xb = x_ref[0].astype(BF16)

    for src_ref, dst_ref in zip(cast_in, cast_out):
        dst_ref[...] = src_ref[...].astype(BF16)

    @pl.when(t == 0)
    def _():
        carry[...] = jnp.zeros_like(carry)

    glu_a = _mm(xb, w_ref[:, C_GLU_A:C_GLU_A + CONV_WIDTH])
    glu_b = _mm(xb, w_ref[:, C_GLU_B:C_GLU_B + CONV_WIDTH])
    u_ref[0] = glu_a * _sigmoid(glu_b)

    lane = lax.broadcasted_iota(jnp.int32, (rows, LANES), 1)
    f = _mm(xb, wf_ref[...]) + bf_ref[...]
    logf = jnp.minimum(f, 0.0) - jnp.log(1.0 + jnp.exp(-jnp.abs(f)))
    logf = jnp.where(lane < FOX_HEADS, logf, 0.0)
    packed = _pack3(*_split3(logf)).astype(BF16)
    res = _mm(tri_ref[...], packed)
    cum = res + pltpu.roll(res, LANES - 8, axis=1) + pltpu.roll(res, LANES - 16, axis=1)
    cum = jnp.where(lane < FOX_HEADS, cum + carry[0:1, :], 0.0)
    carry[0:1, :] = cum[rows - 1:rows, :]
    cpack = _pack3(*_split3(cum * LOG2E))

    scale = LOG2E / math.sqrt(FOX_HEAD_DIM)
    for hp in range(FOX_HEADS // 2):
        if hp % 2 == 0:
            qquad = _mm(xb, w_ref[:, C_Q + hp * LANES:C_Q + (hp + 2) * LANES]) * scale
            kquad = _mm(xb, w_ref[:, C_K + hp * LANES:C_K + (hp + 2) * LANES])
        qpair = qquad[:, (hp % 2) * LANES:(hp % 2 + 1) * LANES]
        kpair = kquad[:, (hp % 2) * LANES:(hp % 2 + 1) * LANES]
        for sub in range(2):
            h = 2 * hp + sub
            aug0 = FOX_HEAD_DIM if sub == 0 else 0
            data = (lane < FOX_HEAD_DIM) if sub == 0 else (lane >= FOX_HEAD_DIM)
            slot_a = (lane == aug0) | (lane == aug0 + 8) | (lane == aug0 + 16)
            slot_b = (lane == aug0 + 1) | (lane == aug0 + 9) | (lane == aug0 + 17)
            ra = pltpu.roll(cpack, (aug0 - h) % LANES, axis=1)
            rb = pltpu.roll(cpack, (aug0 + 1 - h) % LANES, axis=1)
            qa = jnp.where(data, qpair, jnp.where(slot_a, ra, jnp.where(slot_b, 1.0, 0.0)))
            ka = jnp.where(data, kpair, jnp.where(slot_a, 1.0, jnp.where(slot_b, -rb, 0.0)))
            q_ref[0, h] = qa.astype(BF16)
            k_ref[0, h] = ka.astype(BF16)
    v_ref[0] = _mm(xb, w_ref[:, C_V:C_V + FOX_WIDTH]).astype(BF16)


def _slab_spec(rows, cols, steps, steps_per_seq):
    nslabs = steps
    while rows % nslabs or (rows // nslabs) % (2 * SUBLANES):
        nslabs //= 2
    hold = steps // nslabs
    return pl.BlockSpec((rows // nslabs, cols), lambda b, t: ((b * steps_per_seq + t) // hold, 0))


def _inproj(x, w_main, w_f, b_f, tri, to_cast):
    bsz, seq, _ = x.shape
    nt = seq // ROW_TILE
    const2 = lambda b, t: (0, 0)
    slabs = [_slab_spec(w.shape[0], w.shape[1], bsz * nt, nt) for w in to_cast]
    outs = pl.pallas_call(
        functools.partial(_inproj_kernel, n_cast=len(to_cast)),
        grid=(bsz, nt),
        in_specs=[
            pl.BlockSpec((1, ROW_TILE, D_MODEL), lambda b, t: (b, t, 0)),
            pl.BlockSpec(w_main.shape, const2),
            pl.BlockSpec(w_f.shape, const2),
            pl.BlockSpec(b_f.shape, const2),
            pl.BlockSpec(tri.shape, const2),
        ] + slabs,
        out_specs=[
            pl.BlockSpec((1, ROW_TILE, CONV_WIDTH), lambda b, t: (b, t, 0)),
            pl.BlockSpec((1, FOX_HEADS, ROW_TILE, LANES), lambda b, t: (b, 0, t, 0)),
            pl.BlockSpec((1, FOX_HEADS, ROW_TILE, LANES), lambda b, t: (b, 0, t, 0)),
            pl.BlockSpec((1, ROW_TILE, FOX_WIDTH), lambda b, t: (b, t, 0)),
        ] + slabs,
        out_shape=[
            jax.ShapeDtypeStruct((bsz, seq, CONV_WIDTH), F32),
            jax.ShapeDtypeStruct((bsz, FOX_HEADS, seq, LANES), BF16),
            jax.ShapeDtypeStruct((bsz, FOX_HEADS, seq, LANES), BF16),
            jax.ShapeDtypeStruct((bsz, seq, FOX_WIDTH), BF16),
        ] + [jax.ShapeDtypeStruct(w.shape, BF16) for w in to_cast],
        scratch_shapes=[pltpu.VMEM((SUBLANES, LANES), F32)],
        compiler_params=pltpu.CompilerParams(
            dimension_semantics=("arbitrary", "arbitrary"), vmem_limit_bytes=VMEM_LIMIT),
        name="inproj",
    )(x, w_main, w_f, b_f, tri, *to_cast)
    return outs[:4], outs[4:]


def _fox_kernel(q_ref, qn_ref, k_ref, v_ref, o_ref, vt_scr, qt_scr, qtn_scr, acc_scr, s_scr, m_scr):
    i = pl.program_id(1)
    nq = pl.num_programs(1)
    nk = vt_scr.shape[0]
    npairs = FOX_HEADS // 2
    units = [(h, qb) for h in range(2) for qb in range(Q_TILE // Q_SUB)]
    first_diag = DIAG * i

    def band_keys(qb, band):
        if band is None:
            return K_TILE
        return max(0, min(K_TILE, (qb + 1) * Q_SUB - band * K_TILE))

    def transpose_q(src_ref, head0, dst):
        for h in range(2):
            dst[h] = src_ref[0, head0 + h].astype(F32).T.astype(BF16)

    def score_unit(pair, j, u, band, fresh, qt):
        h, qb = units[u]
        nkeys = band_keys(qb, band)
        if nkeys == 0:
            return
        kt = k_ref[0, 2 * pair + h, pl.ds(pl.multiple_of(j * K_TILE, K_TILE), nkeys), :]
        st = _mm(kt, qt[h, :, qb * Q_SUB:(qb + 1) * Q_SUB])
        if band is not None and band * K_TILE + nkeys - 1 > qb * Q_SUB:
            kpos = lax.broadcasted_iota(jnp.int32, st.shape, 0) + band * K_TILE
            qpos = lax.broadcasted_iota(jnp.int32, st.shape, 1) + qb * Q_SUB
            st = jnp.where(kpos <= qpos, st, NEG_INF)
        s_scr[u, 0:nkeys, :] = st
        m_prev = jnp.full((1, Q_SUB), NEG_INF, F32) if fresh else m_scr[u, 1]
        m_scr[u, 0] = m_prev
        m_scr[u, 1] = jnp.maximum(m_prev, jnp.max(st, axis=0, keepdims=True))

    def value_unit(pair, j, u, band):
        h, qb = units[u]
        nkeys = band_keys(qb, band)
        if nkeys == 0:
            return
        m_new = m_scr[u, 1]
        alpha = jnp.exp2(m_scr[u, 0] - m_new)
        pt = jnp.exp2((s_scr[u, 0:nkeys, :] - m_new).astype(BF16))
        acc_scr[u] = alpha * acc_scr[u] + _mm(vt_scr[j, 2 * pair + h, :, 0:nkeys], pt)

    def stages(value=None, score=None, value_band=None, score_band=None, fresh=False, qt=None):
        for u in range(len(units)):
            if value is not None:
                value_unit(value[0], value[1], u, value_band)
            if score is not None:
                score_unit(score[0], score[1], u, score_band, fresh, qt_scr.at[score[0]] if qt is None else qt)

    @pl.when(i == 0)
    def _():
        tail = (lax.broadcasted_iota(jnp.int32, (VT_ROWS - FOX_HEAD_DIM, K_TILE), 0) == 0).astype(BF16)
        for c in range(nk):
            for pair in range(npairs):
                blk = v_ref[0, c * K_TILE:(c + 1) * K_TILE, pair * LANES:(pair + 1) * LANES]
                blk = blk.astype(F32).T.astype(BF16)
                for h in range(2):
                    vt_scr[c, 2 * pair + h, 0:FOX_HEAD_DIM, :] = blk[h * FOX_HEAD_DIM:(h + 1) * FOX_HEAD_DIM, :]
                    vt_scr[c, 2 * pair + h, FOX_HEAD_DIM:VT_ROWS, :] = tail

    for pair in range(npairs):
        transpose_q(q_ref, 2 * pair, qt_scr.at[pair])

    @pl.when(i == 0)
    def _():
        stages(score=(0, 0), score_band=0, fresh=True)

    for pair in range(npairs):
        acc_scr[...] = jnp.zeros(acc_scr.shape, F32)

        def trip(t, c, pair=pair):
            stages(value=(pair, t), score=(pair, t + 1))
            return c

        lax.fori_loop(0, first_diag - 1, trip, 0)

        @pl.when(i > 0)
        def _(pair=pair):
            stages(value=(pair, first_diag - 1), score=(pair, first_diag), score_band=0)

        for band in range(DIAG - 1):
            stages(value=(pair, first_diag + band), score=(pair, first_diag + band + 1),
                   value_band=band, score_band=band + 1)

        last = (pair, first_diag + DIAG - 1)
        if pair + 1 < npairs:
            @pl.when(i == 0)
            def _(pair=pair, last=last):
                stages(value=last, score=(pair + 1, 0), value_band=DIAG - 1, score_band=0, fresh=True)

            @pl.when(i > 0)
            def _(pair=pair, last=last):
                stages(value=last, score=(pair + 1, 0), value_band=DIAG - 1, fresh=True)
        else:
            @pl.when(i < nq - 1)
            def _(last=last):
                transpose_q(qn_ref, 0, qtn_scr)
                stages(value=last, score=(0, 0), value_band=DIAG - 1, fresh=True, qt=qtn_scr)

            @pl.when(i == nq - 1)
            def _(last=last):
                stages(value=last, value_band=DIAG - 1)

        rows = []
        for h in range(2):
            blocks = [acc_scr[u] for u in range(len(units)) if units[u][0] == h]
            rows.append(jnp.concatenate(
                [a[0:FOX_HEAD_DIM] / a[FOX_HEAD_DIM:FOX_HEAD_DIM + 1] for a in blocks], axis=1))
        pair_out = jnp.concatenate(rows, axis=0).T.astype(BF16)
        o_ref[0, :, pair * LANES:(pair + 1) * LANES] = pair_out


def _fox_attention(q_aug, k_aug, v):
    bsz, _, seq, _ = q_aug.shape
    nq = seq // Q_TILE
    nk = seq // K_TILE
    return pl.pallas_call(
        _fox_kernel,
        grid=(bsz, nq),
        in_specs=[
            pl.BlockSpec((1, FOX_HEADS, Q_TILE, LANES), lambda b, i: (b, 0, i, 0)),
            pl.BlockSpec((1, 2, Q_TILE, LANES), lambda b, i: (b, 0, jnp.minimum(i + 1, nq - 1), 0)),
            pl.BlockSpec((1, FOX_HEADS, seq, LANES), lambda b, i: (b, 0, 0, 0)),
            pl.BlockSpec((1, seq, FOX_WIDTH), lambda b, i: (b, 0, 0)),
        ],
        out_specs=pl.BlockSpec((1, Q_TILE, FOX_WIDTH), lambda b, i: (b, i, 0)),
        out_shape=jax.ShapeDtypeStruct((bsz, seq, FOX_WIDTH), BF16),
        scratch_shapes=[
            pltpu.VMEM((nk, FOX_HEADS, VT_ROWS, K_TILE), BF16),
            pltpu.VMEM((FOX_HEADS // 2, 2, LANES, Q_TILE), BF16),
            pltpu.VMEM((2, LANES, Q_TILE), BF16),
            pltpu.VMEM((2 * (Q_TILE // Q_SUB), VT_ROWS, Q_SUB), F32),
            pltpu.VMEM((2 * (Q_TILE // Q_SUB), K_TILE, Q_SUB), F32),
            pltpu.VMEM((2 * (Q_TILE // Q_SUB), 2, 1, Q_SUB), F32),
        ],
        compiler_params=pltpu.CompilerParams(
            dimension_semantics=("arbitrary", "arbitrary"), vmem_limit_bytes=VMEM_LIMIT),
        name="fox_attention",
    )(q_aug, q_aug, k_aug, v)


def _memkv_kernel(mem_ref, wk_ref, wv_ref, k_ref, v_ref):
    mb = mem_ref[0].astype(BF16)
    k_ref[0] = _mm(mb, wk_ref[...]).astype(BF16)
    v_ref[0] = _mm(mb, wv_ref[...]).astype(BF16)


def _mem_kv(mem, w_ck, w_cv):
    bsz = mem.shape[0]
    const2 = lambda b: (0, 0)
    blk = pl.BlockSpec((1, MEM_LEN, D_MODEL), lambda b: (b, 0, 0))
    return pl.pallas_call(
        _memkv_kernel,
        grid=(bsz,),
        in_specs=[blk, pl.BlockSpec(w_ck.shape, const2), pl.BlockSpec(w_cv.shape, const2)],
        out_specs=[blk, blk],
        out_shape=[jax.ShapeDtypeStruct((bsz, MEM_LEN, D_MODEL), BF16)] * 2,
        compiler_params=pltpu.CompilerParams(
            dimension_semantics=("arbitrary",), vmem_limit_bytes=VMEM_LIMIT),
        name="mem_kv",
    )(mem, w_ck, w_cv)


def _conv_branch(u_ref, cw_ref, cb_ref, cg_ref, cbeta_ref, ubuf, shifted):
    t = pl.program_id(1)
    rows = u_ref.shape[1]
    first = HALO - (CONV_K - 1)
    span = shifted.shape[1]

    @pl.when(t == 0)
    def _():
        ubuf[0:HALO, :] = jnp.zeros((HALO, CONV_WIDTH), F32)

    @pl.when(t > 0)
    def _():
        ubuf[0:HALO, :] = ubuf[rows:rows + HALO, :]

    ubuf[HALO:HALO + rows, :] = u_ref[0]
    for p in range(1, SUBLANES):
        shifted[p - 1] = ubuf[p:p + span, :]

    cw = cw_ref[...]
    cbias = cb_ref[...]
    gam = cg_ref[...]
    beta = cbeta_ref[...]
    outs = []
    for r in range(rows // CONV_ROWS):
        base = r * CONV_ROWS
        acc = jnp.broadcast_to(cbias, (CONV_ROWS, CONV_WIDTH))
        for j in range(CONV_K):
            a, p = divmod(first + j, SUBLANES)
            lo = base + SUBLANES * a
            src = ubuf[lo:lo + CONV_ROWS, :] if p == 0 else shifted[p - 1, lo:lo + CONV_ROWS, :]
            acc = acc + cw[j:j + 1, :] * src
        y = _layer_norm(acc, gam, beta)
        outs.append((y * _sigmoid(y)).astype(BF16))
    return jnp.concatenate(outs, axis=0)


def _mix_cross_kernel(x_ref, u_ref, o_ref, cw_ref, cb_ref, cg_ref, cbeta_ref, wout_ref, g1_ref, b1_ref,
                      wcq_ref, kc_ref, vc_ref, wco_ref, g2_ref, b2_ref, h_ref, ubuf, shifted):
    uc = _conv_branch(u_ref, cw_ref, cb_ref, cg_ref, cbeta_ref, ubuf, shifted)
    mix = _mm(uc, wout_ref[0:CONV_WIDTH, :]) + _mm(o_ref[0], wout_ref[CONV_WIDTH:, :])
    h1 = _layer_norm(DEEPNORM_ALPHA * x_ref[0] + mix, g1_ref[...], b1_ref[...])

    q = (_mm(h1.astype(BF16), wcq_ref[...]) * (1.0 / math.sqrt(MEM_HEAD_DIM))).astype(BF16)
    outs = []
    for hh in range(MEM_HEADS):
        cols = slice(hh * MEM_HEAD_DIM, (hh + 1) * MEM_HEAD_DIM)
        s = _mm_nt(q[:, cols], kc_ref[0, :, cols])
        p = jnp.exp(s - jnp.max(s, axis=-1, keepdims=True))
        l = jnp.sum(p, axis=-1, keepdims=True)
        outs.append((_mm(p.astype(BF16), vc_ref[0, :, cols]) / l).astype(BF16))
    o = jnp.concatenate(outs, axis=-1)
    y = _mm(o, wco_ref[...])
    h_ref[0] = _layer_norm(DEEPNORM_ALPHA * h1 + y, g2_ref[...], b2_ref[...])


def _mix_cross(x, u, o, conv_w, conv_b, conv_g, conv_beta, w_out, g1, b1, w_cq, kc, vc, w_co, g2, b2):
    bsz, seq, _ = x.shape
    nt = seq // ROW_TILE
    const2 = lambda b, t: (0, 0)
    tok = lambda width: pl.BlockSpec((1, ROW_TILE, width), lambda b, t: (b, t, 0))
    memblk = pl.BlockSpec((1, MEM_LEN, D_MODEL), lambda b, t: (b, 0, 0))
    vec = pl.BlockSpec((1, D_MODEL), const2)
    cvec = pl.BlockSpec((1, CONV_WIDTH), const2)
    mat = pl.BlockSpec((D_MODEL, D_MODEL), const2)
    return pl.pallas_call(
        _mix_cross_kernel,
        grid=(bsz, nt),
        in_specs=[tok(D_MODEL), tok(CONV_WIDTH), tok(FOX_WIDTH), pl.BlockSpec(conv_w.shape, const2), cvec, cvec,
                  cvec, mat, vec, vec, mat, memblk, memblk, mat, vec, vec],
        out_specs=tok(D_MODEL),
        out_shape=jax.ShapeDtypeStruct((bsz, seq, D_MODEL), F32),
        scratch_shapes=[
            pltpu.VMEM((ROW_TILE + HALO, CONV_WIDTH), F32),
            pltpu.VMEM((SUBLANES - 1, ROW_TILE + HALO - SUBLANES, CONV_WIDTH), F32),
        ],
        compiler_params=pltpu.CompilerParams(
            dimension_semantics=("arbitrary", "arbitrary"), vmem_limit_bytes=VMEM_LIMIT),
        name="mix_cross",
    )(x, u, o, conv_w, conv_b, conv_g, conv_beta, w_out, g1, b1, w_cq, kc, vc, w_co, g2, b2)


def _swiglu_kernel(h_ref, wg_ref, wu_ref, wd_ref, g_ref, b_ref, out_ref):
    h = h_ref[...]
    hb = h.astype(BF16)
    acc = DEEPNORM_ALPHA * h
    for c in range(D_FF // FF_CHUNK):
        cols = slice(c * FF_CHUNK, (c + 1) * FF_CHUNK)
        gate = _mm(hb, wg_ref[:, cols])
        up = _mm(hb, wu_ref[:, cols])
        act = (gate * _sigmoid(gate) * up).astype(BF16)
        acc = acc + _mm(act, wd_ref[cols, :])
    out_ref[...] = _layer_norm(acc, g_ref[...], b_ref[...])


def _swiglu(h, w_gate, w_up, w_down, g, b):
    n = h.shape[0]
    const2 = lambda t: (0, 0)
    tok = pl.BlockSpec((FF_ROW_TILE, D_MODEL), lambda t: (t, 0))
    return pl.pallas_call(
        _swiglu_kernel,
        grid=(n // FF_ROW_TILE,),
        in_specs=[tok, pl.BlockSpec(w_gate.shape, const2), pl.BlockSpec(w_up.shape, const2),
                  pl.BlockSpec(w_down.shape, const2), pl.BlockSpec((1, D_MODEL), const2),
                  pl.BlockSpec((1, D_MODEL), const2)],
        out_specs=tok,
        out_shape=jax.ShapeDtypeStruct((n, D_MODEL), F32),
        compiler_params=pltpu.CompilerParams(
            dimension_semantics=("arbitrary",), vmem_limit_bytes=VMEM_LIMIT),
        name="swiglu",
    )(h, w_gate, w_up, w_down, g, b)


def _layer(h, mem, w_in, b_forget, conv_w, conv_b, conv_ln_g, conv_ln_b, w_out, ln_mix_g, ln_mix_b,
           w_cq, w_ck, w_cv, w_co, ln_cross_g, ln_cross_b, w_gate, w_up, w_down, ln_ffn_g, ln_ffn_b):
    bsz, seq, _ = h.shape
    row = lambda v: v.reshape(1, -1).astype(F32)
    w_main = w_in[:, :C_F].astype(BF16)
    w_f = jnp.pad(w_in[:, C_F:], ((0, 0), (0, LANES - FOX_HEADS))).astype(BF16)
    b_f = jnp.pad(b_forget.astype(F32), (0, LANES - FOX_HEADS)).reshape(1, LANES)
    tri = jnp.tri(ROW_TILE, dtype=BF16)

    later = [w.astype(F32) for w in (w_ck, w_cv, w_out, w_cq, w_co, w_gate, w_up, w_down)]
    (u, q_aug, k_aug, v), later = _inproj(h, w_main, w_f, b_f, tri, later)
    w_ck, w_cv, w_out, w_cq, w_co, w_gate, w_up, w_down = later
    o = _fox_attention(q_aug, k_aug, v)
    kc, vc = _mem_kv(mem, w_ck, w_cv)
    h2 = _mix_cross(h, u, o, conv_w.astype(F32), row(conv_b), row(conv_ln_g), row(conv_ln_b),
                    w_out, row(ln_mix_g), row(ln_mix_b), w_cq, kc, vc, w_co, row(ln_cross_g), row(ln_cross_b))
    h3 = _swiglu(h2.reshape(bsz * seq, D_MODEL), w_gate, w_up, w_down, row(ln_ffn_g), row(ln_ffn_b))
    return h3.reshape(bsz, seq, D_MODEL)


def kernel(x, mem, w_in, b_forget, conv_w, conv_b, conv_ln_g, conv_ln_b, w_out, ln_mix_g, ln_mix_b,
           w_cq, w_ck, w_cv, w_co, ln_cross_g, ln_cross_b, w_gate, w_up, w_down, ln_ffn_g, ln_ffn_b):
    depth = w_in.shape[0]
    assert depth == 1, "DEEPNORM_ALPHA is fixed for a single layer"
    h = x
    for l in range(depth):
        h = _layer(h, mem, w_in[l], b_forget[l], conv_w[l], conv_b[l], conv_ln_g[l], conv_ln_b[l],
                   w_out[l], ln_mix_g[l], ln_mix_b[l], w_cq[l], w_ck[l], w_cv[l], w_co[l],
                   ln_cross_g[l], ln_cross_b[l], w_gate[l], w_up[l], w_down[l], ln_ffn_g[l], ln_ffn_b[l])
    return h
```

```python
import functools
import math

import jax
import jax.numpy as jnp
from jax import lax
from jax.experimental import pallas as pl
from jax.experimental.pallas import tpu as pltpu

D_MODEL = 1024
CONV_WIDTH = 512
CONV_K = 31
FOX_WIDTH = 512
FOX_HEADS = 8
FOX_HEAD_DIM = 64
MEM_LEN = 256
MEM_HEADS = 4
MEM_HEAD_DIM = 256
D_FF = 2816
LN_EPS = 1e-5
NEG_INF = -1e30
DEEPNORM_ALPHA = 2.0 ** 0.25
LOG2E = math.log2(math.e)

LANES = 128
SUBLANES = 8
HALO = 32
ROW_TILE = 512
FF_ROW_TILE = 1024
Q_TILE = 2048
K_TILE = 512
DIAG = Q_TILE // K_TILE
Q_SUB = 256
EXP_ROWS = 64
VT_ROWS = FOX_HEAD_DIM + 16
FF_CHUNK = 256
CONV_ROWS = 32
VMEM_LIMIT = 56 * 1024 * 1024

BF16 = jnp.bfloat16
F32 = jnp.float32

C_GLU_A = 0
C_GLU_B = CONV_WIDTH
C_Q = 2 * CONV_WIDTH
C_K = C_Q + FOX_WIDTH
C_V = C_K + FOX_WIDTH
C_F = C_V + FOX_WIDTH


def _mm(a, b):
    return jnp.dot(a, b, preferred_element_type=F32)


def _mm_nt(a, b):
    return lax.dot_general(a, b, (((1,), (1,)), ((), ())), preferred_element_type=F32)


def _layer_norm(x, g, b):
    mu = jnp.mean(x, axis=-1, keepdims=True)
    xc = x - mu
    var = jnp.mean(xc * xc, axis=-1, keepdims=True)
    return xc * lax.rsqrt(var + LN_EPS) * g + b


def _sigmoid(x):
    return 1.0 / (1.0 + jnp.exp(-x))


def _split3(x):
    hi = x.astype(BF16).astype(F32)
    r = x - hi
    mid = r.astype(BF16).astype(F32)
    lo = (r - mid).astype(BF16).astype(F32)
    return hi, mid, lo


def _pack3(hi, mid, lo):
    return hi + pltpu.roll(mid, 8, axis=1) + pltpu.roll(lo, 16, axis=1)


def _inproj_kernel(x_ref, w_ref, wf_ref, bf_ref, tri_ref, *refs, n_cast):
    cast_in, (u_ref, q_ref, k_ref, v_ref), cast_out = refs[:n_cast], refs[n_cast:n_cast + 4], refs[n_cast + 4:-1]
    carry = refs[-1]
    t = pl.program_id(1)
    rows = x_ref.shape[1]
    xb = x_ref[0].astype(BF16)

    for src_ref, dst_ref in zip(cast_in, cast_out):
        dst_ref[...] = src_ref[...].astype(BF16)

    @pl.when(t == 0)
    def _():
        carry[...] = jnp.zeros_like(carry)

    glu_a = _mm(xb, w_ref[:, C_GLU_A:C_GLU_A + CONV_WIDTH])
    glu_b = _mm(xb, w_ref[:, C_GLU_B:C_GLU_B + CONV_WIDTH])
    u_ref[0] = glu_a * _sigmoid(glu_b)

    lane = lax.broadcasted_iota(jnp.int32, (rows, LANES), 1)
    f = _mm(xb, wf_ref[...]) + bf_ref[...]
    logf = jnp.minimum(f, 0.0) - jnp.log(1.0 + jnp.exp(-jnp.abs(f)))
    logf = jnp.where(lane < FOX_HEADS, logf, 0.0)
    packed = _pack3(*_split3(logf)).astype(BF16)
    res = _mm(tri_ref[...], packed)
    cum = res + pltpu.roll(res, LANES - 8, axis=1) + pltpu.roll(res, LANES - 16, axis=1)
    cum = jnp.where(lane < FOX_HEADS, cum + carry[0:1, :], 0.0)
    carry[0:1, :] = cum[rows - 1:rows, :]
    cpack = _pack3(*_split3(cum * LOG2E))

    scale = LOG2E / math.sqrt(FOX_HEAD_DIM)
    for hp in range(FOX_HEADS // 2):
        if hp % 2 == 0:
            qquad = _mm(xb, w_ref[:, C_Q + hp * LANES:C_Q + (hp + 2) * LANES]) * scale
            kquad = _mm(xb, w_ref[:, C_K + hp * LANES:C_K + (hp + 2) * LANES])
        qpair = qquad[:, (hp % 2) * LANES:(hp % 2 + 1) * LANES]
        kpair = kquad[:, (hp % 2) * LANES:(hp % 2 + 1) * LANES]
        for sub in range(2):
            h = 2 * hp + sub
            aug0 = FOX_HEAD_DIM if sub == 0 else 0
            data = (lane < FOX_HEAD_DIM) if sub == 0 else (lane >= FOX_HEAD_DIM)
            slot_a = (lane == aug0) | (lane == aug0 + 8) | (lane == aug0 + 16)
            slot_b = (lane == aug0 + 1) | (lane == aug0 + 9) | (lane == aug0 + 17)
            ra = pltpu.roll(cpack, (aug0 - h) % LANES, axis=1)
            rb = pltpu.roll(cpack, (aug0 + 1 - h) % LANES, axis=1)
            qa = jnp.where(data, qpair, jnp.where(slot_a, ra, jnp.where(slot_b, 1.0, 0.0)))
            ka = jnp.where(data, kpair, jnp.where(slot_a, 1.0, jnp.where(slot_b, -rb, 0.0)))
            q_ref[0, h] = qa.astype(BF16)
            k_ref[0, h] = ka.astype(BF16)
    v_ref[0] = _mm(xb, w_ref[:, C_V:C_V + FOX_WIDTH]).astype(BF16)


def _slab_spec(rows, cols, steps, steps_per_seq):
    nslabs = steps
    while rows % nslabs or (rows // nslabs) % (2 * SUBLANES):
        nslabs //= 2
    hold = steps // nslabs
    return pl.BlockSpec((rows // nslabs, cols), lambda b, t: ((b * steps_per_seq + t) // hold, 0))


def _inproj(x, w_main, w_f, b_f, tri, to_cast):
    bsz, seq, _ = x.shape
    nt = seq // ROW_TILE
    const2 = lambda b, t: (0, 0)
    slabs = [_slab_spec(w.shape[0], w.shape[1], bsz * nt, nt) for w in to_cast]
    outs = pl.pallas_call(
        functools.partial(_inproj_kernel, n_cast=len(to_cast)),
        grid=(bsz, nt),
        in_specs=[
            pl.BlockSpec((1, ROW_TILE, D_MODEL), lambda b, t: (b, t, 0)),
            pl.BlockSpec(w_main.shape, const2),
            pl.BlockSpec(w_f.shape, const2),
            pl.BlockSpec(b_f.shape, const2),
            pl.BlockSpec(tri.shape, const2),
        ] + slabs,
        out_specs=[
            pl.BlockSpec((1, ROW_TILE, CONV_WIDTH), lambda b, t: (b, t, 0)),
            pl.BlockSpec((1, FOX_HEADS, ROW_TILE, LANES), lambda b, t: (b, 0, t, 0)),
            pl.BlockSpec((1, FOX_HEADS, ROW_TILE, LANES), lambda b, t: (b, 0, t, 0)),
            pl.BlockSpec((1, ROW_TILE, FOX_WIDTH), lambda b, t: (b, t, 0)),
        ] + slabs,
        out_shape=[
            jax.ShapeDtypeStruct((bsz, seq, CONV_WIDTH), F32),
            jax.ShapeDtypeStruct((bsz, FOX_HEADS, seq, LANES), BF16),
            jax.ShapeDtypeStruct((bsz, FOX_HEADS, seq, LANES), BF16),
            jax.ShapeDtypeStruct((bsz, seq, FOX_WIDTH), BF16),
        ] + [jax.ShapeDtypeStruct(w.shape, BF16) for w in to_cast],
        scratch_shapes=[pltpu.VMEM((SUBLANES, LANES), F32)],
        compiler_params=pltpu.CompilerParams(
            dimension_semantics=("arbitrary", "arbitrary"), vmem_limit_bytes=VMEM_LIMIT),
        name="inproj",
    )(x, w_main, w_f, b_f, tri, *to_cast)
    return outs[:4], outs[4:]


def _fox_kernel(q_ref, qn_ref, k_ref, v_ref, o_ref, vt_scr, qt_scr, qtn_scr, acc_scr, s_scr, m_scr):
    i = pl.program_id(1)
    nq = pl.num_programs(1)
    nk = vt_scr.shape[0]
    npairs = FOX_HEADS // 2
    units = [(h, qb) for h in range(2) for qb in range(Q_TILE // Q_SUB)]
    first_diag = DIAG * i

    def band_keys(qb, band):
        if band is None:
            return K_TILE
        return max(0, min(K_TILE, (qb + 1) * Q_SUB - band * K_TILE))

    def transpose_q(src_ref, head0, dst):
        for h in range(2):
            dst[h] = src_ref[0, head0 + h].astype(F32).T.astype(BF16)

    def score_unit(pair, j, u, band, fresh, qt):
        h, qb = units[u]
        nkeys = band_keys(qb, band)
        if nkeys == 0:
            return
        kt = k_ref[0, 2 * pair + h, pl.ds(pl.multiple_of(j * K_TILE, K_TILE), nkeys), :]
        st = _mm(kt, qt[h, :, qb * Q_SUB:(qb + 1) * Q_SUB])
        if band is not None and band * K_TILE + nkeys - 1 > qb * Q_SUB:
            kpos = lax.broadcasted_iota(jnp.int32, st.shape, 0) + band * K_TILE
            qpos = lax.broadcasted_iota(jnp.int32, st.shape, 1) + qb * Q_SUB
            st = jnp.where(kpos <= qpos, st, NEG_INF)
        s_scr[u, 0:nkeys, :] = st
        m_prev = jnp.full((1, Q_SUB), NEG_INF, F32) if fresh else m_scr[u, 1]
        m_scr[u, 0] = m_prev
        m_scr[u, 1] = jnp.maximum(m_prev, jnp.max(st, axis=0, keepdims=True))

    def value_unit(pair, j, u, band):
        h, qb = units[u]
        nkeys = band_keys(qb, band)
        if nkeys == 0:
            return
        m_new = m_scr[u, 1]
        alpha = jnp.exp2(m_scr[u, 0] - m_new)
        pt = jnp.exp2((s_scr[u, 0:nkeys, :] - m_new).astype(BF16))
        acc_scr[u] = alpha * acc_scr[u] + _mm(vt_scr[j, 2 * pair + h, :, 0:nkeys], pt)

    def stages(value=None, score=None, value_band=None, score_band=None, fresh=False, qt=None):
        for u in range(len(units)):
            if value is not None:
                value_unit(value[0], value[1], u, value_band)
            if score is not None:
                score_unit(score[0], score[1], u, score_band, fresh, qt_scr.at[score[0]] if qt is None else qt)

    @pl.when(i == 0)
    def _():
        tail = (lax.broadcasted_iota(jnp.int32, (VT_ROWS - FOX_HEAD_DIM, K_TILE), 0) == 0).astype(BF16)
        for c in range(nk):
            for pair in range(npairs):
                blk = v_ref[0, c * K_TILE:(c + 1) * K_TILE, pair * LANES:(pair + 1) * LANES]
                blk = blk.astype(F32).T.astype(BF16)
                for h in range(2):
                    vt_scr[c, 2 * pair + h, 0:FOX_HEAD_DIM, :] = blk[h * FOX_HEAD_DIM:(h + 1) * FOX_HEAD_DIM, :]
                    vt_scr[c, 2 * pair + h, FOX_HEAD_DIM:VT_ROWS, :] = tail

    for pair in range(npairs):
        transpose_q(q_ref, 2 * pair, qt_scr.at[pair])

    @pl.when(i == 0)
    def _():
        stages(score=(0, 0), score_band=0, fresh=True)

    for pair in range(npairs):
        acc_scr[...] = jnp.zeros(acc_scr.shape, F32)

        def trip(t, c, pair=pair):
            stages(value=(pair, t), score=(pair, t + 1))
            return c

        lax.fori_loop(0, first_diag - 1, trip, 0)

        @pl.when(i > 0)
        def _(pair=pair):
            stages(value=(pair, first_diag - 1), score=(pair, first_diag), score_band=0)

        for band in range(DIAG - 1):
            stages(value=(pair, first_diag + band), score=(pair, first_diag + band + 1),
                   value_band=band, score_band=band + 1)

        last = (pair, first_diag + DIAG - 1)
        if pair + 1 < npairs:
            @pl.when(i == 0)
            def _(pair=pair, last=last):
                stages(value=last, score=(pair + 1, 0), value_band=DIAG - 1, score_band=0, fresh=True)

            @pl.when(i > 0)
            def _(pair=pair, last=last):
                stages(value=last, score=(pair + 1, 0), value_band=DIAG - 1, fresh=True)
        else:
            @pl.when(i < nq - 1)
            def _(last=last):
                transpose_q(qn_ref, 0, qtn_scr)
                stages(value=last, score=(0, 0), value_band=DIAG - 1, fresh=True, qt=qtn_scr)

            @pl.when(i == nq - 1)
            def _(last=last):
                stages(value=last, value_band=DIAG - 1)

        rows = []
        for h in range(2):
            blocks = [acc_scr[u] for u in range(len(units)) if units[u][0] == h]
            rows.append(jnp.concatenate(
                [a[0:FOX_HEAD_DIM] / a[FOX_HEAD_DIM:FOX_HEAD_DIM + 1] for a in blocks], axis=1))
        pair_out = jnp.concatenate(rows, axis=0).T.astype(BF16)
        o_ref[0, :, pair * LANES:(pair + 1) * LANES] = pair_out


def _fox_attention(q_aug, k_aug, v):
    bsz, _, seq, _ = q_aug.shape
    nq = seq // Q_TILE
    nk = seq // K_TILE
    return pl.pallas_call(
        _fox_kernel,
        grid=(bsz, nq),
        in_specs=[
            pl.BlockSpec((1, FOX_HEADS, Q_TILE, LANES), lambda b, i: (b, 0, i, 0)),
            pl.BlockSpec((1, 2, Q_TILE, LANES), lambda b, i: (b, 0, jnp.minimum(i + 1, nq - 1), 0)),
            pl.BlockSpec((1, FOX_HEADS, seq, LANES), lambda b, i: (b, 0, 0, 0), pipeline_mode=pl.Buffered(1)),
            pl.BlockSpec((1, seq, FOX_WIDTH), lambda b, i: (b, 0, 0), pipeline_mode=pl.Buffered(1)),
        ],
        out_specs=pl.BlockSpec((1, Q_TILE, FOX_WIDTH), lambda b, i: (b, i, 0)),
        out_shape=jax.ShapeDtypeStruct((bsz, seq, FOX_WIDTH), BF16),
        scratch_shapes=[
            pltpu.VMEM((nk, FOX_HEADS, VT_ROWS, K_TILE), BF16),
            pltpu.VMEM((FOX_HEADS // 2, 2, LANES, Q_TILE), BF16),
            pltpu.VMEM((2, LANES, Q_TILE), BF16),
            pltpu.VMEM((2 * (Q_TILE // Q_SUB), VT_ROWS, Q_SUB), F32),
            pltpu.VMEM((2 * (Q_TILE // Q_SUB), K_TILE, Q_SUB), F32),
            pltpu.VMEM((2 * (Q_TILE // Q_SUB), 2, 1, Q_SUB), F32),
        ],
        compiler_params=pltpu.CompilerParams(
            dimension_semantics=("arbitrary", "arbitrary"), vmem_limit_bytes=VMEM_LIMIT),
        name="fox_attention",
    )(q_aug, q_aug, k_aug, v)


def _memkv_kernel(mem_ref, wk_ref, wv_ref, k_ref, v_ref):
    mb = mem_ref[0].astype(BF16)
    k_ref[0] = _mm(mb, wk_ref[...]).astype(BF16)
    v_ref[0] = _mm(mb, wv_ref[...]).astype(BF16)


def _mem_kv(mem, w_ck, w_cv):
    bsz = mem.shape[0]
    const2 = lambda b: (0, 0)
    blk = pl.BlockSpec((1, MEM_LEN, D_MODEL), lambda b: (b, 0, 0))
    return pl.pallas_call(
        _memkv_kernel,
        grid=(bsz,),
        in_specs=[blk, pl.BlockSpec(w_ck.shape, const2), pl.BlockSpec(w_cv.shape, const2)],
        out_specs=[blk, blk],
        out_shape=[jax.ShapeDtypeStruct((bsz, MEM_LEN, D_MODEL), BF16)] * 2,
        compiler_params=pltpu.CompilerParams(
            dimension_semantics=("arbitrary",), vmem_limit_bytes=VMEM_LIMIT),
        name="mem_kv",
    )(mem, w_ck, w_cv)


def _conv_branch(u_ref, cw_ref, cb_ref, cg_ref, cbeta_ref, ubuf, shifted):
    t = pl.program_id(1)
    rows = u_ref.shape[1]
    first = HALO - (CONV_K - 1)
    span = shifted.shape[1]

    @pl.when(t == 0)
    def _():
        ubuf[0:HALO, :] = jnp.zeros((HALO, CONV_WIDTH), F32)

    @pl.when(t > 0)
    def _():
        ubuf[0:HALO, :] = ubuf[rows:rows + HALO, :]

    ubuf[HALO:HALO + rows, :] = u_ref[0]
    for p in range(1, SUBLANES):
        shifted[p - 1] = ubuf[p:p + span, :]

    cw = cw_ref[...]
    cbias = cb_ref[...]
    gam = cg_ref[...]
    beta = cbeta_ref[...]
    outs = []
    for r in range(rows // CONV_ROWS):
        base = r * CONV_ROWS
        acc = jnp.broadcast_to(cbias, (CONV_ROWS, CONV_WIDTH))
        for j in range(CONV_K):
            a, p = divmod(first + j, SUBLANES)
            lo = base + SUBLANES * a
            src = ubuf[lo:lo + CONV_ROWS, :] if p == 0 else shifted[p - 1, lo:lo + CONV_ROWS, :]
            acc = acc + cw[j:j + 1, :] * src
        y = _layer_norm(acc, gam, beta)
        outs.append((y * _sigmoid(y)).astype(BF16))
    return jnp.concatenate(outs, axis=0)


def _mix_cross_kernel(x_ref, u_ref, o_ref, cw_ref, cb_ref, cg_ref, cbeta_ref, wout_ref, g1_ref, b1_ref,
                      wcq_ref, kc_ref, vc_ref, wco_ref, g2_ref, b2_ref, h_ref, ubuf, shifted):
    uc = _conv_branch(u_ref, cw_ref, cb_ref, cg_ref, cbeta_ref, ubuf, shifted)
    mix = _mm(uc, wout_ref[0:CONV_WIDTH, :]) + _mm(o_ref[0], wout_ref[CONV_WIDTH:, :])
    h1 = _layer_norm(DEEPNORM_ALPHA * x_ref[0] + mix, g1_ref[...], b1_ref[...])

    q = (_mm(h1.astype(BF16), wcq_ref[...]) * (1.0 / math.sqrt(MEM_HEAD_DIM))).astype(BF16)
    outs = []
    for hh in range(MEM_HEADS):
        cols = slice(hh * MEM_HEAD_DIM, (hh + 1) * MEM_HEAD_DIM)
        s = _mm_nt(q[:, cols], kc_ref[0, :, cols])
        p = jnp.exp(s - jnp.max(s, axis=-1, keepdims=True))
        l = jnp.sum(p, axis=-1, keepdims=True)
        outs.append((_mm(p.astype(BF16), vc_ref[0, :, cols]) / l).astype(BF16))
    o = jnp.concatenate(outs, axis=-1)
    y = _mm(o, wco_ref[...])
    h_ref[0] = _layer_norm(DEEPNORM_ALPHA * h1 + y, g2_ref[...], b2_ref[...])


def _mix_cross(x, u, o, conv_w, conv_b, conv_g, conv_beta, w_out, g1, b1, w_cq, kc, vc, w_co, g2, b2):
    bsz, seq, _ = x.shape
    nt = seq // ROW_TILE
    const2 = lambda b, t: (0, 0)
    tok = lambda width: pl.BlockSpec((1, ROW_TILE, width), lambda b, t: (b, t, 0))
    memblk = pl.BlockSpec((1, MEM_LEN, D_MODEL), lambda b, t: (b, 0, 0))
    vec = pl.BlockSpec((1, D_MODEL), const2)
    cvec = pl.BlockSpec((1, CONV_WIDTH), const2)
    mat = pl.BlockSpec((D_MODEL, D_MODEL), const2)
    return pl.pallas_call(
        _mix_cross_kernel,
        grid=(bsz, nt),
        in_specs=[tok(D_MODEL), tok(CONV_WIDTH), tok(FOX_WIDTH), pl.BlockSpec(conv_w.shape, const2), cvec, cvec,
                  cvec, mat, vec, vec, mat, memblk, memblk, mat, vec, vec],
        out_specs=tok(D_MODEL),
        out_shape=jax.ShapeDtypeStruct((bsz, seq, D_MODEL), F32),
        scratch_shapes=[
            pltpu.VMEM((ROW_TILE + HALO, CONV_WIDTH), F32),
            pltpu.VMEM((SUBLANES - 1, ROW_TILE + HALO - SUBLANES, CONV_WIDTH), F32),
        ],
        compiler_params=pltpu.CompilerParams(
            dimension_semantics=("arbitrary", "arbitrary"), vmem_limit_bytes=VMEM_LIMIT),
        name="mix_cross",
    )(x, u, o, conv_w, conv_b, conv_g, conv_beta, w_out, g1, b1, w_cq, kc, vc, w_co, g2, b2)


def _swiglu_kernel(h_ref, wg_ref, wu_ref, wd_ref, g_ref, b_ref, out_ref):
    h = h_ref[...]
    hb = h.astype(BF16)
    acc = DEEPNORM_ALPHA * h
    for c in range(D_FF // FF_CHUNK):
        cols = slice(c * FF_CHUNK, (c + 1) * FF_CHUNK)
        gate = _mm(hb, wg_ref[:, cols])
        up = _mm(hb, wu_ref[:, cols])
        act = (gate * _sigmoid(gate) * up).astype(BF16)
        acc = acc + _mm(act, wd_ref[cols, :])
    out_ref[...] = _layer_norm(acc, g_ref[...], b_ref[...])


def _swiglu(h, w_gate, w_up, w_down, g, b):
    n = h.shape[0]
    const2 = lambda t: (0, 0)
    tok = pl.BlockSpec((FF_ROW_TILE, D_MODEL), lambda t: (t, 0))
    return pl.pallas_call(
        _swiglu_kernel,
        grid=(n // FF_ROW_TILE,),
        in_specs=[tok, pl.BlockSpec(w_gate.shape, const2), pl.BlockSpec(w_up.shape, const2),
                  pl.BlockSpec(w_down.shape, const2), pl.BlockSpec((1, D_MODEL), const2),
                  pl.BlockSpec((1, D_MODEL), const2)],
        out_specs=tok,
        out_shape=jax.ShapeDtypeStruct((n, D_MODEL), F32),
        compiler_params=pltpu.CompilerParams(
            dimension_semantics=("arbitrary",), vmem_limit_bytes=VMEM_LIMIT),
        name="swiglu",
    )(h, w_gate, w_up, w_down, g, b)


def _layer(h, mem, w_in, b_forget, conv_w, conv_b, conv_ln_g, conv_ln_b, w_out, ln_mix_g, ln_mix_b,
           w_cq, w_ck, w_cv, w_co, ln_cross_g, ln_cross_b, w_gate, w_up, w_down, ln_ffn_g, ln_ffn_b):
    bsz, seq, _ = h.shape
    row = lambda v: v.reshape(1, -1).astype(F32)
    w_main = w_in[:, :C_F].astype(BF16)
    w_f = jnp.pad(w_in[:, C_F:], ((0, 0), (0, LANES - FOX_HEADS))).astype(BF16)
    b_f = jnp.pad(b_forget.astype(F32), (0, LANES - FOX_HEADS)).reshape(1, LANES)
    tri = jnp.tri(ROW_TILE, dtype=BF16)

    later = [w.astype(F32) for w in (w_ck, w_cv, w_out, w_cq, w_co, w_gate, w_up, w_down)]
    (u, q_aug, k_aug, v), later = _inproj(h, w_main, w_f, b_f, tri, later)
    w_ck, w_cv, w_out, w_cq, w_co, w_gate, w_up, w_down = later
    o = _fox_attention(q_aug, k_aug, v)
    kc, vc = _mem_kv(mem, w_ck, w_cv)
    h2 = _mix_cross(h, u, o, conv_w.astype(F32), row(conv_b), row(conv_ln_g), row(conv_ln_b),
                    w_out, row(ln_mix_g), row(ln_mix_b), w_cq, kc, vc, w_co, row(ln_cross_g), row(ln_cross_b))
    h3 = _swiglu(h2.reshape(bsz * seq, D_MODEL), w_gate, w_up, w_down, row(ln_ffn_g), row(ln_ffn_b))
    return h3.reshape(bsz, seq, D_MODEL)


def kernel(x, mem, w_in, b_forget, conv_w, conv_b, conv_ln_g, conv_ln_b, w_out, ln_mix_g, ln_mix_b,
           w_cq, w_ck, w_cv, w_co, ln_cross_g, ln_cross_b, w_gate, w_up, w_down, ln_ffn_g, ln_ffn_b):
    depth = w_in.shape[0]
    assert depth == 1, "DEEPNORM_ALPHA is fixed for a single layer"
    h = x
    for l in range(depth):
        h = _layer(h, mem, w_in[l], b_forget[l], conv_w[l], conv_b[l], conv_ln_g[l], conv_ln_b[l],
                   w_out[l], ln_mix_g[l], ln_mix_b[l], w_cq[l], w_ck[l], w_cv[l], w_co[l],
                   ln_cross_g[l], ln_cross_b[l], w_gate[l], w_up[l], w_down[l], ln_ffn_g[l], ln_ffn_b[l])
    return h
```

```python
import functools
import math

import jax
import jax.numpy as jnp
from jax import lax
from jax.experimental import pallas as pl
from jax.experimental.pallas import tpu as pltpu

D_MODEL = 1024
CONV_WIDTH = 512
CONV_K = 31
FOX_WIDTH = 512
FOX_HEADS = 8
FOX_HEAD_DIM = 64
MEM_LEN = 256
MEM_HEADS = 4
MEM_HEAD_DIM = 256
D_FF = 2816
LN_EPS = 1e-5
NEG_INF = -1e30
DEEPNORM_ALPHA = 2.0 ** 0.25
LOG2E = math.log2(math.e)

LANES = 128
SUBLANES = 8
HALO = 32
ROW_TILE = 512
FF_ROW_TILE = 1024
Q_TILE = 1024
K_TILE = 1024
DIAG = Q_TILE // K_TILE
Q_SUB = 256
EXP_ROWS = 64
VT_ROWS = FOX_HEAD_DIM + 16
FF_CHUNK = 256
CONV_ROWS = 32
VMEM_LIMIT = 56 * 1024 * 1024

BF16 = jnp.bfloat16
F32 = jnp.float32

C_GLU_A = 0
C_GLU_B = CONV_WIDTH
C_Q = 2 * CONV_WIDTH
C_K = C_Q + FOX_WIDTH
C_V = C_K + FOX_WIDTH
C_F = C_V + FOX_WIDTH


def _mm(a, b):
    return jnp.dot(a, b, preferred_element_type=F32)


def _mm_nt(a, b):
    return lax.dot_general(a, b, (((1,), (1,)), ((), ())), preferred_element_type=F32)


def _layer_norm(x, g, b):
    mu = jnp.mean(x, axis=-1, keepdims=True)
    xc = x - mu
    var = jnp.mean(xc * xc, axis=-1, keepdims=True)
    return xc * lax.rsqrt(var + LN_EPS) * g + b


def _sigmoid(x):
    return 1.0 / (1.0 + jnp.exp(-x))


def _split3(x):
    hi = x.astype(BF16).astype(F32)
    r = x - hi
    mid = r.astype(BF16).astype(F32)
    lo = (r - mid).astype(BF16).astype(F32)
    return hi, mid, lo


def _pack3(hi, mid, lo):
    return hi + pltpu.roll(mid, 8, axis=1) + pltpu.roll(lo, 16, axis=1)


def _inproj_kernel(x_ref, w_ref, wf_ref, bf_ref, tri_ref, *refs, n_cast):
    cast_in, (u_ref, q_ref, k_ref, v_ref), cast_out = refs[:n_cast], refs[n_cast:n_cast + 4], refs[n_cast + 4:-1]
    carry = refs[-1]
    t = pl.program_id(1)
    rows = x_ref.shape[1]
    xb = x_ref[0].astype(BF16)

    for src_ref, dst_ref in zip(cast_in, cast_out):
        dst_ref[...] = src_ref[...].astype(BF16)

    @pl.when(t == 0)
    def _():
        carry[...] = jnp.zeros_like(carry)

    glu_a = _mm(xb, w_ref[:, C_GLU_A:C_GLU_A + CONV_WIDTH])
    glu_b = _mm(xb, w_ref[:, C_GLU_B:C_GLU_B + CONV_WIDTH])
    u_ref[0] = glu_a * _sigmoid(glu_b)

    lane = lax.broadcasted_iota(jnp.int32, (rows, LANES), 1)
    f = _mm(xb, wf_ref[...]) + bf_ref[...]
    logf = jnp.minimum(f, 0.0) - jnp.log(1.0 + jnp.exp(-jnp.abs(f)))
    logf = jnp.where(lane < FOX_HEADS, logf, 0.0)
    packed = _pack3(*_split3(logf)).astype(BF16)
    res = _mm(tri_ref[...], packed)
    cum = res + pltpu.roll(res, LANES - 8, axis=1) + pltpu.roll(res, LANES - 16, axis=1)
    cum = jnp.where(lane < FOX_HEADS, cum + carry[0:1, :], 0.0)
    carry[0:1, :] = cum[rows - 1:rows, :]
    cpack = _pack3(*_split3(cum * LOG2E))

    scale = LOG2E / math.sqrt(FOX_HEAD_DIM)
    for hp in range(FOX_HEADS // 2):
        if hp % 2 == 0:
            qquad = _mm(xb, w_ref[:, C_Q + hp * LANES:C_Q + (hp + 2) * LANES]) * scale
            kquad = _mm(xb, w_ref[:, C_K + hp * LANES:C_K + (hp + 2) * LANES])
        qpair = qquad[:, (hp % 2) * LANES:(hp % 2 + 1) * LANES]
        kpair = kquad[:, (hp % 2) * LANES:(hp % 2 + 1) * LANES]
        for sub in range(2):
            h = 2 * hp + sub
            aug0 = FOX_HEAD_DIM if sub == 0 else 0
            data = (lane < FOX_HEAD_DIM) if sub == 0 else (lane >= FOX_HEAD_DIM)
            slot_a = (lane == aug0) | (lane == aug0 + 8) | (lane == aug0 + 16)
            slot_b = (lane == aug0 + 1) | (lane == aug0 + 9) | (lane == aug0 + 17)
            ra = pltpu.roll(cpack, (aug0 - h) % LANES, axis=1)
            rb = pltpu.roll(cpack, (aug0 + 1 - h) % LANES, axis=1)
            qa = jnp.where(data, qpair, jnp.where(slot_a, ra, jnp.where(slot_b, 1.0, 0.0)))
            ka = jnp.where(data, kpair, jnp.where(slot_a, 1.0, jnp.where(slot_b, -rb, 0.0)))
            q_ref[0, h] = qa.astype(BF16)
            k_ref[0, h] = ka.astype(BF16)
    v_ref[0] = _mm(xb, w_ref[:, C_V:C_V + FOX_WIDTH]).astype(BF16)


def _slab_spec(rows, cols, steps, steps_per_seq):
    nslabs = steps
    while rows % nslabs or (rows // nslabs) % (2 * SUBLANES):
        nslabs //= 2
    hold = steps // nslabs
    return pl.BlockSpec((rows // nslabs, cols), lambda b, t: ((b * steps_per_seq + t) // hold, 0))


def _inproj(x, w_main, w_f, b_f, tri, to_cast):
    bsz, seq, _ = x.shape
    nt = seq // ROW_TILE
    const2 = lambda b, t: (0, 0)
    slabs = [_slab_spec(w.shape[0], w.shape[1], bsz * nt, nt) for w in to_cast]
    outs = pl.pallas_call(
        functools.partial(_inproj_kernel, n_cast=len(to_cast)),
        grid=(bsz, nt),
        in_specs=[
            pl.BlockSpec((1, ROW_TILE, D_MODEL), lambda b, t: (b, t, 0)),
            pl.BlockSpec(w_main.shape, const2),
            pl.BlockSpec(w_f.shape, const2),
            pl.BlockSpec(b_f.shape, const2),
            pl.BlockSpec(tri.shape, const2),
        ] + slabs,
        out_specs=[
            pl.BlockSpec((1, ROW_TILE, CONV_WIDTH), lambda b, t: (b, t, 0)),
            pl.BlockSpec((1, FOX_HEADS, ROW_TILE, LANES), lambda b, t: (b, 0, t, 0)),
            pl.BlockSpec((1, FOX_HEADS, ROW_TILE, LANES), lambda b, t: (b, 0, t, 0)),
            pl.BlockSpec((1, ROW_TILE, FOX_WIDTH), lambda b, t: (b, t, 0)),
        ] + slabs,
        out_shape=[
            jax.ShapeDtypeStruct((bsz, seq, CONV_WIDTH), F32),
            jax.ShapeDtypeStruct((bsz, FOX_HEADS, seq, LANES), BF16),
            jax.ShapeDtypeStruct((bsz, FOX_HEADS, seq, LANES), BF16),
            jax.ShapeDtypeStruct((bsz, seq, FOX_WIDTH), BF16),
        ] + [jax.ShapeDtypeStruct(w.shape, BF16) for w in to_cast],
        scratch_shapes=[pltpu.VMEM((SUBLANES, LANES), F32)],
        compiler_params=pltpu.CompilerParams(
            dimension_semantics=("arbitrary", "arbitrary"), vmem_limit_bytes=VMEM_LIMIT),
        name="inproj",
    )(x, w_main, w_f, b_f, tri, *to_cast)
    return outs[:4], outs[4:]


def _fox_kernel(q_ref, qn_ref, k_ref, v_ref, o_ref, vt_scr, qt_scr, qtn_scr, acc_scr, s_scr, m_scr):
    i = pl.program_id(1)
    nq = pl.num_programs(1)
    nk = vt_scr.shape[0]
    npairs = FOX_HEADS // 2
    units = [(h, qb) for h in range(2) for qb in range(Q_TILE // Q_SUB)]
    first_diag = DIAG * i

    def band_keys(qb, band):
        if band is None:
            return K_TILE
        return max(0, min(K_TILE, (qb + 1) * Q_SUB - band * K_TILE))

    def transpose_q(src_ref, head0, dst):
        for h in range(2):
            dst[h] = src_ref[0, head0 + h].astype(F32).T.astype(BF16)

    def score_unit(pair, j, u, band, fresh, qt):
        h, qb = units[u]
        nkeys = band_keys(qb, band)
        if nkeys == 0:
            return
        kt = k_ref[0, 2 * pair + h, pl.ds(pl.multiple_of(j * K_TILE, K_TILE), nkeys), :]
        st = _mm(kt, qt[h, :, qb * Q_SUB:(qb + 1) * Q_SUB])
        if band is not None and band * K_TILE + nkeys - 1 > qb * Q_SUB:
            kpos = lax.broadcasted_iota(jnp.int32, st.shape, 0) + band * K_TILE
            qpos = lax.broadcasted_iota(jnp.int32, st.shape, 1) + qb * Q_SUB
            st = jnp.where(kpos <= qpos, st, NEG_INF)
        s_scr[u, 0:nkeys, :] = st
        m_prev = jnp.full((1, Q_SUB), NEG_INF, F32) if fresh else m_scr[u, 1]
        m_scr[u, 0] = m_prev
        m_scr[u, 1] = jnp.maximum(m_prev, jnp.max(st, axis=0, keepdims=True))

    def value_unit(pair, j, u, band):
        h, qb = units[u]
        nkeys = band_keys(qb, band)
        if nkeys == 0:
            return
        m_new = m_scr[u, 1]
        alpha = jnp.exp2(m_scr[u, 0] - m_new)
        pt = jnp.exp2((s_scr[u, 0:nkeys, :] - m_new).astype(BF16))
        acc_scr[u] = alpha * acc_scr[u] + _mm(vt_scr[j, 2 * pair + h, :, 0:nkeys], pt)

    def stages(value=None, score=None, value_band=None, score_band=None, fresh=False, qt=None):
        for u in range(len(units)):
            if value is not None:
                value_unit(value[0], value[1], u, value_band)
            if score is not None:
                score_unit(score[0], score[1], u, score_band, fresh, qt_scr.at[score[0]] if qt is None else qt)

    @pl.when(i == 0)
    def _():
        tail = (lax.broadcasted_iota(jnp.int32, (VT_ROWS - FOX_HEAD_DIM, K_TILE), 0) == 0).astype(BF16)
        for c in range(nk):
            for pair in range(npairs):
                blk = v_ref[0, c * K_TILE:(c + 1) * K_TILE, pair * LANES:(pair + 1) * LANES]
                blk = blk.astype(F32).T.astype(BF16)
                for h in range(2):
                    vt_scr[c, 2 * pair + h, 0:FOX_HEAD_DIM, :] = blk[h * FOX_HEAD_DIM:(h + 1) * FOX_HEAD_DIM, :]
                    vt_scr[c, 2 * pair + h, FOX_HEAD_DIM:VT_ROWS, :] = tail

    for pair in range(npairs):
        transpose_q(q_ref, 2 * pair, qt_scr.at[pair])

    @pl.when(i == 0)
    def _():
        stages(score=(0, 0), score_band=0, fresh=True)

    def one_pair(pair, carry):
        acc_scr[...] = jnp.zeros(acc_scr.shape, F32)

        def trip(t, c):
            stages(value=(pair, t), score=(pair, t + 1))
            return c

        lax.fori_loop(0, first_diag - 1, trip, 0)

        @pl.when(i > 0)
        def _():
            stages(value=(pair, first_diag - 1), score=(pair, first_diag), score_band=0)

        for band in range(DIAG - 1):
            stages(value=(pair, first_diag + band), score=(pair, first_diag + band + 1),
                   value_band=band, score_band=band + 1)

        last = (pair, first_diag + DIAG - 1)
        more_pairs = pair + 1 < npairs

        @pl.when(more_pairs & (i == 0))
        def _():
            stages(value=last, score=(pair + 1, 0), value_band=DIAG - 1, score_band=0, fresh=True)

        @pl.when(more_pairs & (i > 0))
        def _():
            stages(value=last, score=(pair + 1, 0), value_band=DIAG - 1, fresh=True)

        @pl.when(jnp.logical_not(more_pairs) & (i < nq - 1))
        def _():
            transpose_q(qn_ref, 0, qtn_scr)
            stages(value=last, score=(0, 0), value_band=DIAG - 1, fresh=True, qt=qtn_scr)

        @pl.when(jnp.logical_not(more_pairs) & (i == nq - 1))
        def _():
            stages(value=last, value_band=DIAG - 1)

        rows = []
        for h in range(2):
            blocks = [acc_scr[u] for u in range(len(units)) if units[u][0] == h]
            rows.append(jnp.concatenate(
                [a[0:FOX_HEAD_DIM] / a[FOX_HEAD_DIM:FOX_HEAD_DIM + 1] for a in blocks], axis=1))
        o_ref[0, pair] = jnp.concatenate(rows, axis=0).T.astype(BF16)
        return carry

    lax.fori_loop(0, npairs, one_pair, 0)


def _fox_attention(q_aug, k_aug, v):
    bsz, _, seq, _ = q_aug.shape
    nq = seq // Q_TILE
    nk = seq // K_TILE
    return pl.pallas_call(
        _fox_kernel,
        grid=(bsz, nq),
        in_specs=[
            pl.BlockSpec((1, FOX_HEADS, Q_TILE, LANES), lambda b, i: (b, 0, i, 0)),
            pl.BlockSpec((1, 2, Q_TILE, LANES), lambda b, i: (b, 0, jnp.minimum(i + 1, nq - 1), 0)),
            pl.BlockSpec((1, FOX_HEADS, seq, LANES), lambda b, i: (b, 0, 0, 0)),
            pl.BlockSpec((1, seq, FOX_WIDTH), lambda b, i: (b, 0, 0)),
        ],
        out_specs=pl.BlockSpec((1, FOX_HEADS // 2, Q_TILE, LANES), lambda b, i: (b, 0, i, 0)),
        out_shape=jax.ShapeDtypeStruct((bsz, FOX_HEADS // 2, seq, LANES), BF16),
        scratch_shapes=[
            pltpu.VMEM((nk, FOX_HEADS, VT_ROWS, K_TILE), BF16),
            pltpu.VMEM((FOX_HEADS // 2, 2, LANES, Q_TILE), BF16),
            pltpu.VMEM((2, LANES, Q_TILE), BF16),
            pltpu.VMEM((2 * (Q_TILE // Q_SUB), VT_ROWS, Q_SUB), F32),
            pltpu.VMEM((2 * (Q_TILE // Q_SUB), K_TILE, Q_SUB), F32),
            pltpu.VMEM((2 * (Q_TILE // Q_SUB), 2, 1, Q_SUB), F32),
        ],
        compiler_params=pltpu.CompilerParams(
            dimension_semantics=("arbitrary", "arbitrary"), vmem_limit_bytes=VMEM_LIMIT),
        name="fox_attention",
    )(q_aug, q_aug, k_aug, v)


def _memkv_kernel(mem_ref, wk_ref, wv_ref, k_ref, v_ref):
    mb = mem_ref[0].astype(BF16)
    k_ref[0] = _mm(mb, wk_ref[...]).astype(BF16)
    v_ref[0] = _mm(mb, wv_ref[...]).astype(BF16)


def _mem_kv(mem, w_ck, w_cv):
    bsz = mem.shape[0]
    const2 = lambda b: (0, 0)
    blk = pl.BlockSpec((1, MEM_LEN, D_MODEL), lambda b: (b, 0, 0))
    return pl.pallas_call(
        _memkv_kernel,
        grid=(bsz,),
        in_specs=[blk, pl.BlockSpec(w_ck.shape, const2), pl.BlockSpec(w_cv.shape, const2)],
        out_specs=[blk, blk],
        out_shape=[jax.ShapeDtypeStruct((bsz, MEM_LEN, D_MODEL), BF16)] * 2,
        compiler_params=pltpu.CompilerParams(
            dimension_semantics=("arbitrary",), vmem_limit_bytes=VMEM_LIMIT),
        name="mem_kv",
    )(mem, w_ck, w_cv)


def _conv_branch(u_ref, cw_ref, cb_ref, cg_ref, cbeta_ref, ubuf, shifted):
    t = pl.program_id(1)
    rows = u_ref.shape[1]
    first = HALO - (CONV_K - 1)
    span = shifted.shape[1]

    @pl.when(t == 0)
    def _():
        ubuf[0:HALO, :] = jnp.zeros((HALO, CONV_WIDTH), F32)

    @pl.when(t > 0)
    def _():
        ubuf[0:HALO, :] = ubuf[rows:rows + HALO, :]

    ubuf[HALO:HALO + rows, :] = u_ref[0]
    for p in range(1, SUBLANES):
        shifted[p - 1] = ubuf[p:p + span, :]

    cw = cw_ref[...]
    cbias = cb_ref[...]
    gam = cg_ref[...]
    beta = cbeta_ref[...]
    outs = []
    for r in range(rows // CONV_ROWS):
        base = r * CONV_ROWS
        acc = jnp.broadcast_to(cbias, (CONV_ROWS, CONV_WIDTH))
        for j in range(CONV_K):
            a, p = divmod(first + j, SUBLANES)
            lo = base + SUBLANES * a
            src = ubuf[lo:lo + CONV_ROWS, :] if p == 0 else shifted[p - 1, lo:lo + CONV_ROWS, :]
            acc = acc + cw[j:j + 1, :] * src
        y = _layer_norm(acc, gam, beta)
        outs.append((y * _sigmoid(y)).astype(BF16))
    return jnp.concatenate(outs, axis=0)


def _mix_cross_kernel(x_ref, u_ref, o_ref, cw_ref, cb_ref, cg_ref, cbeta_ref, wout_ref, g1_ref, b1_ref,
                      wcq_ref, kc_ref, vc_ref, wco_ref, g2_ref, b2_ref, h_ref, ubuf, shifted):
    uc = _conv_branch(u_ref, cw_ref, cb_ref, cg_ref, cbeta_ref, ubuf, shifted)
    o_fox = jnp.concatenate([o_ref[0, p] for p in range(o_ref.shape[1])], axis=1)
    mix = _mm(uc, wout_ref[0:CONV_WIDTH, :]) + _mm(o_fox, wout_ref[CONV_WIDTH:, :])
    h1 = _layer_norm(DEEPNORM_ALPHA * x_ref[0] + mix, g1_ref[...], b1_ref[...])

    q = (_mm(h1.astype(BF16), wcq_ref[...]) * (1.0 / math.sqrt(MEM_HEAD_DIM))).astype(BF16)
    outs = []
    for hh in range(MEM_HEADS):
        cols = slice(hh * MEM_HEAD_DIM, (hh + 1) * MEM_HEAD_DIM)
        s = _mm_nt(q[:, cols], kc_ref[0, :, cols])
        p = jnp.exp(s - jnp.max(s, axis=-1, keepdims=True))
        l = jnp.sum(p, axis=-1, keepdims=True)
        outs.append((_mm(p.astype(BF16), vc_ref[0, :, cols]) / l).astype(BF16))
    o = jnp.concatenate(outs, axis=-1)
    y = _mm(o, wco_ref[...])
    h_ref[0] = _layer_norm(DEEPNORM_ALPHA * h1 + y, g2_ref[...], b2_ref[...])


def _mix_cross(x, u, o, conv_w, conv_b, conv_g, conv_beta, w_out, g1, b1, w_cq, kc, vc, w_co, g2, b2):
    bsz, seq, _ = x.shape
    nt = seq // ROW_TILE
    const2 = lambda b, t: (0, 0)
    tok = lambda width: pl.BlockSpec((1, ROW_TILE, width), lambda b, t: (b, t, 0))
    memblk = pl.BlockSpec((1, MEM_LEN, D_MODEL), lambda b, t: (b, 0, 0))
    vec = pl.BlockSpec((1, D_MODEL), const2)
    cvec = pl.BlockSpec((1, CONV_WIDTH), const2)
    fox = pl.BlockSpec((1, FOX_HEADS // 2, ROW_TILE, LANES), lambda b, t: (b, 0, t, 0))
    mat = pl.BlockSpec((D_MODEL, D_MODEL), const2)
    return pl.pallas_call(
        _mix_cross_kernel,
        grid=(bsz, nt),
        in_specs=[tok(D_MODEL), tok(CONV_WIDTH), fox, pl.BlockSpec(conv_w.shape, const2), cvec, cvec,
                  cvec, mat, vec, vec, mat, memblk, memblk, mat, vec, vec],
        out_specs=tok(D_MODEL),
        out_shape=jax.ShapeDtypeStruct((bsz, seq, D_MODEL), F32),
        scratch_shapes=[
            pltpu.VMEM((ROW_TILE + HALO, CONV_WIDTH), F32),
            pltpu.VMEM((SUBLANES - 1, ROW_TILE + HALO - SUBLANES, CONV_WIDTH), F32),
        ],
        compiler_params=pltpu.CompilerParams(
            dimension_semantics=("arbitrary", "arbitrary"), vmem_limit_bytes=VMEM_LIMIT),
        name="mix_cross",
    )(x, u, o, conv_w, conv_b, conv_g, conv_beta, w_out, g1, b1, w_cq, kc, vc, w_co, g2, b2)


def _swiglu_kernel(h_ref, wg_ref, wu_ref, wd_ref, g_ref, b_ref, out_ref):
    h = h_ref[...]
    hb = h.astype(BF16)
    acc = DEEPNORM_ALPHA * h
    for c in range(D_FF // FF_CHUNK):
        cols = slice(c * FF_CHUNK, (c + 1) * FF_CHUNK)
        gate = _mm(hb, wg_ref[:, cols])
        up = _mm(hb, wu_ref[:, cols])
        act = (gate * _sigmoid(gate) * up).astype(BF16)
        acc = acc + _mm(act, wd_ref[cols, :])
    out_ref[...] = _layer_norm(acc, g_ref[...], b_ref[...])


def _swiglu(h, w_gate, w_up, w_down, g, b):
    n = h.shape[0]
    const2 = lambda t: (0, 0)
    tok = pl.BlockSpec((FF_ROW_TILE, D_MODEL), lambda t: (t, 0))
    return pl.pallas_call(
        _swiglu_kernel,
        grid=(n // FF_ROW_TILE,),
        in_specs=[tok, pl.BlockSpec(w_gate.shape, const2), pl.BlockSpec(w_up.shape, const2),
                  pl.BlockSpec(w_down.shape, const2), pl.BlockSpec((1, D_MODEL), const2),
                  pl.BlockSpec((1, D_MODEL), const2)],
        out_specs=tok,
        out_shape=jax.ShapeDtypeStruct((n, D_MODEL), F32),
        compiler_params=pltpu.CompilerParams(
            dimension_semantics=("arbitrary",), vmem_limit_bytes=VMEM_LIMIT),
        name="swiglu",
    )(h, w_gate, w_up, w_down, g, b)


def _layer(h, mem, w_in, b_forget, conv_w, conv_b, conv_ln_g, conv_ln_b, w_out, ln_mix_g, ln_mix_b,
           w_cq, w_ck, w_cv, w_co, ln_cross_g, ln_cross_b, w_gate, w_up, w_down, ln_ffn_g, ln_ffn_b):
    bsz, seq, _ = h.shape
    row = lambda v: v.reshape(1, -1).astype(F32)
    w_main = w_in[:, :C_F].astype(BF16)
    w_f = jnp.pad(w_in[:, C_F:], ((0, 0), (0, LANES - FOX_HEADS))).astype(BF16)
    b_f = jnp.pad(b_forget.astype(F32), (0, LANES - FOX_HEADS)).reshape(1, LANES)
    tri = jnp.tri(ROW_TILE, dtype=BF16)

    later = [w.astype(F32) for w in (w_ck, w_cv, w_out, w_cq, w_co, w_gate, w_up, w_down)]
    (u, q_aug, k_aug, v), later = _inproj(h, w_main, w_f, b_f, tri, later)
    w_ck, w_cv, w_out, w_cq, w_co, w_gate, w_up, w_down = later
    o = _fox_attention(q_aug, k_aug, v)
    kc, vc = _mem_kv(mem, w_ck, w_cv)
    h2 = _mix_cross(h, u, o, conv_w.astype(F32), row(conv_b), row(conv_ln_g), row(conv_ln_b),
                    w_out, row(ln_mix_g), row(ln_mix_b), w_cq, kc, vc, w_co, row(ln_cross_g), row(ln_cross_b))
    h3 = _swiglu(h2.reshape(bsz * seq, D_MODEL), w_gate, w_up, w_down, row(ln_ffn_g), row(ln_ffn_b))
    return h3.reshape(bsz, seq, D_MODEL)


def kernel(x, mem, w_in, b_forget, conv_w, conv_b, conv_ln_g, conv_ln_b, w_out, ln_mix_g, ln_mix_b,
           w_cq, w_ck, w_cv, w_co, ln_cross_g, ln_cross_b, w_gate, w_up, w_down, ln_ffn_g, ln_ffn_b):
    depth = w_in.shape[0]
    assert depth == 1, "DEEPNORM_ALPHA is fixed for a single layer"
    h = x
    for l in range(depth):
        h = _layer(h, mem, w_in[l], b_forget[l], conv_w[l], conv_b[l], conv_ln_g[l], conv_ln_b[l],
                   w_out[l], ln_mix_g[l], ln_mix_b[l], w_cq[l], w_ck[l], w_cv[l], w_co[l],
                   ln_cross_g[l], ln_cross_b[l], w_gate[l], w_up[l], w_down[l], ln_ffn_g[l], ln_ffn_b[l])
    return h
```

```python
import functools
import math

import jax
import jax.numpy as jnp
from jax import lax
from jax.experimental import pallas as pl
from jax.experimental.pallas import tpu as pltpu

D_MODEL = 1024
CONV_WIDTH = 512
CONV_K = 31
FOX_WIDTH = 512
FOX_HEADS = 8
FOX_HEAD_DIM = 64
MEM_LEN = 256
MEM_HEADS = 4
MEM_HEAD_DIM = 256
D_FF = 2816
LN_EPS = 1e-5
NEG_INF = -1e30
DEEPNORM_ALPHA = 2.0 ** 0.25
LOG2E = math.log2(math.e)

LANES = 128
SUBLANES = 8
HALO = 32
ROW_TILE = 512
FF_ROW_TILE = 1024
Q_TILE = 1024
K_TILE = 1024
DIAG = Q_TILE // K_TILE
Q_SUB = 256
EXP_ROWS = 64
VT_ROWS = FOX_HEAD_DIM + 16
FF_CHUNK = 256
CONV_ROWS = 32
CHAIN_SPLIT = 2
VMEM_LIMIT = 56 * 1024 * 1024

BF16 = jnp.bfloat16
F32 = jnp.float32

C_GLU_A = 0
C_GLU_B = CONV_WIDTH
C_Q = 2 * CONV_WIDTH
C_K = C_Q + FOX_WIDTH
C_V = C_K + FOX_WIDTH
C_F = C_V + FOX_WIDTH


def _mm(a, b):
    return jnp.dot(a, b, preferred_element_type=F32)


def _mm_nt(a, b):
    return lax.dot_general(a, b, (((1,), (1,)), ((), ())), preferred_element_type=F32)


def _layer_norm(x, g, b):
    mu = jnp.mean(x, axis=-1, keepdims=True)
    xc = x - mu
    var = jnp.mean(xc * xc, axis=-1, keepdims=True)
    return xc * lax.rsqrt(var + LN_EPS) * g + b


def _sigmoid(x):
    return 1.0 / (1.0 + jnp.exp(-x))


def _split3(x):
    hi = x.astype(BF16).astype(F32)
    r = x - hi
    mid = r.astype(BF16).astype(F32)
    lo = (r - mid).astype(BF16).astype(F32)
    return hi, mid, lo


def _pack3(hi, mid, lo):
    return hi + pltpu.roll(mid, 8, axis=1) + pltpu.roll(lo, 16, axis=1)


def _inproj_kernel(x_ref, w_ref, wf_ref, bf_ref, tri_ref, *refs, n_cast):
    cast_in, (u_ref, q_ref, k_ref, v_ref), cast_out = refs[:n_cast], refs[n_cast:n_cast + 4], refs[n_cast + 4:-1]
    carry = refs[-1]
    t = pl.program_id(1)
    rows = x_ref.shape[1]
    xb = x_ref[0].astype(BF16)

    for src_ref, dst_ref in zip(cast_in, cast_out):
        dst_ref[...] = src_ref[...].astype(BF16)

    @pl.when(t == 0)
    def _():
        carry[...] = jnp.zeros_like(carry)

    glu_a = _mm(xb, w_ref[:, C_GLU_A:C_GLU_A + CONV_WIDTH])
    glu_b = _mm(xb, w_ref[:, C_GLU_B:C_GLU_B + CONV_WIDTH])
    u_ref[0] = glu_a * _sigmoid(glu_b)

    lane = lax.broadcasted_iota(jnp.int32, (rows, LANES), 1)
    f = _mm(xb, wf_ref[...]) + bf_ref[...]
    logf = jnp.minimum(f, 0.0) - jnp.log(1.0 + jnp.exp(-jnp.abs(f)))
    logf = jnp.where(lane < FOX_HEADS, logf, 0.0)
    packed = _pack3(*_split3(logf)).astype(BF16)
    res = _mm(tri_ref[...], packed)
    cum = res + pltpu.roll(res, LANES - 8, axis=1) + pltpu.roll(res, LANES - 16, axis=1)
    cum = jnp.where(lane < FOX_HEADS, cum + carry[0:1, :], 0.0)
    carry[0:1, :] = cum[rows - 1:rows, :]
    cpack = _pack3(*_split3(cum * LOG2E))

    scale = LOG2E / math.sqrt(FOX_HEAD_DIM)
    for hp in range(FOX_HEADS // 2):
        if hp % 2 == 0:
            qquad = _mm(xb, w_ref[:, C_Q + hp * LANES:C_Q + (hp + 2) * LANES]) * scale
            kquad = _mm(xb, w_ref[:, C_K + hp * LANES:C_K + (hp + 2) * LANES])
        qpair = qquad[:, (hp % 2) * LANES:(hp % 2 + 1) * LANES]
        kpair = kquad[:, (hp % 2) * LANES:(hp % 2 + 1) * LANES]
        for sub in range(2):
            h = 2 * hp + sub
            aug0 = FOX_HEAD_DIM if sub == 0 else 0
            data = (lane < FOX_HEAD_DIM) if sub == 0 else (lane >= FOX_HEAD_DIM)
            slot_a = (lane == aug0) | (lane == aug0 + 8) | (lane == aug0 + 16)
            slot_b = (lane == aug0 + 1) | (lane == aug0 + 9) | (lane == aug0 + 17)
            ra = pltpu.roll(cpack, (aug0 - h) % LANES, axis=1)
            rb = pltpu.roll(cpack, (aug0 + 1 - h) % LANES, axis=1)
            qa = jnp.where(data, qpair, jnp.where(slot_a, ra, jnp.where(slot_b, 1.0, 0.0)))
            ka = jnp.where(data, kpair, jnp.where(slot_a, 1.0, jnp.where(slot_b, -rb, 0.0)))
            q_ref[0, h] = qa.astype(BF16)
            k_ref[0, h] = ka.astype(BF16)
    v_ref[0] = _mm(xb, w_ref[:, C_V:C_V + FOX_WIDTH]).astype(BF16)


def _slab_spec(rows, cols, steps, steps_per_seq):
    nslabs = steps
    while rows % nslabs or (rows // nslabs) % (2 * SUBLANES):
        nslabs //= 2
    hold = steps // nslabs
    return pl.BlockSpec((rows // nslabs, cols), lambda b, t: ((b * steps_per_seq + t) // hold, 0))


def _inproj(x, w_main, w_f, b_f, tri, to_cast):
    bsz, seq, _ = x.shape
    nt = seq // ROW_TILE
    const2 = lambda b, t: (0, 0)
    slabs = [_slab_spec(w.shape[0], w.shape[1], bsz * nt, nt) for w in to_cast]
    outs = pl.pallas_call(
        functools.partial(_inproj_kernel, n_cast=len(to_cast)),
        grid=(bsz, nt),
        in_specs=[
            pl.BlockSpec((1, ROW_TILE, D_MODEL), lambda b, t: (b, t, 0)),
            pl.BlockSpec(w_main.shape, const2),
            pl.BlockSpec(w_f.shape, const2),
            pl.BlockSpec(b_f.shape, const2),
            pl.BlockSpec(tri.shape, const2),
        ] + slabs,
        out_specs=[
            pl.BlockSpec((1, ROW_TILE, CONV_WIDTH), lambda b, t: (b, t, 0)),
            pl.BlockSpec((1, FOX_HEADS, ROW_TILE, LANES), lambda b, t: (b, 0, t, 0)),
            pl.BlockSpec((1, FOX_HEADS, ROW_TILE, LANES), lambda b, t: (b, 0, t, 0)),
            pl.BlockSpec((1, ROW_TILE, FOX_WIDTH), lambda b, t: (b, t, 0)),
        ] + slabs,
        out_shape=[
            jax.ShapeDtypeStruct((bsz, seq, CONV_WIDTH), F32),
            jax.ShapeDtypeStruct((bsz, FOX_HEADS, seq, LANES), BF16),
            jax.ShapeDtypeStruct((bsz, FOX_HEADS, seq, LANES), BF16),
            jax.ShapeDtypeStruct((bsz, seq, FOX_WIDTH), BF16),
        ] + [jax.ShapeDtypeStruct(w.shape, BF16) for w in to_cast],
        scratch_shapes=[pltpu.VMEM((SUBLANES, LANES), F32)],
        compiler_params=pltpu.CompilerParams(
            dimension_semantics=("arbitrary", "arbitrary"), vmem_limit_bytes=VMEM_LIMIT),
        name="inproj",
    )(x, w_main, w_f, b_f, tri, *to_cast)
    return outs[:4], outs[4:]


def _fox_kernel(q_ref, qn_ref, k_ref, v_ref, o_ref, vt_scr, qt_scr, qtn_scr, acc_scr, s_scr, m_scr):
    i = pl.program_id(1)
    nq = pl.num_programs(1)
    nk = vt_scr.shape[0]
    npairs = FOX_HEADS // 2
    units = [(h, qb) for h in range(2) for qb in range(Q_TILE // Q_SUB)]
    first_diag = DIAG * i

    def band_keys(qb, band):
        if band is None:
            return K_TILE
        return max(0, min(K_TILE, (qb + 1) * Q_SUB - band * K_TILE))

    def transpose_q(src_ref, head0, dst):
        for h in range(2):
            dst[h] = src_ref[0, head0 + h].astype(F32).T.astype(BF16)

    def score_unit(pair, j, u, band, fresh, qt):
        h, qb = units[u]
        nkeys = band_keys(qb, band)
        if nkeys == 0:
            return
        kt = k_ref[0, 2 * pair + h, pl.ds(pl.multiple_of(j * K_TILE, K_TILE), nkeys), :]
        st = _mm(kt, qt[h, :, qb * Q_SUB:(qb + 1) * Q_SUB])
        if band is not None and band * K_TILE + nkeys - 1 > qb * Q_SUB:
            kpos = lax.broadcasted_iota(jnp.int32, st.shape, 0) + band * K_TILE
            qpos = lax.broadcasted_iota(jnp.int32, st.shape, 1) + qb * Q_SUB
            st = jnp.where(kpos <= qpos, st, NEG_INF)
        s_scr[u, 0:nkeys, :] = st
        m_prev = jnp.full((1, Q_SUB), NEG_INF, F32) if fresh else m_scr[u, 1]
        m_scr[u, 0] = m_prev
        m_scr[u, 1] = jnp.maximum(m_prev, jnp.max(st, axis=0, keepdims=True))

    def value_unit(pair, j, u, band):
        h, qb = units[u]
        nkeys = band_keys(qb, band)
        if nkeys == 0:
            return
        m_new = m_scr[u, 1]
        alpha = jnp.exp2(m_scr[u, 0] - m_new)
        pt = jnp.exp2((s_scr[u, 0:nkeys, :] - m_new).astype(BF16))
        acc_scr[u] = alpha * acc_scr[u] + _mm(vt_scr[j, 2 * pair + h, :, 0:nkeys], pt)

    def stages(value=None, score=None, value_band=None, score_band=None, fresh=False, qt=None):
        for u in range(len(units)):
            if value is not None:
                value_unit(value[0], value[1], u, value_band)
            if score is not None:
                score_unit(score[0], score[1], u, score_band, fresh, qt_scr.at[score[0]] if qt is None else qt)

    @pl.when(i == 0)
    def _():
        tail = (lax.broadcasted_iota(jnp.int32, (VT_ROWS - FOX_HEAD_DIM, K_TILE), 0) == 0).astype(BF16)
        for c in range(nk):
            for pair in range(npairs):
                blk = v_ref[0, c * K_TILE:(c + 1) * K_TILE, pair * LANES:(pair + 1) * LANES]
                blk = blk.astype(F32).T.astype(BF16)
                for h in range(2):
                    vt_scr[c, 2 * pair + h, 0:FOX_HEAD_DIM, :] = blk[h * FOX_HEAD_DIM:(h + 1) * FOX_HEAD_DIM, :]
                    vt_scr[c, 2 * pair + h, FOX_HEAD_DIM:VT_ROWS, :] = tail

    for pair in range(npairs):
        transpose_q(q_ref, 2 * pair, qt_scr.at[pair])

    @pl.when(i == 0)
    def _():
        stages(score=(0, 0), score_band=0, fresh=True)

    def one_pair(pair, carry):
        acc_scr[...] = jnp.zeros(acc_scr.shape, F32)

        def trip(t, c):
            stages(value=(pair, t), score=(pair, t + 1))
            return c

        lax.fori_loop(0, first_diag - 1, trip, 0)

        @pl.when(i > 0)
        def _():
            stages(value=(pair, first_diag - 1), score=(pair, first_diag), score_band=0)

        for band in range(DIAG - 1):
            stages(value=(pair, first_diag + band), score=(pair, first_diag + band + 1),
                   value_band=band, score_band=band + 1)

        last = (pair, first_diag + DIAG - 1)
        more_pairs = pair + 1 < npairs

        @pl.when(more_pairs & (i == 0))
        def _():
            stages(value=last, score=(pair + 1, 0), value_band=DIAG - 1, score_band=0, fresh=True)

        @pl.when(more_pairs & (i > 0))
        def _():
            stages(value=last, score=(pair + 1, 0), value_band=DIAG - 1, fresh=True)

        @pl.when(jnp.logical_not(more_pairs) & (i < nq - 1))
        def _():
            transpose_q(qn_ref, 0, qtn_scr)
            stages(value=last, score=(0, 0), value_band=DIAG - 1, fresh=True, qt=qtn_scr)

        @pl.when(jnp.logical_not(more_pairs) & (i == nq - 1))
        def _():
            stages(value=last, value_band=DIAG - 1)

        rows = []
        for h in range(2):
            blocks = [acc_scr[u] for u in range(len(units)) if units[u][0] == h]
            rows.append(jnp.concatenate(
                [a[0:FOX_HEAD_DIM] / a[FOX_HEAD_DIM:FOX_HEAD_DIM + 1] for a in blocks], axis=1))
        o_ref[0, pair] = jnp.concatenate(rows, axis=0).T.astype(BF16)
        return carry

    lax.fori_loop(0, npairs, one_pair, 0)


def _fox_attention(q_aug, k_aug, v):
    bsz, _, seq, _ = q_aug.shape
    nq = seq // Q_TILE
    nk = seq // K_TILE
    return pl.pallas_call(
        _fox_kernel,
        grid=(bsz, nq),
        in_specs=[
            pl.BlockSpec((1, FOX_HEADS, Q_TILE, LANES), lambda b, i: (b, 0, i, 0)),
            pl.BlockSpec((1, 2, Q_TILE, LANES), lambda b, i: (b, 0, jnp.minimum(i + 1, nq - 1), 0)),
            pl.BlockSpec((1, FOX_HEADS, seq, LANES), lambda b, i: (b, 0, 0, 0)),
            pl.BlockSpec((1, seq, FOX_WIDTH), lambda b, i: (b, 0, 0)),
        ],
        out_specs=pl.BlockSpec((1, FOX_HEADS // 2, Q_TILE, LANES), lambda b, i: (b, 0, i, 0)),
        out_shape=jax.ShapeDtypeStruct((bsz, FOX_HEADS // 2, seq, LANES), BF16),
        scratch_shapes=[
            pltpu.VMEM((nk, FOX_HEADS, VT_ROWS, K_TILE), BF16),
            pltpu.VMEM((FOX_HEADS // 2, 2, LANES, Q_TILE), BF16),
            pltpu.VMEM((2, LANES, Q_TILE), BF16),
            pltpu.VMEM((2 * (Q_TILE // Q_SUB), VT_ROWS, Q_SUB), F32),
            pltpu.VMEM((2 * (Q_TILE // Q_SUB), K_TILE, Q_SUB), F32),
            pltpu.VMEM((2 * (Q_TILE // Q_SUB), 2, 1, Q_SUB), F32),
        ],
        compiler_params=pltpu.CompilerParams(
            dimension_semantics=("arbitrary", "arbitrary"), vmem_limit_bytes=VMEM_LIMIT),
        name="fox_attention",
    )(q_aug, q_aug, k_aug, v)


def _memkv_kernel(mem_ref, wk_ref, wv_ref, k_ref, v_ref):
    mb = mem_ref[0].astype(BF16)
    k_ref[0] = _mm(mb, wk_ref[...]).astype(BF16)
    v_ref[0] = _mm(mb, wv_ref[...]).astype(BF16)


def _mem_kv(mem, w_ck, w_cv):
    bsz = mem.shape[0]
    const2 = lambda b: (0, 0)
    blk = pl.BlockSpec((1, MEM_LEN, D_MODEL), lambda b: (b, 0, 0))
    return pl.pallas_call(
        _memkv_kernel,
        grid=(bsz,),
        in_specs=[blk, pl.BlockSpec(w_ck.shape, const2), pl.BlockSpec(w_cv.shape, const2)],
        out_specs=[blk, blk],
        out_shape=[jax.ShapeDtypeStruct((bsz, MEM_LEN, D_MODEL), BF16)] * 2,
        compiler_params=pltpu.CompilerParams(
            dimension_semantics=("arbitrary",), vmem_limit_bytes=VMEM_LIMIT),
        name="mem_kv",
    )(mem, w_ck, w_cv)


def _conv_branch(u_ref, cw_ref, cb_ref, cg_ref, cbeta_ref, ubuf, shifted):
    t = pl.program_id(1)
    rows = u_ref.shape[1]
    first = HALO - (CONV_K - 1)
    span = shifted.shape[1]

    @pl.when(t == 0)
    def _():
        ubuf[0:HALO, :] = jnp.zeros((HALO, CONV_WIDTH), F32)

    @pl.when(t > 0)
    def _():
        ubuf[0:HALO, :] = ubuf[rows:rows + HALO, :]

    ubuf[HALO:HALO + rows, :] = u_ref[0]
    for p in range(1, SUBLANES):
        shifted[p - 1] = ubuf[p:p + span, :]

    cw = cw_ref[...]
    cbias = cb_ref[...]
    gam = cg_ref[...]
    beta = cbeta_ref[...]
    outs = []
    for r in range(rows // CONV_ROWS):
        base = r * CONV_ROWS
        acc = jnp.broadcast_to(cbias, (CONV_ROWS, CONV_WIDTH))
        for j in range(CONV_K):
            a, p = divmod(first + j, SUBLANES)
            lo = base + SUBLANES * a
            src = ubuf[lo:lo + CONV_ROWS, :] if p == 0 else shifted[p - 1, lo:lo + CONV_ROWS, :]
            acc = acc + cw[j:j + 1, :] * src
        y = _layer_norm(acc, gam, beta)
        outs.append((y * _sigmoid(y)).astype(BF16))
    return jnp.concatenate(outs, axis=0)


def _mix_cross_kernel(x_ref, u_ref, o_ref, cw_ref, cb_ref, cg_ref, cbeta_ref, wout_ref, g1_ref, b1_ref,
                      wcq_ref, kc_ref, vc_ref, wco_ref, g2_ref, b2_ref, h_ref, ubuf, shifted):
    uc = _conv_branch(u_ref, cw_ref, cb_ref, cg_ref, cbeta_ref, ubuf, shifted)
    o_fox = jnp.concatenate([o_ref[0, p] for p in range(o_ref.shape[1])], axis=1)
    rows = x_ref.shape[1]
    groups = [slice(g * rows // CHAIN_SPLIT, (g + 1) * rows // CHAIN_SPLIT) for g in range(CHAIN_SPLIT)]
    mix = [_mm(uc[r], wout_ref[0:CONV_WIDTH, :]) + _mm(o_fox[r], wout_ref[CONV_WIDTH:, :]) for r in groups]
    h1, q = [], []
    for g, r in enumerate(groups):
        h1.append(_layer_norm(DEEPNORM_ALPHA * x_ref[0, r, :] + mix[g], g1_ref[...], b1_ref[...]))
        q.append((_mm(h1[g].astype(BF16), wcq_ref[...]) * (1.0 / math.sqrt(MEM_HEAD_DIM))).astype(BF16))
    o = []
    for g in range(CHAIN_SPLIT):
        outs = []
        for hh in range(MEM_HEADS):
            cols = slice(hh * MEM_HEAD_DIM, (hh + 1) * MEM_HEAD_DIM)
            s = _mm_nt(q[g][:, cols], kc_ref[0, :, cols])
            p = jnp.exp(s - jnp.max(s, axis=-1, keepdims=True))
            l = jnp.sum(p, axis=-1, keepdims=True)
            outs.append((_mm(p.astype(BF16), vc_ref[0, :, cols]) / l).astype(BF16))
        o.append(jnp.concatenate(outs, axis=-1))
    y = [_mm(o[g], wco_ref[...]) for g in range(CHAIN_SPLIT)]
    for g, r in enumerate(groups):
        h_ref[0, r, :] = _layer_norm(DEEPNORM_ALPHA * h1[g] + y[g], g2_ref[...], b2_ref[...])


def _mix_cross(x, u, o, conv_w, conv_b, conv_g, conv_beta, w_out, g1, b1, w_cq, kc, vc, w_co, g2, b2):
    bsz, seq, _ = x.shape
    nt = seq // ROW_TILE
    const2 = lambda b, t: (0, 0)
    tok = lambda width: pl.BlockSpec((1, ROW_TILE, width), lambda b, t: (b, t, 0))
    memblk = pl.BlockSpec((1, MEM_LEN, D_MODEL), lambda b, t: (b, 0, 0))
    vec = pl.BlockSpec((1, D_MODEL), const2)
    cvec = pl.BlockSpec((1, CONV_WIDTH), const2)
    fox = pl.BlockSpec((1, FOX_HEADS // 2, ROW_TILE, LANES), lambda b, t: (b, 0, t, 0))
    mat = pl.BlockSpec((D_MODEL, D_MODEL), const2)
    return pl.pallas_call(
        _mix_cross_kernel,
        grid=(bsz, nt),
        in_specs=[tok(D_MODEL), tok(CONV_WIDTH), fox, pl.BlockSpec(conv_w.shape, const2), cvec, cvec,
                  cvec, mat, vec, vec, mat, memblk, memblk, mat, vec, vec],
        out_specs=tok(D_MODEL),
        out_shape=jax.ShapeDtypeStruct((bsz, seq, D_MODEL), F32),
        scratch_shapes=[
            pltpu.VMEM((ROW_TILE + HALO, CONV_WIDTH), F32),
            pltpu.VMEM((SUBLANES - 1, ROW_TILE + HALO - SUBLANES, CONV_WIDTH), F32),
        ],
        compiler_params=pltpu.CompilerParams(
            dimension_semantics=("arbitrary", "arbitrary"), vmem_limit_bytes=VMEM_LIMIT),
        name="mix_cross",
    )(x, u, o, conv_w, conv_b, conv_g, conv_beta, w_out, g1, b1, w_cq, kc, vc, w_co, g2, b2)


def _swiglu_kernel(h_ref, wg_ref, wu_ref, wd_ref, g_ref, b_ref, out_ref):
    h = h_ref[...]
    hb = h.astype(BF16)
    acc = DEEPNORM_ALPHA * h
    for c in range(D_FF // FF_CHUNK):
        cols = slice(c * FF_CHUNK, (c + 1) * FF_CHUNK)
        gate = _mm(hb, wg_ref[:, cols])
        up = _mm(hb, wu_ref[:, cols])
        act = (gate * _sigmoid(gate) * up).astype(BF16)
        acc = acc + _mm(act, wd_ref[cols, :])
    out_ref[...] = _layer_norm(acc, g_ref[...], b_ref[...])


def _swiglu(h, w_gate, w_up, w_down, g, b):
    n = h.shape[0]
    const2 = lambda t: (0, 0)
    tok = pl.BlockSpec((FF_ROW_TILE, D_MODEL), lambda t: (t, 0))
    return pl.pallas_call(
        _swiglu_kernel,
        grid=(n // FF_ROW_TILE,),
        in_specs=[tok, pl.BlockSpec(w_gate.shape, const2), pl.BlockSpec(w_up.shape, const2),
                  pl.BlockSpec(w_down.shape, const2), pl.BlockSpec((1, D_MODEL), const2),
                  pl.BlockSpec((1, D_MODEL), const2)],
        out_specs=tok,
        out_shape=jax.ShapeDtypeStruct((n, D_MODEL), F32),
        compiler_params=pltpu.CompilerParams(
            dimension_semantics=("arbitrary",), vmem_limit_bytes=VMEM_LIMIT),
        name="swiglu",
    )(h, w_gate, w_up, w_down, g, b)


def _layer(h, mem, w_in, b_forget, conv_w, conv_b, conv_ln_g, conv_ln_b, w_out, ln_mix_g, ln_mix_b,
           w_cq, w_ck, w_cv, w_co, ln_cross_g, ln_cross_b, w_gate, w_up, w_down, ln_ffn_g, ln_ffn_b):
    bsz, seq, _ = h.shape
    row = lambda v: v.reshape(1, -1).astype(F32)
    w_main = w_in[:, :C_F].astype(BF16)
    w_f = jnp.pad(w_in[:, C_F:], ((0, 0), (0, LANES - FOX_HEADS))).astype(BF16)
    b_f = jnp.pad(b_forget.astype(F32), (0, LANES - FOX_HEADS)).reshape(1, LANES)
    tri = jnp.tri(ROW_TILE, dtype=BF16)

    later = [w.astype(F32) for w in (w_ck, w_cv, w_out, w_cq, w_co, w_gate, w_up, w_down)]
    (u, q_aug, k_aug, v), later = _inproj(h, w_main, w_f, b_f, tri, later)
    w_ck, w_cv, w_out, w_cq, w_co, w_gate, w_up, w_down = later
    o = _fox_attention(q_aug, k_aug, v)
    kc, vc = _mem_kv(mem, w_ck, w_cv)
    h2 = _mix_cross(h, u, o, conv_w.astype(F32), row(conv_b), row(conv_ln_g), row(conv_ln_b),
                    w_out, row(ln_mix_g), row(ln_mix_b), w_cq, kc, vc, w_co, row(ln_cross_g), row(ln_cross_b))
    h3 = _swiglu(h2.reshape(bsz * seq, D_MODEL), w_gate, w_up, w_down, row(ln_ffn_g), row(ln_ffn_b))
    return h3.reshape(bsz, seq, D_MODEL)


def kernel(x, mem, w_in, b_forget, conv_w, conv_b, conv_ln_g, conv_ln_b, w_out, ln_mix_g, ln_mix_b,
           w_cq, w_ck, w_cv, w_co, ln_cross_g, ln_cross_b, w_gate, w_up, w_down, ln_ffn_g, ln_ffn_b):
    depth = w_in.shape[0]
    assert depth == 1, "DEEPNORM_ALPHA is fixed for a single layer"
    h = x
    for l in range(depth):
        h = _layer(h, mem, w_in[l], b_forget[l], conv_w[l], conv_b[l], conv_ln_g[l], conv_ln_b[l],
                   w_out[l], ln_mix_g[l], ln_mix_b[l], w_cq[l], w_ck[l], w_cv[l], w_co[l],
                   ln_cross_g[l], ln_cross_b[l], w_gate[l], w_up[l], w_down[l], ln_ffn_g[l], ln_ffn_b[l])
    return h
```

```python
import functools
import math

import jax
import jax.numpy as jnp
from jax import lax
from jax.experimental import pallas as pl
from jax.experimental.pallas import tpu as pltpu

D_MODEL = 1024
CONV_WIDTH = 512
CONV_K = 31
FOX_WIDTH = 512
FOX_HEADS = 8
FOX_HEAD_DIM = 64
MEM_LEN = 256
MEM_HEADS = 4
MEM_HEAD_DIM = 256
D_FF = 2816
LN_EPS = 1e-5
NEG_INF = -1e30
DEEPNORM_ALPHA = 2.0 ** 0.25
LOG2E = math.log2(math.e)

LANES = 128
SUBLANES = 8
HALO = 32
ROW_TILE = 512
FF_ROW_TILE = 1024
Q_TILE = 1024
K_TILE = 1024
DIAG = Q_TILE // K_TILE
Q_SUB = 256
VT_ROWS = FOX_HEAD_DIM + 16
FF_CHUNK = 256
CONV_ROWS = 32
CHAIN_SPLIT = 2
VMEM_LIMIT = 56 * 1024 * 1024

BF16 = jnp.bfloat16
F32 = jnp.float32

C_GLU_A = 0
C_GLU_B = CONV_WIDTH
C_Q = 2 * CONV_WIDTH
C_K = C_Q + FOX_WIDTH
C_V = C_K + FOX_WIDTH
C_F = C_V + FOX_WIDTH


def _mm(a, b):
    return jnp.dot(a, b, preferred_element_type=F32)


def _mm_nt(a, b):
    return lax.dot_general(a, b, (((1,), (1,)), ((), ())), preferred_element_type=F32)


def _layer_norm(x, g, b):
    mu = jnp.mean(x, axis=-1, keepdims=True)
    xc = x - mu
    var = jnp.mean(xc * xc, axis=-1, keepdims=True)
    return xc * lax.rsqrt(var + LN_EPS) * g + b


def _sigmoid(x):
    return 1.0 / (1.0 + jnp.exp(-x))


def _split3(x):
    hi = x.astype(BF16).astype(F32)
    r = x - hi
    mid = r.astype(BF16).astype(F32)
    lo = (r - mid).astype(BF16).astype(F32)
    return hi, mid, lo


def _pack3(hi, mid, lo):
    return hi + pltpu.roll(mid, 8, axis=1) + pltpu.roll(lo, 16, axis=1)


def _inproj_kernel(x_ref, w_ref, wf_ref, bf_ref, tri_ref, *refs, n_cast):
    cast_in, (u_ref, q_ref, k_ref, v_ref), cast_out = refs[:n_cast], refs[n_cast:n_cast + 4], refs[n_cast + 4:-1]
    carry = refs[-1]
    t = pl.program_id(1)
    rows = x_ref.shape[1]
    xb = x_ref[0].astype(BF16)

    @pl.when(t == 0)
    def _():
        carry[...] = jnp.zeros_like(carry)

    glu_a = _mm(xb, w_ref[:, C_GLU_A:C_GLU_A + CONV_WIDTH])
    glu_b = _mm(xb, w_ref[:, C_GLU_B:C_GLU_B + CONV_WIDTH])
    u_ref[0] = glu_a * _sigmoid(glu_b)

    lane = lax.broadcasted_iota(jnp.int32, (rows, LANES), 1)
    f = _mm(xb, wf_ref[...]) + bf_ref[...]
    logf = jnp.minimum(f, 0.0) - jnp.log(1.0 + jnp.exp(-jnp.abs(f)))
    logf = jnp.where(lane < FOX_HEADS, logf, 0.0)
    packed = _pack3(*_split3(logf)).astype(BF16)
    res = _mm(tri_ref[...], packed)
    cum = res + pltpu.roll(res, LANES - 8, axis=1) + pltpu.roll(res, LANES - 16, axis=1)
    cum = jnp.where(lane < FOX_HEADS, cum + carry[0:1, :], 0.0)
    carry[0:1, :] = cum[rows - 1:rows, :]
    cpack = _pack3(*_split3(cum * LOG2E))

    scale = LOG2E / math.sqrt(FOX_HEAD_DIM)
    for hp in range(FOX_HEADS // 2):
        if hp % 2 == 0:
            qquad = _mm(xb, w_ref[:, C_Q + hp * LANES:C_Q + (hp + 2) * LANES]) * scale
            kquad = _mm(xb, w_ref[:, C_K + hp * LANES:C_K + (hp + 2) * LANES])
        qpair = qquad[:, (hp % 2) * LANES:(hp % 2 + 1) * LANES]
        kpair = kquad[:, (hp % 2) * LANES:(hp % 2 + 1) * LANES]
        for sub in range(2):
            h = 2 * hp + sub
            aug0 = FOX_HEAD_DIM if sub == 0 else 0
            data = (lane < FOX_HEAD_DIM) if sub == 0 else (lane >= FOX_HEAD_DIM)
            slot_a = (lane == aug0) | (lane == aug0 + 8) | (lane == aug0 + 16)
            slot_b = (lane == aug0 + 1) | (lane == aug0 + 9) | (lane == aug0 + 17)
            ra = pltpu.roll(cpack, (aug0 - h) % LANES, axis=1)
            rb = pltpu.roll(cpack, (aug0 + 1 - h) % LANES, axis=1)
            qa = jnp.where(data, qpair, jnp.where(slot_a, ra, jnp.where(slot_b, 1.0, 0.0)))
            ka = jnp.where(data, kpair, jnp.where(slot_a, 1.0, jnp.where(slot_b, -rb, 0.0)))
            q_ref[0, h] = qa.astype(BF16)
            k_ref[0, h] = ka.astype(BF16)
    v_ref[0] = _mm(xb, w_ref[:, C_V:C_V + FOX_WIDTH]).astype(BF16)

    for src_ref, dst_ref in zip(cast_in, cast_out):
        dst_ref[...] = src_ref[...].astype(BF16)


def _slab_spec(rows, cols, steps, steps_per_seq):
    nslabs = steps
    while rows % nslabs or (rows // nslabs) % (2 * SUBLANES):
        nslabs //= 2
    hold = steps // nslabs
    return pl.BlockSpec((rows // nslabs, cols), lambda b, t: ((b * steps_per_seq + t) // hold, 0))


def _inproj(x, w_main, w_f, b_f, tri, to_cast):
    bsz, seq, _ = x.shape
    nt = seq // ROW_TILE
    const2 = lambda b, t: (0, 0)
    slabs = [_slab_spec(w.shape[0], w.shape[1], bsz * nt, nt) for w in to_cast]
    outs = pl.pallas_call(
        functools.partial(_inproj_kernel, n_cast=len(to_cast)),
        grid=(bsz, nt),
        in_specs=[
            pl.BlockSpec((1, ROW_TILE, D_MODEL), lambda b, t: (b, t, 0)),
            pl.BlockSpec(w_main.shape, const2),
            pl.BlockSpec(w_f.shape, const2),
            pl.BlockSpec(b_f.shape, const2),
            pl.BlockSpec(tri.shape, const2),
        ] + slabs,
        out_specs=[
            pl.BlockSpec((1, ROW_TILE, CONV_WIDTH), lambda b, t: (b, t, 0)),
            pl.BlockSpec((1, FOX_HEADS, ROW_TILE, LANES), lambda b, t: (b, 0, t, 0)),
            pl.BlockSpec((1, FOX_HEADS, ROW_TILE, LANES), lambda b, t: (b, 0, t, 0)),
            pl.BlockSpec((1, ROW_TILE, FOX_WIDTH), lambda b, t: (b, t, 0)),
        ] + slabs,
        out_shape=[
            jax.ShapeDtypeStruct((bsz, seq, CONV_WIDTH), F32),
            jax.ShapeDtypeStruct((bsz, FOX_HEADS, seq, LANES), BF16),
            jax.ShapeDtypeStruct((bsz, FOX_HEADS, seq, LANES), BF16),
            jax.ShapeDtypeStruct((bsz, seq, FOX_WIDTH), BF16),
        ] + [jax.ShapeDtypeStruct(w.shape, BF16) for w in to_cast],
        scratch_shapes=[pltpu.VMEM((SUBLANES, LANES), F32)],
        compiler_params=pltpu.CompilerParams(
            dimension_semantics=("arbitrary", "arbitrary"), vmem_limit_bytes=VMEM_LIMIT),
        name="inproj",
    )(x, w_main, w_f, b_f, tri, *to_cast)
    return outs[:4], outs[4:]


def _fox_kernel(q_ref, qn_ref, k_ref, v_ref, o_ref, vt_scr, qt_scr, qtn_scr, acc_scr, s_scr, m_scr):
    i = pl.program_id(1)
    nq = pl.num_programs(1)
    nk = vt_scr.shape[0]
    npairs = FOX_HEADS // 2
    units = [(h, qb) for h in range(2) for qb in range(Q_TILE // Q_SUB)]
    first_diag = DIAG * i

    def band_keys(qb, band):
        if band is None:
            return K_TILE
        return max(0, min(K_TILE, (qb + 1) * Q_SUB - band * K_TILE))

    def transpose_q(src_ref, head0, dst):
        for h in range(2):
            dst[h] = src_ref[0, head0 + h].astype(F32).T.astype(BF16)

    def score_unit(pair, j, u, band, fresh, qt):
        h, qb = units[u]
        nkeys = band_keys(qb, band)
        if nkeys == 0:
            return
        kt = k_ref[0, 2 * pair + h, pl.ds(pl.multiple_of(j * K_TILE, K_TILE), nkeys), :]
        st = _mm(kt, qt[h, :, qb * Q_SUB:(qb + 1) * Q_SUB])
        if band is not None and band * K_TILE + nkeys - 1 > qb * Q_SUB:
            kpos = lax.broadcasted_iota(jnp.int32, st.shape, 0) + band * K_TILE
            qpos = lax.broadcasted_iota(jnp.int32, st.shape, 1) + qb * Q_SUB
            st = jnp.where(kpos <= qpos, st, NEG_INF)
        s_scr[u, 0:nkeys, :] = st
        m_prev = jnp.full((1, Q_SUB), NEG_INF, F32) if fresh else m_scr[u, 1]
        m_scr[u, 0] = m_prev
        m_scr[u, 1] = jnp.maximum(m_prev, jnp.max(st, axis=0, keepdims=True))

    def value_unit(pair, j, u, band):
        h, qb = units[u]
        nkeys = band_keys(qb, band)
        if nkeys == 0:
            return
        m_new = m_scr[u, 1]
        alpha = jnp.exp2(m_scr[u, 0] - m_new)
        pt = jnp.exp2((s_scr[u, 0:nkeys, :] - m_new).astype(BF16))
        acc_scr[u] = alpha * acc_scr[u] + _mm(vt_scr[j, 2 * pair + h, :, 0:nkeys], pt)

    def stages(value=None, score=None, value_band=None, score_band=None, fresh=False, qt=None):
        for u in range(len(units)):
            if value is not None:
                value_unit(value[0], value[1], u, value_band)
            if score is not None:
                score_unit(score[0], score[1], u, score_band, fresh, qt_scr.at[score[0]] if qt is None else qt)

    @pl.when(i == 0)
    def _():
        tail = (lax.broadcasted_iota(jnp.int32, (VT_ROWS - FOX_HEAD_DIM, K_TILE), 0) == 0).astype(BF16)
        for c in range(nk):
            for pair in range(npairs):
                blk = v_ref[0, c * K_TILE:(c + 1) * K_TILE, pair * LANES:(pair + 1) * LANES]
                blk = blk.astype(F32).T.astype(BF16)
                for h in range(2):
                    vt_scr[c, 2 * pair + h, 0:FOX_HEAD_DIM, :] = blk[h * FOX_HEAD_DIM:(h + 1) * FOX_HEAD_DIM, :]
                    vt_scr[c, 2 * pair + h, FOX_HEAD_DIM:VT_ROWS, :] = tail

    for pair in range(npairs):
        transpose_q(q_ref, 2 * pair, qt_scr.at[pair])

    @pl.when(i == 0)
    def _():
        stages(score=(0, 0), score_band=0, fresh=True)

    def one_pair(pair, carry):
        acc_scr[...] = jnp.zeros(acc_scr.shape, F32)

        def trip(t, c):
            stages(value=(pair, t), score=(pair, t + 1))
            return c

        lax.fori_loop(0, first_diag - 1, trip, 0)

        @pl.when(i > 0)
        def _():
            stages(value=(pair, first_diag - 1), score=(pair, first_diag), score_band=0)

        for band in range(DIAG - 1):
            stages(value=(pair, first_diag + band), score=(pair, first_diag + band + 1),
                   value_band=band, score_band=band + 1)

        last = (pair, first_diag + DIAG - 1)
        more_pairs = pair + 1 < npairs

        @pl.when(more_pairs & (i == 0))
        def _():
            stages(value=last, score=(pair + 1, 0), value_band=DIAG - 1, score_band=0, fresh=True)

        @pl.when(more_pairs & (i > 0))
        def _():
            stages(value=last, score=(pair + 1, 0), value_band=DIAG - 1, fresh=True)

        @pl.when(jnp.logical_not(more_pairs) & (i < nq - 1))
        def _():
            transpose_q(qn_ref, 0, qtn_scr)
            stages(value=last, score=(0, 0), value_band=DIAG - 1, fresh=True, qt=qtn_scr)

        @pl.when(jnp.logical_not(more_pairs) & (i == nq - 1))
        def _():
            stages(value=last, value_band=DIAG - 1)

        rows = []
        for h in range(2):
            blocks = [acc_scr[u] for u in range(len(units)) if units[u][0] == h]
            rows.append(jnp.concatenate(
                [a[0:FOX_HEAD_DIM] / a[FOX_HEAD_DIM:FOX_HEAD_DIM + 1] for a in blocks], axis=1))
        o_ref[0, pair] = jnp.concatenate(rows, axis=0).T.astype(BF16)
        return carry

    lax.fori_loop(0, npairs, one_pair, 0)


def _fox_attention(q_aug, k_aug, v):
    bsz, _, seq, _ = q_aug.shape
    nq = seq // Q_TILE
    nk = seq // K_TILE
    return pl.pallas_call(
        _fox_kernel,
        grid=(bsz, nq),
        in_specs=[
            pl.BlockSpec((1, FOX_HEADS, Q_TILE, LANES), lambda b, i: (b, 0, i, 0)),
            pl.BlockSpec((1, 2, Q_TILE, LANES), lambda b, i: (b, 0, jnp.minimum(i + 1, nq - 1), 0)),
            pl.BlockSpec((1, FOX_HEADS, seq, LANES), lambda b, i: (b, 0, 0, 0)),
            pl.BlockSpec((1, seq, FOX_WIDTH), lambda b, i: (b, 0, 0)),
        ],
        out_specs=pl.BlockSpec((1, FOX_HEADS // 2, Q_TILE, LANES), lambda b, i: (b, 0, i, 0)),
        out_shape=jax.ShapeDtypeStruct((bsz, FOX_HEADS // 2, seq, LANES), BF16),
        scratch_shapes=[
            pltpu.VMEM((nk, FOX_HEADS, VT_ROWS, K_TILE), BF16),
            pltpu.VMEM((FOX_HEADS // 2, 2, LANES, Q_TILE), BF16),
            pltpu.VMEM((2, LANES, Q_TILE), BF16),
            pltpu.VMEM((2 * (Q_TILE // Q_SUB), VT_ROWS, Q_SUB), F32),
            pltpu.VMEM((2 * (Q_TILE // Q_SUB), K_TILE, Q_SUB), F32),
            pltpu.VMEM((2 * (Q_TILE // Q_SUB), 2, 1, Q_SUB), F32),
        ],
        compiler_params=pltpu.CompilerParams(
            dimension_semantics=("arbitrary", "arbitrary"), vmem_limit_bytes=VMEM_LIMIT),
        name="fox_attention",
    )(q_aug, q_aug, k_aug, v)


def _memkv_kernel(mem_ref, wk_ref, wv_ref, k_ref, v_ref):
    mb = mem_ref[0].astype(BF16)
    k_ref[0] = _mm(mb, wk_ref[...]).astype(BF16)
    v_ref[0] = _mm(mb, wv_ref[...]).astype(BF16)


def _mem_kv(mem, w_ck, w_cv):
    bsz = mem.shape[0]
    const2 = lambda b: (0, 0)
    blk = pl.BlockSpec((1, MEM_LEN, D_MODEL), lambda b: (b, 0, 0))
    return pl.pallas_call(
        _memkv_kernel,
        grid=(bsz,),
        in_specs=[blk, pl.BlockSpec(w_ck.shape, const2), pl.BlockSpec(w_cv.shape, const2)],
        out_specs=[blk, blk],
        out_shape=[jax.ShapeDtypeStruct((bsz, MEM_LEN, D_MODEL), BF16)] * 2,
        compiler_params=pltpu.CompilerParams(
            dimension_semantics=("arbitrary",), vmem_limit_bytes=VMEM_LIMIT),
        name="mem_kv",
    )(mem, w_ck, w_cv)


def _conv_branch(u_ref, cw_ref, cb_ref, cg_ref, cbeta_ref, ubuf, shifted):
    t = pl.program_id(1)
    rows = u_ref.shape[1]
    first = HALO - (CONV_K - 1)
    span = shifted.shape[1]

    @pl.when(t == 0)
    def _():
        ubuf[0:HALO, :] = jnp.zeros((HALO, CONV_WIDTH), F32)

    @pl.when(t > 0)
    def _():
        ubuf[0:HALO, :] = ubuf[rows:rows + HALO, :]

    ubuf[HALO:HALO + rows, :] = u_ref[0]
    for p in range(1, SUBLANES):
        shifted[p - 1] = ubuf[p:p + span, :]

    cw = cw_ref[...]
    cbias = cb_ref[...]
    gam = cg_ref[...]
    beta = cbeta_ref[...]
    outs = []
    for r in range(rows // CONV_ROWS):
        base = r * CONV_ROWS
        acc = jnp.broadcast_to(cbias, (CONV_ROWS, CONV_WIDTH))
        for j in range(CONV_K):
            a, p = divmod(first + j, SUBLANES)
            lo = base + SUBLANES * a
            src = ubuf[lo:lo + CONV_ROWS, :] if p == 0 else shifted[p - 1, lo:lo + CONV_ROWS, :]
            acc = acc + cw[j:j + 1, :] * src
        y = _layer_norm(acc, gam, beta)
        outs.append((y * _sigmoid(y)).astype(BF16))
    return jnp.concatenate(outs, axis=0)


def _mix_cross_kernel(x_ref, u_ref, o_ref, cw_ref, cb_ref, cg_ref, cbeta_ref, wout_ref, g1_ref, b1_ref,
                      wcq_ref, kc_ref, vc_ref, wco_ref, g2_ref, b2_ref, h_ref, ubuf, shifted):
    uc = _conv_branch(u_ref, cw_ref, cb_ref, cg_ref, cbeta_ref, ubuf, shifted)
    o_fox = jnp.concatenate([o_ref[0, p] for p in range(o_ref.shape[1])], axis=1)
    rows = x_ref.shape[1]
    groups = [slice(g * rows // CHAIN_SPLIT, (g + 1) * rows // CHAIN_SPLIT) for g in range(CHAIN_SPLIT)]
    mix = [_mm(uc[r], wout_ref[0:CONV_WIDTH, :]) + _mm(o_fox[r], wout_ref[CONV_WIDTH:, :]) for r in groups]
    h1, q = [], []
    for g, r in enumerate(groups):
        h1.append(_layer_norm(DEEPNORM_ALPHA * x_ref[0, r, :] + mix[g], g1_ref[...], b1_ref[...]))
        q.append((_mm(h1[g].astype(BF16), wcq_ref[...]) * (1.0 / math.sqrt(MEM_HEAD_DIM))).astype(BF16))
    o = []
    for g in range(CHAIN_SPLIT):
        outs = []
        for hh in range(MEM_HEADS):
            cols = slice(hh * MEM_HEAD_DIM, (hh + 1) * MEM_HEAD_DIM)
            s = _mm_nt(q[g][:, cols], kc_ref[0, :, cols])
            p = jnp.exp(s - jnp.max(s, axis=-1, keepdims=True))
            l = jnp.sum(p, axis=-1, keepdims=True)
            outs.append((_mm(p.astype(BF16), vc_ref[0, :, cols]) / l).astype(BF16))
        o.append(jnp.concatenate(outs, axis=-1))
    y = [_mm(o[g], wco_ref[...]) for g in range(CHAIN_SPLIT)]
    for g, r in enumerate(groups):
        h_ref[0, r, :] = _layer_norm(DEEPNORM_ALPHA * h1[g] + y[g], g2_ref[...], b2_ref[...])


def _mix_cross(x, u, o, conv_w, conv_b, conv_g, conv_beta, w_out, g1, b1, w_cq, kc, vc, w_co, g2, b2):
    bsz, seq, _ = x.shape
    nt = seq // ROW_TILE
    const2 = lambda b, t: (0, 0)
    tok = lambda width: pl.BlockSpec((1, ROW_TILE, width), lambda b, t: (b, t, 0))
    memblk = pl.BlockSpec((1, MEM_LEN, D_MODEL), lambda b, t: (b, 0, 0))
    vec = pl.BlockSpec((1, D_MODEL), const2)
    cvec = pl.BlockSpec((1, CONV_WIDTH), const2)
    fox = pl.BlockSpec((1, FOX_HEADS // 2, ROW_TILE, LANES), lambda b, t: (b, 0, t, 0))
    mat = pl.BlockSpec((D_MODEL, D_MODEL), const2)
    return pl.pallas_call(
        _mix_cross_kernel,
        grid=(bsz, nt),
        in_specs=[tok(D_MODEL), tok(CONV_WIDTH), fox, pl.BlockSpec(conv_w.shape, const2), cvec, cvec,
                  cvec, mat, vec, vec, mat, memblk, memblk, mat, vec, vec],
        out_specs=tok(D_MODEL),
        out_shape=jax.ShapeDtypeStruct((bsz, seq, D_MODEL), F32),
        scratch_shapes=[
            pltpu.VMEM((ROW_TILE + HALO, CONV_WIDTH), F32),
            pltpu.VMEM((SUBLANES - 1, ROW_TILE + HALO - SUBLANES, CONV_WIDTH), F32),
        ],
        compiler_params=pltpu.CompilerParams(
            dimension_semantics=("arbitrary", "arbitrary"), vmem_limit_bytes=VMEM_LIMIT),
        name="mix_cross",
    )(x, u, o, conv_w, conv_b, conv_g, conv_beta, w_out, g1, b1, w_cq, kc, vc, w_co, g2, b2)


def _swiglu_kernel(h_ref, wg_ref, wu_ref, wd_ref, g_ref, b_ref, out_ref):
    h = h_ref[...]
    hb = h.astype(BF16)
    acc = DEEPNORM_ALPHA * h
    for c in range(D_FF // FF_CHUNK):
        cols = slice(c * FF_CHUNK, (c + 1) * FF_CHUNK)
        gate = _mm(hb, wg_ref[:, cols])
        up = _mm(hb, wu_ref[:, cols])
        act = (gate * _sigmoid(gate) * up).astype(BF16)
        acc = acc + _mm(act, wd_ref[cols, :])
    out_ref[...] = _layer_norm(acc, g_ref[...], b_ref[...])


def _swiglu(h, w_gate, w_up, w_down, g, b):
    n = h.shape[0]
    const2 = lambda t: (0, 0)
    tok = pl.BlockSpec((FF_ROW_TILE, D_MODEL), lambda t: (t, 0))
    return pl.pallas_call(
        _swiglu_kernel,
        grid=(n // FF_ROW_TILE,),
        in_specs=[tok, pl.BlockSpec(w_gate.shape, const2), pl.BlockSpec(w_up.shape, const2),
                  pl.BlockSpec(w_down.shape, const2), pl.BlockSpec((1, D_MODEL), const2),
                  pl.BlockSpec((1, D_MODEL), const2)],
        out_specs=tok,
        out_shape=jax.ShapeDtypeStruct((n, D_MODEL), F32),
        compiler_params=pltpu.CompilerParams(
            dimension_semantics=("arbitrary",), vmem_limit_bytes=VMEM_LIMIT),
        name="swiglu",
    )(h, w_gate, w_up, w_down, g, b)


def _layer(h, mem, w_in, b_forget, conv_w, conv_b, conv_ln_g, conv_ln_b, w_out, ln_mix_g, ln_mix_b,
           w_cq, w_ck, w_cv, w_co, ln_cross_g, ln_cross_b, w_gate, w_up, w_down, ln_ffn_g, ln_ffn_b):
    bsz, seq, _ = h.shape
    row = lambda v: v.reshape(1, -1).astype(F32)
    w_main = w_in[:, :C_F].astype(BF16)
    w_f = jnp.pad(w_in[:, C_F:], ((0, 0), (0, LANES - FOX_HEADS))).astype(BF16)
    b_f = jnp.pad(b_forget.astype(F32), (0, LANES - FOX_HEADS)).reshape(1, LANES)
    tri = jnp.tri(ROW_TILE, dtype=BF16)

    later = [w.astype(F32) for w in (w_ck, w_cv, w_out, w_cq, w_co, w_gate, w_up, w_down)]
    (u, q_aug, k_aug, v), later = _inproj(h, w_main, w_f, b_f, tri, later)
    w_ck, w_cv, w_out, w_cq, w_co, w_gate, w_up, w_down = later
    o = _fox_attention(q_aug, k_aug, v)
    kc, vc = _mem_kv(mem, w_ck, w_cv)
    h2 = _mix_cross(h, u, o, conv_w.astype(F32), row(conv_b), row(conv_ln_g), row(conv_ln_b),
                    w_out, row(ln_mix_g), row(ln_mix_b), w_cq, kc, vc, w_co, row(ln_cross_g), row(ln_cross_b))
    h3 = _swiglu(h2.reshape(bsz * seq, D_MODEL), w_gate, w_up, w_down, row(ln_ffn_g), row(ln_ffn_b))
    return h3.reshape(bsz, seq, D_MODEL)


def kernel(x, mem, w_in, b_forget, conv_w, conv_b, conv_ln_g, conv_ln_b, w_out, ln_mix_g, ln_mix_b,
           w_cq, w_ck, w_cv, w_co, ln_cross_g, ln_cross_b, w_gate, w_up, w_down, ln_ffn_g, ln_ffn_b):
    depth = w_in.shape[0]
    assert depth == 1, "DEEPNORM_ALPHA is fixed for a single layer"
    h = x
    for l in range(depth):
        h = _layer(h, mem, w_in[l], b_forget[l], conv_w[l], conv_b[l], conv_ln_g[l], conv_ln_b[l],
                   w_out[l], ln_mix_g[l], ln_mix_b[l], w_cq[l], w_ck[l], w_cv[l], w_co[l],
                   ln_cross_g[l], ln_cross_b[l], w_gate[l], w_up[l], w_down[l], ln_ffn_g[l], ln_ffn_b[l])
    return h
```

```python
import functools
import math

import jax
import jax.numpy as jnp
from jax import lax
from jax.experimental import pallas as pl
from jax.experimental.pallas import tpu as pltpu

D_MODEL = 1024
CONV_WIDTH = 512
CONV_K = 31
FOX_WIDTH = 512
FOX_HEADS = 8
FOX_HEAD_DIM = 64
MEM_LEN = 256
MEM_HEADS = 4
MEM_HEAD_DIM = 256
D_FF = 2816
LN_EPS = 1e-5
NEG_INF = -1e30
DEEPNORM_ALPHA = 2.0 ** 0.25
LOG2E = math.log2(math.e)

LANES = 128
SUBLANES = 8
HALO = 32
ROW_TILE = 512
FF_ROW_TILE = 1024
Q_TILE = 1024
K_TILE = 1024
DIAG = Q_TILE // K_TILE
Q_SUB = 256
VT_ROWS = FOX_HEAD_DIM + 2 * SUBLANES
FF_CHUNK = 256
CONV_ROWS = 32
CHAIN_SPLIT = 2
VMEM_LIMIT = 56 * 1024 * 1024

BF16 = jnp.bfloat16
F32 = jnp.float32

C_GLU_A = 0
C_GLU_B = CONV_WIDTH
C_Q = 2 * CONV_WIDTH
C_K = C_Q + FOX_WIDTH
C_V = C_K + FOX_WIDTH
C_F = C_V + FOX_WIDTH


def _mm(a, b):
    return jnp.dot(a, b, preferred_element_type=F32)


def _mm_nt(a, b):
    return lax.dot_general(a, b, (((1,), (1,)), ((), ())), preferred_element_type=F32)


def _layer_norm(x, g, b):
    mu = jnp.mean(x, axis=-1, keepdims=True)
    xc = x - mu
    var = jnp.mean(xc * xc, axis=-1, keepdims=True)
    return xc * lax.rsqrt(var + LN_EPS) * g + b


def _sigmoid(x):
    return 1.0 / (1.0 + jnp.exp(-x))


def _split3(x):
    hi = x.astype(BF16).astype(F32)
    r = x - hi
    mid = r.astype(BF16).astype(F32)
    lo = (r - mid).astype(BF16).astype(F32)
    return hi, mid, lo


def _pack3(hi, mid, lo):
    return hi + pltpu.roll(mid, FOX_HEADS, axis=1) + pltpu.roll(lo, 2 * FOX_HEADS, axis=1)


def _inproj_kernel(x_ref, w_ref, wf_ref, bf_ref, tri_ref, *refs, n_cast):
    cast_in, (u_ref, q_ref, k_ref, v_ref), cast_out = refs[:n_cast], refs[n_cast:n_cast + 4], refs[n_cast + 4:-1]
    carry = refs[-1]
    t = pl.program_id(1)
    rows = x_ref.shape[1]
    xb = x_ref[0].astype(BF16)

    @pl.when(t == 0)
    def _():
        carry[...] = jnp.zeros_like(carry)

    glu_a = _mm(xb, w_ref[:, C_GLU_A:C_GLU_A + CONV_WIDTH])
    glu_b = _mm(xb, w_ref[:, C_GLU_B:C_GLU_B + CONV_WIDTH])
    u_ref[0] = glu_a * _sigmoid(glu_b)

    lane = lax.broadcasted_iota(jnp.int32, (rows, LANES), 1)
    f = _mm(xb, wf_ref[...]) + bf_ref[...]
    logf = jnp.minimum(f, 0.0) - jnp.log(1.0 + jnp.exp(-jnp.abs(f)))
    logf = jnp.where(lane < FOX_HEADS, logf, 0.0)
    packed = _pack3(*_split3(logf)).astype(BF16)
    res = _mm(tri_ref[...], packed)
    cum = res + pltpu.roll(res, LANES - FOX_HEADS, axis=1) + pltpu.roll(res, LANES - 2 * FOX_HEADS, axis=1)
    cum = jnp.where(lane < FOX_HEADS, cum + carry[0:1, :], 0.0)
    carry[0:1, :] = cum[rows - 1:rows, :]
    cpack = _pack3(*_split3(cum * LOG2E))

    scale = LOG2E / math.sqrt(FOX_HEAD_DIM)
    for hp in range(FOX_HEADS // 2):
        if hp % 2 == 0:
            qquad = _mm(xb, w_ref[:, C_Q + hp * LANES:C_Q + (hp + 2) * LANES]) * scale
            kquad = _mm(xb, w_ref[:, C_K + hp * LANES:C_K + (hp + 2) * LANES])
        qpair = qquad[:, (hp % 2) * LANES:(hp % 2 + 1) * LANES]
        kpair = kquad[:, (hp % 2) * LANES:(hp % 2 + 1) * LANES]
        for sub in range(2):
            h = 2 * hp + sub
            aug0 = FOX_HEAD_DIM if sub == 0 else 0
            data = (lane < FOX_HEAD_DIM) if sub == 0 else (lane >= FOX_HEAD_DIM)
            slot_a = (lane == aug0) | (lane == aug0 + FOX_HEADS) | (lane == aug0 + 2 * FOX_HEADS)
            slot_b = (lane == aug0 + 1) | (lane == aug0 + FOX_HEADS + 1) | (lane == aug0 + 2 * FOX_HEADS + 1)
            ra = pltpu.roll(cpack, (aug0 - h) % LANES, axis=1)
            rb = pltpu.roll(cpack, (aug0 + 1 - h) % LANES, axis=1)
            qa = jnp.where(data, qpair, jnp.where(slot_a, ra, jnp.where(slot_b, 1.0, 0.0)))
            ka = jnp.where(data, kpair, jnp.where(slot_a, 1.0, jnp.where(slot_b, -rb, 0.0)))
            q_ref[0, h] = qa.astype(BF16)
            k_ref[0, h] = ka.astype(BF16)
    v_ref[0] = _mm(xb, w_ref[:, C_V:C_V + FOX_WIDTH]).astype(BF16)

    for src_ref, dst_ref in zip(cast_in, cast_out):
        dst_ref[...] = src_ref[...].astype(BF16)


def _slab_spec(rows, cols, steps, steps_per_seq):
    nslabs = steps
    while rows % nslabs or (rows // nslabs) % (2 * SUBLANES):
        nslabs //= 2
    hold = steps // nslabs
    return pl.BlockSpec((rows // nslabs, cols), lambda b, t: ((b * steps_per_seq + t) // hold, 0))


def _inproj(x, w_main, w_f, b_f, tri, to_cast):
    bsz, seq, _ = x.shape
    nt = seq // ROW_TILE
    const2 = lambda b, t: (0, 0)
    slabs = [_slab_spec(w.shape[0], w.shape[1], bsz * nt, nt) for w in to_cast]
    outs = pl.pallas_call(
        functools.partial(_inproj_kernel, n_cast=len(to_cast)),
        grid=(bsz, nt),
        in_specs=[
            pl.BlockSpec((1, ROW_TILE, D_MODEL), lambda b, t: (b, t, 0)),
            pl.BlockSpec(w_main.shape, const2),
            pl.BlockSpec(w_f.shape, const2),
            pl.BlockSpec(b_f.shape, const2),
            pl.BlockSpec(tri.shape, const2),
        ] + slabs,
        out_specs=[
            pl.BlockSpec((1, ROW_TILE, CONV_WIDTH), lambda b, t: (b, t, 0)),
            pl.BlockSpec((1, FOX_HEADS, ROW_TILE, LANES), lambda b, t: (b, 0, t, 0)),
            pl.BlockSpec((1, FOX_HEADS, ROW_TILE, LANES), lambda b, t: (b, 0, t, 0)),
            pl.BlockSpec((1, ROW_TILE, FOX_WIDTH), lambda b, t: (b, t, 0)),
        ] + slabs,
        out_shape=[
            jax.ShapeDtypeStruct((bsz, seq, CONV_WIDTH), F32),
            jax.ShapeDtypeStruct((bsz, FOX_HEADS, seq, LANES), BF16),
            jax.ShapeDtypeStruct((bsz, FOX_HEADS, seq, LANES), BF16),
            jax.ShapeDtypeStruct((bsz, seq, FOX_WIDTH), BF16),
        ] + [jax.ShapeDtypeStruct(w.shape, BF16) for w in to_cast],
        scratch_shapes=[pltpu.VMEM((SUBLANES, LANES), F32)],
        compiler_params=pltpu.CompilerParams(
            dimension_semantics=("arbitrary", "arbitrary"), vmem_limit_bytes=VMEM_LIMIT),
        name="inproj",
    )(x, w_main, w_f, b_f, tri, *to_cast)
    return outs[:4], outs[4:]


def _fox_kernel(q_ref, qn_ref, k_ref, v_ref, o_ref, vt_scr, qt_scr, qtn_scr, acc_scr, s_scr, m_scr):
    i = pl.program_id(1)
    nq = pl.num_programs(1)
    nk = vt_scr.shape[0]
    npairs = FOX_HEADS // 2
    units = [(h, qb) for h in range(2) for qb in range(Q_TILE // Q_SUB)]
    first_diag = DIAG * i

    def band_keys(qb, band):
        if band is None:
            return K_TILE
        return max(0, min(K_TILE, (qb + 1) * Q_SUB - band * K_TILE))

    def transpose_q(src_ref, head0, dst):
        for h in range(2):
            dst[h] = src_ref[0, head0 + h].astype(F32).T.astype(BF16)

    def score_unit(pair, j, u, band, fresh, qt):
        h, qb = units[u]
        nkeys = band_keys(qb, band)
        if nkeys == 0:
            return
        kt = k_ref[0, 2 * pair + h, pl.ds(pl.multiple_of(j * K_TILE, K_TILE), nkeys), :]
        st = _mm(kt, qt[h, :, qb * Q_SUB:(qb + 1) * Q_SUB])
        if band is not None and band * K_TILE + nkeys - 1 > qb * Q_SUB:
            kpos = lax.broadcasted_iota(jnp.int32, st.shape, 0) + band * K_TILE
            qpos = lax.broadcasted_iota(jnp.int32, st.shape, 1) + qb * Q_SUB
            st = jnp.where(kpos <= qpos, st, NEG_INF)
        s_scr[u, 0:nkeys, :] = st
        m_prev = jnp.full((1, Q_SUB), NEG_INF, F32) if fresh else m_scr[u, 1]
        m_scr[u, 0] = m_prev
        m_scr[u, 1] = jnp.maximum(m_prev, jnp.max(st, axis=0, keepdims=True))

    def value_unit(pair, j, u, band):
        h, qb = units[u]
        nkeys = band_keys(qb, band)
        if nkeys == 0:
            return
        m_new = m_scr[u, 1]
        alpha = jnp.exp2(m_scr[u, 0] - m_new)
        pt = jnp.exp2((s_scr[u, 0:nkeys, :] - m_new).astype(BF16))
        acc_scr[u] = alpha * acc_scr[u] + _mm(vt_scr[j, 2 * pair + h, :, 0:nkeys], pt)

    def stages(value=None, score=None, value_band=None, score_band=None, fresh=False, qt=None):
        for u in range(len(units)):
            if value is not None:
                value_unit(value[0], value[1], u, value_band)
            if score is not None:
                score_unit(score[0], score[1], u, score_band, fresh, qt_scr.at[score[0]] if qt is None else qt)

    @pl.when(i == 0)
    def _():
        tail = (lax.broadcasted_iota(jnp.int32, (VT_ROWS - FOX_HEAD_DIM, K_TILE), 0) == 0).astype(BF16)
        for c in range(nk):
            for pair in range(npairs):
                blk = v_ref[0, c * K_TILE:(c + 1) * K_TILE, pair * LANES:(pair + 1) * LANES]
                blk = blk.astype(F32).T.astype(BF16)
                for h in range(2):
                    vt_scr[c, 2 * pair + h, 0:FOX_HEAD_DIM, :] = blk[h * FOX_HEAD_DIM:(h + 1) * FOX_HEAD_DIM, :]
                    vt_scr[c, 2 * pair + h, FOX_HEAD_DIM:VT_ROWS, :] = tail

    for pair in range(npairs):
        transpose_q(q_ref, 2 * pair, qt_scr.at[pair])

    @pl.when(i == 0)
    def _():
        stages(score=(0, 0), score_band=0, fresh=True)

    def one_pair(pair, carry):
        acc_scr[...] = jnp.zeros(acc_scr.shape, F32)

        def trip(t, c):
            stages(value=(pair, t), score=(pair, t + 1))
            return c

        lax.fori_loop(0, first_diag - 1, trip, 0)

        @pl.when(i > 0)
        def _():
            stages(value=(pair, first_diag - 1), score=(pair, first_diag), score_band=0)

        for band in range(DIAG - 1):
            stages(value=(pair, first_diag + band), score=(pair, first_diag + band + 1),
                   value_band=band, score_band=band + 1)

        last = (pair, first_diag + DIAG - 1)
        more_pairs = pair + 1 < npairs

        @pl.when(more_pairs & (i == 0))
        def _():
            stages(value=last, score=(pair + 1, 0), value_band=DIAG - 1, score_band=0, fresh=True)

        @pl.when(more_pairs & (i > 0))
        def _():
            stages(value=last, score=(pair + 1, 0), value_band=DIAG - 1, fresh=True)

        @pl.when(jnp.logical_not(more_pairs) & (i < nq - 1))
        def _():
            transpose_q(qn_ref, 0, qtn_scr)
            stages(value=last, score=(0, 0), value_band=DIAG - 1, fresh=True, qt=qtn_scr)

        @pl.when(jnp.logical_not(more_pairs) & (i == nq - 1))
        def _():
            stages(value=last, value_band=DIAG - 1)

        rows = []
        for h in range(2):
            blocks = [acc_scr[u] for u in range(len(units)) if units[u][0] == h]
            rows.append(jnp.concatenate(
                [a[0:FOX_HEAD_DIM] / a[FOX_HEAD_DIM:FOX_HEAD_DIM + 1] for a in blocks], axis=1))
        o_ref[0, pair] = jnp.concatenate(rows, axis=0).T.astype(BF16)
        return carry

    lax.fori_loop(0, npairs, one_pair, 0)


def _fox_attention(q_aug, k_aug, v):
    bsz, _, seq, _ = q_aug.shape
    nq = seq // Q_TILE
    nk = seq // K_TILE
    return pl.pallas_call(
        _fox_kernel,
        grid=(bsz, nq),
        in_specs=[
            pl.BlockSpec((1, FOX_HEADS, Q_TILE, LANES), lambda b, i: (b, 0, i, 0)),
            pl.BlockSpec((1, 2, Q_TILE, LANES), lambda b, i: (b, 0, jnp.minimum(i + 1, nq - 1), 0)),
            pl.BlockSpec((1, FOX_HEADS, seq, LANES), lambda b, i: (b, 0, 0, 0)),
            pl.BlockSpec((1, seq, FOX_WIDTH), lambda b, i: (b, 0, 0)),
        ],
        out_specs=pl.BlockSpec((1, FOX_HEADS // 2, Q_TILE, LANES), lambda b, i: (b, 0, i, 0)),
        out_shape=jax.ShapeDtypeStruct((bsz, FOX_HEADS // 2, seq, LANES), BF16),
        scratch_shapes=[
            pltpu.VMEM((nk, FOX_HEADS, VT_ROWS, K_TILE), BF16),
            pltpu.VMEM((FOX_HEADS // 2, 2, LANES, Q_TILE), BF16),
            pltpu.VMEM((2, LANES, Q_TILE), BF16),
            pltpu.VMEM((2 * (Q_TILE // Q_SUB), VT_ROWS, Q_SUB), F32),
            pltpu.VMEM((2 * (Q_TILE // Q_SUB), K_TILE, Q_SUB), F32),
            pltpu.VMEM((2 * (Q_TILE // Q_SUB), 2, 1, Q_SUB), F32),
        ],
        compiler_params=pltpu.CompilerParams(
            dimension_semantics=("arbitrary", "arbitrary"), vmem_limit_bytes=VMEM_LIMIT),
        name="fox_attention",
    )(q_aug, q_aug, k_aug, v)


def _memkv_kernel(mem_ref, wk_ref, wv_ref, k_ref, v_ref):
    mb = mem_ref[0].astype(BF16)
    k_ref[0] = _mm(mb, wk_ref[...]).astype(BF16)
    v_ref[0] = _mm(mb, wv_ref[...]).astype(BF16)


def _mem_kv(mem, w_ck, w_cv):
    bsz = mem.shape[0]
    const2 = lambda b: (0, 0)
    blk = pl.BlockSpec((1, MEM_LEN, D_MODEL), lambda b: (b, 0, 0))
    return pl.pallas_call(
        _memkv_kernel,
        grid=(bsz,),
        in_specs=[blk, pl.BlockSpec(w_ck.shape, const2), pl.BlockSpec(w_cv.shape, const2)],
        out_specs=[blk, blk],
        out_shape=[jax.ShapeDtypeStruct((bsz, MEM_LEN, D_MODEL), BF16)] * 2,
        compiler_params=pltpu.CompilerParams(
            dimension_semantics=("arbitrary",), vmem_limit_bytes=VMEM_LIMIT),
        name="mem_kv",
    )(mem, w_ck, w_cv)


def _conv_branch(u_ref, cw_ref, cb_ref, cg_ref, cbeta_ref, ubuf, shifted):
    t = pl.program_id(1)
    rows = u_ref.shape[1]
    first = HALO - (CONV_K - 1)
    span = shifted.shape[1]

    @pl.when(t == 0)
    def _():
        ubuf[0:HALO, :] = jnp.zeros((HALO, CONV_WIDTH), F32)

    @pl.when(t > 0)
    def _():
        ubuf[0:HALO, :] = ubuf[rows:rows + HALO, :]

    ubuf[HALO:HALO + rows, :] = u_ref[0]
    for p in range(1, SUBLANES):
        shifted[p - 1] = ubuf[p:p + span, :]

    cw = cw_ref[...]
    cbias = cb_ref[...]
    gam = cg_ref[...]
    beta = cbeta_ref[...]
    outs = []
    for r in range(rows // CONV_ROWS):
        base = r * CONV_ROWS
        acc = jnp.broadcast_to(cbias, (CONV_ROWS, CONV_WIDTH))
        for j in range(CONV_K):
            a, p = divmod(first + j, SUBLANES)
            lo = base + SUBLANES * a
            src = ubuf[lo:lo + CONV_ROWS, :] if p == 0 else shifted[p - 1, lo:lo + CONV_ROWS, :]
            acc = acc + cw[j:j + 1, :] * src
        y = _layer_norm(acc, gam, beta)
        outs.append((y * _sigmoid(y)).astype(BF16))
    return jnp.concatenate(outs, axis=0)


def _mix_cross_kernel(x_ref, u_ref, o_ref, cw_ref, cb_ref, cg_ref, cbeta_ref, wout_ref, g1_ref, b1_ref,
                      wcq_ref, kc_ref, vc_ref, wco_ref, g2_ref, b2_ref, h_ref, ubuf, shifted):
    uc = _conv_branch(u_ref, cw_ref, cb_ref, cg_ref, cbeta_ref, ubuf, shifted)
    o_fox = jnp.concatenate([o_ref[0, p] for p in range(o_ref.shape[1])], axis=1)
    rows = x_ref.shape[1]
    groups = [slice(g * rows // CHAIN_SPLIT, (g + 1) * rows // CHAIN_SPLIT) for g in range(CHAIN_SPLIT)]
    mix = [_mm(uc[r], wout_ref[0:CONV_WIDTH, :]) + _mm(o_fox[r], wout_ref[CONV_WIDTH:, :]) for r in groups]
    h1, q = [], []
    for g, r in enumerate(groups):
        h1.append(_layer_norm(DEEPNORM_ALPHA * x_ref[0, r, :] + mix[g], g1_ref[...], b1_ref[...]))
        q.append((_mm(h1[g].astype(BF16), wcq_ref[...]) * (1.0 / math.sqrt(MEM_HEAD_DIM))).astype(BF16))
    o = []
    for g in range(CHAIN_SPLIT):
        outs = []
        for hh in range(MEM_HEADS):
            cols = slice(hh * MEM_HEAD_DIM, (hh + 1) * MEM_HEAD_DIM)
            s = _mm_nt(q[g][:, cols], kc_ref[0, :, cols])
            p = jnp.exp(s - jnp.max(s, axis=-1, keepdims=True))
            l = jnp.sum(p, axis=-1, keepdims=True)
            outs.append((_mm(p.astype(BF16), vc_ref[0, :, cols]) / l).astype(BF16))
        o.append(jnp.concatenate(outs, axis=-1))
    y = [_mm(o[g], wco_ref[...]) for g in range(CHAIN_SPLIT)]
    for g, r in enumerate(groups):
        h_ref[0, r, :] = _layer_norm(DEEPNORM_ALPHA * h1[g] + y[g], g2_ref[...], b2_ref[...])


def _mix_cross(x, u, o, conv_w, conv_b, conv_g, conv_beta, w_out, g1, b1, w_cq, kc, vc, w_co, g2, b2):
    bsz, seq, _ = x.shape
    nt = seq // ROW_TILE
    const2 = lambda b, t: (0, 0)
    tok = lambda width: pl.BlockSpec((1, ROW_TILE, width), lambda b, t: (b, t, 0))
    memblk = pl.BlockSpec((1, MEM_LEN, D_MODEL), lambda b, t: (b, 0, 0))
    vec = pl.BlockSpec((1, D_MODEL), const2)
    cvec = pl.BlockSpec((1, CONV_WIDTH), const2)
    fox = pl.BlockSpec((1, FOX_HEADS // 2, ROW_TILE, LANES), lambda b, t: (b, 0, t, 0))
    mat = pl.BlockSpec((D_MODEL, D_MODEL), const2)
    return pl.pallas_call(
        _mix_cross_kernel,
        grid=(bsz, nt),
        in_specs=[tok(D_MODEL), tok(CONV_WIDTH), fox, pl.BlockSpec(conv_w.shape, const2), cvec, cvec,
                  cvec, mat, vec, vec, mat, memblk, memblk, mat, vec, vec],
        out_specs=tok(D_MODEL),
        out_shape=jax.ShapeDtypeStruct((bsz, seq, D_MODEL), F32),
        scratch_shapes=[
            pltpu.VMEM((ROW_TILE + HALO, CONV_WIDTH), F32),
            pltpu.VMEM((SUBLANES - 1, ROW_TILE + HALO - SUBLANES, CONV_WIDTH), F32),
        ],
        compiler_params=pltpu.CompilerParams(
            dimension_semantics=("arbitrary", "arbitrary"), vmem_limit_bytes=VMEM_LIMIT),
        name="mix_cross",
    )(x, u, o, conv_w, conv_b, conv_g, conv_beta, w_out, g1, b1, w_cq, kc, vc, w_co, g2, b2)


def _swiglu_kernel(h_ref, wg_ref, wu_ref, wd_ref, g_ref, b_ref, out_ref):
    h = h_ref[...]
    hb = h.astype(BF16)
    acc = DEEPNORM_ALPHA * h
    for c in range(D_FF // FF_CHUNK):
        cols = slice(c * FF_CHUNK, (c + 1) * FF_CHUNK)
        gate = _mm(hb, wg_ref[:, cols])
        up = _mm(hb, wu_ref[:, cols])
        act = (gate * _sigmoid(gate) * up).astype(BF16)
        acc = acc + _mm(act, wd_ref[cols, :])
    out_ref[...] = _layer_norm(acc, g_ref[...], b_ref[...])


def _swiglu(h, w_gate, w_up, w_down, g, b):
    n = h.shape[0]
    const2 = lambda t: (0, 0)
    tok = pl.BlockSpec((FF_ROW_TILE, D_MODEL), lambda t: (t, 0))
    return pl.pallas_call(
        _swiglu_kernel,
        grid=(n // FF_ROW_TILE,),
        in_specs=[tok, pl.BlockSpec(w_gate.shape, const2), pl.BlockSpec(w_up.shape, const2),
                  pl.BlockSpec(w_down.shape, const2), pl.BlockSpec((1, D_MODEL), const2),
                  pl.BlockSpec((1, D_MODEL), const2)],
        out_specs=tok,
        out_shape=jax.ShapeDtypeStruct((n, D_MODEL), F32),
        compiler_params=pltpu.CompilerParams(
            dimension_semantics=("arbitrary",), vmem_limit_bytes=VMEM_LIMIT),
        name="swiglu",
    )(h, w_gate, w_up, w_down, g, b)


def _layer(h, mem, w_in, b_forget, conv_w, conv_b, conv_ln_g, conv_ln_b, w_out, ln_mix_g, ln_mix_b,
           w_cq, w_ck, w_cv, w_co, ln_cross_g, ln_cross_b, w_gate, w_up, w_down, ln_ffn_g, ln_ffn_b):
    bsz, seq, _ = h.shape
    row = lambda v: v.reshape(1, -1).astype(F32)
    w_main = w_in[:, :C_F].astype(BF16)
    w_f = jnp.pad(w_in[:, C_F:], ((0, 0), (0, LANES - FOX_HEADS))).astype(BF16)
    b_f = jnp.pad(b_forget.astype(F32), (0, LANES - FOX_HEADS)).reshape(1, LANES)
    tri = jnp.tri(ROW_TILE, dtype=BF16)

    later = [w.astype(F32) for w in (w_ck, w_cv, w_out, w_cq, w_co, w_gate, w_up, w_down)]
    (u, q_aug, k_aug, v), later = _inproj(h, w_main, w_f, b_f, tri, later)
    w_ck, w_cv, w_out, w_cq, w_co, w_gate, w_up, w_down = later
    o = _fox_attention(q_aug, k_aug, v)
    kc, vc = _mem_kv(mem, w_ck, w_cv)
    h2 = _mix_cross(h, u, o, conv_w.astype(F32), row(conv_b), row(conv_ln_g), row(conv_ln_b),
                    w_out, row(ln_mix_g), row(ln_mix_b), w_cq, kc, vc, w_co, row(ln_cross_g), row(ln_cross_b))
    h3 = _swiglu(h2.reshape(bsz * seq, D_MODEL), w_gate, w_up, w_down, row(ln_ffn_g), row(ln_ffn_b))
    return h3.reshape(bsz, seq, D_MODEL)


def kernel(x, mem, w_in, b_forget, conv_w, conv_b, conv_ln_g, conv_ln_b, w_out, ln_mix_g, ln_mix_b,
           w_cq, w_ck, w_cv, w_co, ln_cross_g, ln_cross_b, w_gate, w_up, w_down, ln_ffn_g, ln_ffn_b):
    depth = w_in.shape[0]
    assert depth == 1, "DEEPNORM_ALPHA is fixed for a single layer"
    h = x
    for l in range(depth):
        h = _layer(h, mem, w_in[l], b_forget[l], conv_w[l], conv_b[l], conv_ln_g[l], conv_ln_b[l],
                   w_out[l], ln_mix_g[l], ln_mix_b[l], w_cq[l], w_ck[l], w_cv[l], w_co[l],
                   ln_cross_g[l], ln_cross_b[l], w_gate[l], w_up[l], w_down[l], ln_ffn_g[l], ln_ffn_b[l])
    return h
```

```python
import functools
import math

import jax
import jax.numpy as jnp
from jax import lax
from jax.experimental import pallas as pl
from jax.experimental.pallas import tpu as pltpu

D_MODEL = 1024
CONV_WIDTH = 512
CONV_K = 31
FOX_WIDTH = 512
FOX_HEADS = 8
FOX_HEAD_DIM = 64
MEM_LEN = 256
MEM_HEADS = 4
MEM_HEAD_DIM = 256
D_FF = 2816
LN_EPS = 1e-5
NEG_INF = -1e30
DEEPNORM_ALPHA = 2.0 ** 0.25
LOG2E = math.log2(math.e)

LANES = 128
SUBLANES = 8
HALO = 32
ROW_TILE = 512
FF_ROW_TILE = 1024
Q_TILE = 1024
K_TILE = 1024
DIAG = Q_TILE // K_TILE
Q_SUB = 256
VT_ROWS = FOX_HEAD_DIM + 2 * SUBLANES
FF_CHUNK = 256
CONV_ROWS = 32
CHAIN_SPLIT = 2
VMEM_LIMIT = 56 * 1024 * 1024

BF16 = jnp.bfloat16
F32 = jnp.float32

C_GLU_A = 0
C_GLU_B = CONV_WIDTH
C_Q = 2 * CONV_WIDTH
C_K = C_Q + FOX_WIDTH
C_V = C_K + FOX_WIDTH
C_F = C_V + FOX_WIDTH


def _mm(a, b):
    return jnp.dot(a, b, preferred_element_type=F32)


def _mm_nt(a, b):
    return lax.dot_general(a, b, (((1,), (1,)), ((), ())), preferred_element_type=F32)


def _layer_norm(x, g, b):
    mu = jnp.mean(x, axis=-1, keepdims=True)
    xc = x - mu
    var = jnp.mean(xc * xc, axis=-1, keepdims=True)
    return xc * lax.rsqrt(var + LN_EPS) * g + b


def _sigmoid(x):
    return 1.0 / (1.0 + jnp.exp(-x))


def _split3(x):
    hi = x.astype(BF16).astype(F32)
    r = x - hi
    mid = r.astype(BF16).astype(F32)
    lo = (r - mid).astype(BF16).astype(F32)
    return hi, mid, lo


def _pack3(hi, mid, lo):
    return hi + pltpu.roll(mid, FOX_HEADS, axis=1) + pltpu.roll(lo, 2 * FOX_HEADS, axis=1)


def _inproj_kernel(x_ref, win_ref, wf_ref, bf_ref, tri_ref, *refs, n_cast):
    cast_in, (u_ref, q_ref, k_ref, v_ref), cast_out = refs[:n_cast], refs[n_cast:n_cast + 4], refs[n_cast + 4:-2]
    carry, w_ref = refs[-2:]
    t = pl.program_id(1)
    rows = x_ref.shape[1]
    xb = x_ref[0].astype(BF16)

    @pl.when((pl.program_id(0) == 0) & (t == 0))
    def _():
        w_ref[...] = win_ref[:, 0:C_F].astype(BF16)

    @pl.when(t == 0)
    def _():
        carry[...] = jnp.zeros_like(carry)

    glu_a = _mm(xb, w_ref[:, C_GLU_A:C_GLU_A + CONV_WIDTH])
    glu_b = _mm(xb, w_ref[:, C_GLU_B:C_GLU_B + CONV_WIDTH])
    u_ref[0] = glu_a * _sigmoid(glu_b)

    lane = lax.broadcasted_iota(jnp.int32, (rows, LANES), 1)
    f = _mm(xb, wf_ref[...]) + bf_ref[...]
    logf = jnp.minimum(f, 0.0) - jnp.log(1.0 + jnp.exp(-jnp.abs(f)))
    logf = jnp.where(lane < FOX_HEADS, logf, 0.0)
    packed = _pack3(*_split3(logf)).astype(BF16)
    res = _mm(tri_ref[...], packed)
    cum = res + pltpu.roll(res, LANES - FOX_HEADS, axis=1) + pltpu.roll(res, LANES - 2 * FOX_HEADS, axis=1)
    cum = jnp.where(lane < FOX_HEADS, cum + carry[0:1, :], 0.0)
    carry[0:1, :] = cum[rows - 1:rows, :]
    cpack = _pack3(*_split3(cum * LOG2E))

    scale = LOG2E / math.sqrt(FOX_HEAD_DIM)
    for hp in range(FOX_HEADS // 2):
        if hp % 2 == 0:
            qquad = _mm(xb, w_ref[:, C_Q + hp * LANES:C_Q + (hp + 2) * LANES]) * scale
            kquad = _mm(xb, w_ref[:, C_K + hp * LANES:C_K + (hp + 2) * LANES])
        qpair = qquad[:, (hp % 2) * LANES:(hp % 2 + 1) * LANES]
        kpair = kquad[:, (hp % 2) * LANES:(hp % 2 + 1) * LANES]
        for sub in range(2):
            h = 2 * hp + sub
            aug0 = FOX_HEAD_DIM if sub == 0 else 0
            data = (lane < FOX_HEAD_DIM) if sub == 0 else (lane >= FOX_HEAD_DIM)
            slot_a = (lane == aug0) | (lane == aug0 + FOX_HEADS) | (lane == aug0 + 2 * FOX_HEADS)
            slot_b = (lane == aug0 + 1) | (lane == aug0 + FOX_HEADS + 1) | (lane == aug0 + 2 * FOX_HEADS + 1)
            ra = pltpu.roll(cpack, (aug0 - h) % LANES, axis=1)
            rb = pltpu.roll(cpack, (aug0 + 1 - h) % LANES, axis=1)
            qa = jnp.where(data, qpair, jnp.where(slot_a, ra, jnp.where(slot_b, 1.0, 0.0)))
            ka = jnp.where(data, kpair, jnp.where(slot_a, 1.0, jnp.where(slot_b, -rb, 0.0)))
            q_ref[0, h] = qa.astype(BF16)
            k_ref[0, h] = ka.astype(BF16)
    v_ref[0] = _mm(xb, w_ref[:, C_V:C_V + FOX_WIDTH]).astype(BF16)

    for src_ref, dst_ref in zip(cast_in, cast_out):
        dst_ref[...] = src_ref[...].astype(BF16)


def _slab_spec(rows, cols, steps, steps_per_seq):
    nslabs = steps
    while rows % nslabs or (rows // nslabs) % (2 * SUBLANES):
        nslabs //= 2
    hold = steps // nslabs
    return pl.BlockSpec((rows // nslabs, cols), lambda b, t: ((b * steps_per_seq + t) // hold, 0))


def _inproj(x, w_in, w_f, b_f, tri, to_cast):
    bsz, seq, _ = x.shape
    nt = seq // ROW_TILE
    const2 = lambda b, t: (0, 0)
    slabs = [_slab_spec(w.shape[0], w.shape[1], bsz * nt, nt) for w in to_cast]
    outs = pl.pallas_call(
        functools.partial(_inproj_kernel, n_cast=len(to_cast)),
        grid=(bsz, nt),
        in_specs=[
            pl.BlockSpec((1, ROW_TILE, D_MODEL), lambda b, t: (b, t, 0)),
            pl.BlockSpec(w_in.shape, const2),
            pl.BlockSpec(w_f.shape, const2),
            pl.BlockSpec(b_f.shape, const2),
            pl.BlockSpec(tri.shape, const2),
        ] + slabs,
        out_specs=[
            pl.BlockSpec((1, ROW_TILE, CONV_WIDTH), lambda b, t: (b, t, 0)),
            pl.BlockSpec((1, FOX_HEADS, ROW_TILE, LANES), lambda b, t: (b, 0, t, 0)),
            pl.BlockSpec((1, FOX_HEADS, ROW_TILE, LANES), lambda b, t: (b, 0, t, 0)),
            pl.BlockSpec((1, ROW_TILE, FOX_WIDTH), lambda b, t: (b, t, 0)),
        ] + slabs,
        out_shape=[
            jax.ShapeDtypeStruct((bsz, seq, CONV_WIDTH), F32),
            jax.ShapeDtypeStruct((bsz, FOX_HEADS, seq, LANES), BF16),
            jax.ShapeDtypeStruct((bsz, FOX_HEADS, seq, LANES), BF16),
            jax.ShapeDtypeStruct((bsz, seq, FOX_WIDTH), BF16),
        ] + [jax.ShapeDtypeStruct(w.shape, BF16) for w in to_cast],
        scratch_shapes=[pltpu.VMEM((SUBLANES, LANES), F32), pltpu.VMEM((D_MODEL, C_F), BF16)],
        compiler_params=pltpu.CompilerParams(
            dimension_semantics=("arbitrary", "arbitrary"), vmem_limit_bytes=VMEM_LIMIT),
        name="inproj",
    )(x, w_in, w_f, b_f, tri, *to_cast)
    return outs[:4], outs[4:]


def _fox_kernel(q_ref, qn_ref, k_ref, v_ref, o_ref, vt_scr, qt_scr, qtn_scr, acc_scr, s_scr, m_scr):
    i = pl.program_id(1)
    nq = pl.num_programs(1)
    nk = vt_scr.shape[0]
    npairs = FOX_HEADS // 2
    units = [(h, qb) for h in range(2) for qb in range(Q_TILE // Q_SUB)]
    first_diag = DIAG * i

    def band_keys(qb, band):
        if band is None:
            return K_TILE
        return max(0, min(K_TILE, (qb + 1) * Q_SUB - band * K_TILE))

    def transpose_q(src_ref, head0, dst):
        for h in range(2):
            dst[h] = src_ref[0, head0 + h].astype(F32).T.astype(BF16)

    def score_unit(pair, j, u, band, fresh, qt):
        h, qb = units[u]
        nkeys = band_keys(qb, band)
        if nkeys == 0:
            return
        kt = k_ref[0, 2 * pair + h, pl.ds(pl.multiple_of(j * K_TILE, K_TILE), nkeys), :]
        st = _mm(kt, qt[h, :, qb * Q_SUB:(qb + 1) * Q_SUB])
        if band is not None and band * K_TILE + nkeys - 1 > qb * Q_SUB:
            kpos = lax.broadcasted_iota(jnp.int32, st.shape, 0) + band * K_TILE
            qpos = lax.broadcasted_iota(jnp.int32, st.shape, 1) + qb * Q_SUB
            st = jnp.where(kpos <= qpos, st, NEG_INF)
        s_scr[u, 0:nkeys, :] = st
        m_prev = jnp.full((1, Q_SUB), NEG_INF, F32) if fresh else m_scr[u, 1]
        m_scr[u, 0] = m_prev
        m_scr[u, 1] = jnp.maximum(m_prev, jnp.max(st, axis=0, keepdims=True))

    def value_unit(pair, j, u, band):
        h, qb = units[u]
        nkeys = band_keys(qb, band)
        if nkeys == 0:
            return
        m_new = m_scr[u, 1]
        alpha = jnp.exp2(m_scr[u, 0] - m_new)
        pt = jnp.exp2((s_scr[u, 0:nkeys, :] - m_new).astype(BF16))
        acc_scr[u] = alpha * acc_scr[u] + _mm(vt_scr[j, 2 * pair + h, :, 0:nkeys], pt)

    def stages(value=None, score=None, value_band=None, score_band=None, fresh=False, qt=None):
        for u in range(len(units)):
            if value is not None:
                value_unit(value[0], value[1], u, value_band)
            if score is not None:
                score_unit(score[0], score[1], u, score_band, fresh, qt_scr.at[score[0]] if qt is None else qt)

    @pl.when(i == 0)
    def _():
        tail = (lax.broadcasted_iota(jnp.int32, (VT_ROWS - FOX_HEAD_DIM, K_TILE), 0) == 0).astype(BF16)
        for c in range(nk):
            for pair in range(npairs):
                blk = v_ref[0, c * K_TILE:(c + 1) * K_TILE, pair * LANES:(pair + 1) * LANES]
                blk = blk.astype(F32).T.astype(BF16)
                for h in range(2):
                    vt_scr[c, 2 * pair + h, 0:FOX_HEAD_DIM, :] = blk[h * FOX_HEAD_DIM:(h + 1) * FOX_HEAD_DIM, :]
                    vt_scr[c, 2 * pair + h, FOX_HEAD_DIM:VT_ROWS, :] = tail

    for pair in range(npairs):
        transpose_q(q_ref, 2 * pair, qt_scr.at[pair])

    @pl.when(i == 0)
    def _():
        stages(score=(0, 0), score_band=0, fresh=True)

    def one_pair(pair, carry):
        acc_scr[...] = jnp.zeros(acc_scr.shape, F32)

        def trip(t, c):
            stages(value=(pair, t), score=(pair, t + 1))
            return c

        lax.fori_loop(0, first_diag - 1, trip, 0)

        @pl.when(i > 0)
        def _():
            stages(value=(pair, first_diag - 1), score=(pair, first_diag), score_band=0)

        for band in range(DIAG - 1):
            stages(value=(pair, first_diag + band), score=(pair, first_diag + band + 1),
                   value_band=band, score_band=band + 1)

        last = (pair, first_diag + DIAG - 1)
        more_pairs = pair + 1 < npairs

        @pl.when(more_pairs & (i == 0))
        def _():
            stages(value=last, score=(pair + 1, 0), value_band=DIAG - 1, score_band=0, fresh=True)

        @pl.when(more_pairs & (i > 0))
        def _():
            stages(value=last, score=(pair + 1, 0), value_band=DIAG - 1, fresh=True)

        @pl.when(jnp.logical_not(more_pairs) & (i < nq - 1))
        def _():
            transpose_q(qn_ref, 0, qtn_scr)
            stages(value=last, score=(0, 0), value_band=DIAG - 1, fresh=True, qt=qtn_scr)

        @pl.when(jnp.logical_not(more_pairs) & (i == nq - 1))
        def _():
            stages(value=last, value_band=DIAG - 1)

        rows = []
        for h in range(2):
            blocks = [acc_scr[u] for u in range(len(units)) if units[u][0] == h]
            rows.append(jnp.concatenate(
                [a[0:FOX_HEAD_DIM] / a[FOX_HEAD_DIM:FOX_HEAD_DIM + 1] for a in blocks], axis=1))
        o_ref[0, pair] = jnp.concatenate(rows, axis=0).T.astype(BF16)
        return carry

    lax.fori_loop(0, npairs, one_pair, 0)


def _fox_attention(q_aug, k_aug, v):
    bsz, _, seq, _ = q_aug.shape
    nq = seq // Q_TILE
    nk = seq // K_TILE
    return pl.pallas_call(
        _fox_kernel,
        grid=(bsz, nq),
        in_specs=[
            pl.BlockSpec((1, FOX_HEADS, Q_TILE, LANES), lambda b, i: (b, 0, i, 0)),
            pl.BlockSpec((1, 2, Q_TILE, LANES), lambda b, i: (b, 0, jnp.minimum(i + 1, nq - 1), 0)),
            pl.BlockSpec((1, FOX_HEADS, seq, LANES), lambda b, i: (b, 0, 0, 0)),
            pl.BlockSpec((1, seq, FOX_WIDTH), lambda b, i: (b, 0, 0)),
        ],
        out_specs=pl.BlockSpec((1, FOX_HEADS // 2, Q_TILE, LANES), lambda b, i: (b, 0, i, 0)),
        out_shape=jax.ShapeDtypeStruct((bsz, FOX_HEADS // 2, seq, LANES), BF16),
        scratch_shapes=[
            pltpu.VMEM((nk, FOX_HEADS, VT_ROWS, K_TILE), BF16),
            pltpu.VMEM((FOX_HEADS // 2, 2, LANES, Q_TILE), BF16),
            pltpu.VMEM((2, LANES, Q_TILE), BF16),
            pltpu.VMEM((2 * (Q_TILE // Q_SUB), VT_ROWS, Q_SUB), F32),
            pltpu.VMEM((2 * (Q_TILE // Q_SUB), K_TILE, Q_SUB), F32),
            pltpu.VMEM((2 * (Q_TILE // Q_SUB), 2, 1, Q_SUB), F32),
        ],
        compiler_params=pltpu.CompilerParams(
            dimension_semantics=("arbitrary", "arbitrary"), vmem_limit_bytes=VMEM_LIMIT),
        name="fox_attention",
    )(q_aug, q_aug, k_aug, v)


def _conv_branch(u_ref, cw_ref, cb_ref, cg_ref, cbeta_ref, ubuf, shifted):
    t = pl.program_id(1)
    rows = u_ref.shape[1]
    first = HALO - (CONV_K - 1)
    span = shifted.shape[1]

    @pl.when(t == 0)
    def _():
        ubuf[0:HALO, :] = jnp.zeros((HALO, CONV_WIDTH), F32)

    @pl.when(t > 0)
    def _():
        ubuf[0:HALO, :] = ubuf[rows:rows + HALO, :]

    ubuf[HALO:HALO + rows, :] = u_ref[0]
    for p in range(1, SUBLANES):
        shifted[p - 1] = ubuf[p:p + span, :]

    cw = cw_ref[...]
    cbias = cb_ref[...]
    gam = cg_ref[...]
    beta = cbeta_ref[...]
    outs = []
    for r in range(rows // CONV_ROWS):
        base = r * CONV_ROWS
        acc = jnp.broadcast_to(cbias, (CONV_ROWS, CONV_WIDTH))
        for j in range(CONV_K):
            a, p = divmod(first + j, SUBLANES)
            lo = base + SUBLANES * a
            src = ubuf[lo:lo + CONV_ROWS, :] if p == 0 else shifted[p - 1, lo:lo + CONV_ROWS, :]
            acc = acc + cw[j:j + 1, :] * src
        y = _layer_norm(acc, gam, beta)
        outs.append((y * _sigmoid(y)).astype(BF16))
    return jnp.concatenate(outs, axis=0)


def _mix_cross_kernel(x_ref, u_ref, o_ref, mem_ref, cw_ref, cb_ref, cg_ref, cbeta_ref, wout_ref, g1_ref, b1_ref,
                      wcq_ref, wck_ref, wcv_ref, wco_ref, g2_ref, b2_ref, h_ref, ubuf, shifted, kc_scr, vc_scr):
    @pl.when(pl.program_id(1) == 0)
    def _():
        mb = mem_ref[0].astype(BF16)
        kc_scr[...] = _mm(mb, wck_ref[...]).astype(BF16)
        vc_scr[...] = _mm(mb, wcv_ref[...]).astype(BF16)

    uc = _conv_branch(u_ref, cw_ref, cb_ref, cg_ref, cbeta_ref, ubuf, shifted)
    o_fox = jnp.concatenate([o_ref[0, p] for p in range(o_ref.shape[1])], axis=1)
    rows = x_ref.shape[1]
    groups = [slice(g * rows // CHAIN_SPLIT, (g + 1) * rows // CHAIN_SPLIT) for g in range(CHAIN_SPLIT)]
    mix = [_mm(uc[r], wout_ref[0:CONV_WIDTH, :]) + _mm(o_fox[r], wout_ref[CONV_WIDTH:, :]) for r in groups]
    h1, q = [], []
    for g, r in enumerate(groups):
        h1.append(_layer_norm(DEEPNORM_ALPHA * x_ref[0, r, :] + mix[g], g1_ref[...], b1_ref[...]))
        q.append((_mm(h1[g].astype(BF16), wcq_ref[...]) * (1.0 / math.sqrt(MEM_HEAD_DIM))).astype(BF16))
    o = []
    for g in range(CHAIN_SPLIT):
        outs = []
        for hh in range(MEM_HEADS):
            cols = slice(hh * MEM_HEAD_DIM, (hh + 1) * MEM_HEAD_DIM)
            s = _mm_nt(q[g][:, cols], kc_scr[:, cols])
            p = jnp.exp(s - jnp.max(s, axis=-1, keepdims=True))
            l = jnp.sum(p, axis=-1, keepdims=True)
            outs.append((_mm(p.astype(BF16), vc_scr[:, cols]) / l).astype(BF16))
        o.append(jnp.concatenate(outs, axis=-1))
    y = [_mm(o[g], wco_ref[...]) for g in range(CHAIN_SPLIT)]
    for g, r in enumerate(groups):
        h_ref[0, r, :] = _layer_norm(DEEPNORM_ALPHA * h1[g] + y[g], g2_ref[...], b2_ref[...])


def _mix_cross(x, u, o, mem, conv_w, conv_b, conv_g, conv_beta, w_out, g1, b1, w_cq, w_ck, w_cv, w_co, g2, b2):
    bsz, seq, _ = x.shape
    nt = seq // ROW_TILE
    const2 = lambda b, t: (0, 0)
    tok = lambda width: pl.BlockSpec((1, ROW_TILE, width), lambda b, t: (b, t, 0))
    memblk = pl.BlockSpec((1, MEM_LEN, D_MODEL), lambda b, t: (b, 0, 0))
    vec = pl.BlockSpec((1, D_MODEL), const2)
    cvec = pl.BlockSpec((1, CONV_WIDTH), const2)
    fox = pl.BlockSpec((1, FOX_HEADS // 2, ROW_TILE, LANES), lambda b, t: (b, 0, t, 0))
    mat = pl.BlockSpec((D_MODEL, D_MODEL), const2)
    return pl.pallas_call(
        _mix_cross_kernel,
        grid=(bsz, nt),
        in_specs=[tok(D_MODEL), tok(CONV_WIDTH), fox, memblk, pl.BlockSpec(conv_w.shape, const2), cvec, cvec,
                  cvec, mat, vec, vec, mat, mat, mat, mat, vec, vec],
        out_specs=tok(D_MODEL),
        out_shape=jax.ShapeDtypeStruct((bsz, seq, D_MODEL), F32),
        scratch_shapes=[
            pltpu.VMEM((ROW_TILE + HALO, CONV_WIDTH), F32),
            pltpu.VMEM((SUBLANES - 1, ROW_TILE + HALO - SUBLANES, CONV_WIDTH), F32),
            pltpu.VMEM((MEM_LEN, D_MODEL), BF16),
            pltpu.VMEM((MEM_LEN, D_MODEL), BF16),
        ],
        compiler_params=pltpu.CompilerParams(
            dimension_semantics=("arbitrary", "arbitrary"), vmem_limit_bytes=VMEM_LIMIT),
        name="mix_cross",
    )(x, u, o, mem, conv_w, conv_b, conv_g, conv_beta, w_out, g1, b1, w_cq, w_ck, w_cv, w_co, g2, b2)


def _swiglu_kernel(h_ref, wg_ref, wu_ref, wd_ref, g_ref, b_ref, out_ref):
    h = h_ref[...]
    hb = h.astype(BF16)
    acc = DEEPNORM_ALPHA * h
    for c in range(D_FF // FF_CHUNK):
        cols = slice(c * FF_CHUNK, (c + 1) * FF_CHUNK)
        gate = _mm(hb, wg_ref[:, cols])
        up = _mm(hb, wu_ref[:, cols])
        act = (gate * _sigmoid(gate) * up).astype(BF16)
        acc = acc + _mm(act, wd_ref[cols, :])
    out_ref[...] = _layer_norm(acc, g_ref[...], b_ref[...])


def _swiglu(h, w_gate, w_up, w_down, g, b):
    n = h.shape[0]
    const2 = lambda t: (0, 0)
    tok = pl.BlockSpec((FF_ROW_TILE, D_MODEL), lambda t: (t, 0))
    return pl.pallas_call(
        _swiglu_kernel,
        grid=(n // FF_ROW_TILE,),
        in_specs=[tok, pl.BlockSpec(w_gate.shape, const2), pl.BlockSpec(w_up.shape, const2),
                  pl.BlockSpec(w_down.shape, const2), pl.BlockSpec((1, D_MODEL), const2),
                  pl.BlockSpec((1, D_MODEL), const2)],
        out_specs=tok,
        out_shape=jax.ShapeDtypeStruct((n, D_MODEL), F32),
        compiler_params=pltpu.CompilerParams(
            dimension_semantics=("arbitrary",), vmem_limit_bytes=VMEM_LIMIT),
        name="swiglu",
    )(h, w_gate, w_up, w_down, g, b)


def _layer(h, mem, w_in, b_forget, conv_w, conv_b, conv_ln_g, conv_ln_b, w_out, ln_mix_g, ln_mix_b,
           w_cq, w_ck, w_cv, w_co, ln_cross_g, ln_cross_b, w_gate, w_up, w_down, ln_ffn_g, ln_ffn_b):
    bsz, seq, _ = h.shape
    row = lambda v: v.reshape(1, -1).astype(F32)
    w_f = jnp.pad(w_in[:, C_F:], ((0, 0), (0, LANES - FOX_HEADS))).astype(BF16)
    b_f = jnp.pad(b_forget.astype(F32), (0, LANES - FOX_HEADS)).reshape(1, LANES)
    tri = jnp.tri(ROW_TILE, dtype=BF16)

    later = [w.astype(F32) for w in (w_ck, w_cv, w_out, w_cq, w_co, w_gate, w_up, w_down)]
    (u, q_aug, k_aug, v), later = _inproj(h, w_in.astype(F32), w_f, b_f, tri, later)
    w_ck, w_cv, w_out, w_cq, w_co, w_gate, w_up, w_down = later
    o = _fox_attention(q_aug, k_aug, v)
    h2 = _mix_cross(h, u, o, mem, conv_w.astype(F32), row(conv_b), row(conv_ln_g), row(conv_ln_b), w_out,
                    row(ln_mix_g), row(ln_mix_b), w_cq, w_ck, w_cv, w_co, row(ln_cross_g), row(ln_cross_b))
    h3 = _swiglu(h2.reshape(bsz * seq, D_MODEL), w_gate, w_up, w_down, row(ln_ffn_g), row(ln_ffn_b))
    return h3.reshape(bsz, seq, D_MODEL)


def kernel(x, mem, w_in, b_forget, conv_w, conv_b, conv_ln_g, conv_ln_b, w_out, ln_mix_g, ln_mix_b,
           w_cq, w_ck, w_cv, w_co, ln_cross_g, ln_cross_b, w_gate, w_up, w_down, ln_ffn_g, ln_ffn_b):
    depth = w_in.shape[0]
    assert depth == 1, "DEEPNORM_ALPHA is fixed for a single layer"
    h = x
    for l in range(depth):
        h = _layer(h, mem, w_in[l], b_forget[l], conv_w[l], conv_b[l], conv_ln_g[l], conv_ln_b[l],
                   w_out[l], ln_mix_g[l], ln_mix_b[l], w_cq[l], w_ck[l], w_cv[l], w_co[l],
                   ln_cross_g[l], ln_cross_b[l], w_gate[l], w_up[l], w_down[l], ln_ffn_g[l], ln_ffn_b[l])
    return h
```

```python
import functools
import math

import jax
import jax.numpy as jnp
from jax import lax
from jax.experimental import pallas as pl
from jax.experimental.pallas import tpu as pltpu

D_MODEL = 1024
CONV_WIDTH = 512
CONV_K = 31
FOX_WIDTH = 512
FOX_HEADS = 8
FOX_HEAD_DIM = 64
MEM_LEN = 256
MEM_HEADS = 4
MEM_HEAD_DIM = 256
D_FF = 2816
LN_EPS = 1e-5
NEG_INF = -1e30
DEEPNORM_ALPHA = 2.0 ** 0.25
LOG2E = math.log2(math.e)

LANES = 128
SUBLANES = 8
HALO = 32
IN_ROW_TILE = 1024
TRI_ROWS = 512
FF_ROW_TILE = 1024
MIX_ROW_TILE = 1024
Q_TILE = 2048
K_TILE = 512
DIAG = Q_TILE // K_TILE
Q_SUB = 256
VT_ROWS = FOX_HEAD_DIM + 2 * SUBLANES
FF_CHUNK = 256
CONV_ROWS = 32
CONV_COLS = 256
LN_ROWS = 64
CHAIN_SPLIT = 2
VMEM_LIMIT = 56 * 1024 * 1024

BF16 = jnp.bfloat16
F32 = jnp.float32

C_GLU_A = 0
C_GLU_B = CONV_WIDTH
C_Q = 2 * CONV_WIDTH
C_K = C_Q + FOX_WIDTH
C_V = C_K + FOX_WIDTH
C_F = C_V + FOX_WIDTH


def _mm(a, b):
    return jnp.dot(a, b, preferred_element_type=F32)


def _mm_nt(a, b):
    return lax.dot_general(a, b, (((1,), (1,)), ((), ())), preferred_element_type=F32)


def _layer_norm(x, g, b):
    mu = jnp.mean(x, axis=-1, keepdims=True)
    xc = x - mu
    var = jnp.mean(xc * xc, axis=-1, keepdims=True)
    return xc * lax.rsqrt(var + LN_EPS) * g + b


def _sigmoid(x):
    return 1.0 / (1.0 + jnp.exp(-x))


def _split3(x):
    hi = x.astype(BF16).astype(F32)
    r = x - hi
    mid = r.astype(BF16).astype(F32)
    lo = (r - mid).astype(BF16).astype(F32)
    return hi, mid, lo


def _pack3(hi, mid, lo):
    return hi + pltpu.roll(mid, FOX_HEADS, axis=1) + pltpu.roll(lo, 2 * FOX_HEADS, axis=1)


def _inproj_kernel(x_ref, w_ref, wf_ref, bf_ref, tri_ref, *refs, n_cast):
    cast_in, (u_ref, q_ref, k_ref, v_ref), cast_out = refs[:n_cast], refs[n_cast:n_cast + 4], refs[n_cast + 4:-1]
    carry = refs[-1]
    t = pl.program_id(1)
    rows = x_ref.shape[1]
    xb = x_ref[0].astype(BF16)

    @pl.when(t == 0)
    def _():
        carry[...] = jnp.zeros_like(carry)

    glu_a = _mm(xb, w_ref[:, C_GLU_A:C_GLU_A + CONV_WIDTH])
    glu_b = _mm(xb, w_ref[:, C_GLU_B:C_GLU_B + CONV_WIDTH])
    u_ref[0] = glu_a * _sigmoid(glu_b)

    lane = lax.broadcasted_iota(jnp.int32, (rows, LANES), 1)
    f = _mm(xb, wf_ref[...]) + bf_ref[...]
    logf = jnp.minimum(f, 0.0) - jnp.log(1.0 + jnp.exp(-jnp.abs(f)))
    logf = jnp.where(lane < FOX_HEADS, logf, 0.0)
    packed = _pack3(*_split3(logf)).astype(BF16)
    seg_rows = tri_ref.shape[0]
    seg_lane = lax.broadcasted_iota(jnp.int32, (seg_rows, LANES), 1)
    running = carry[0:1, :]
    segments = []
    for r0 in range(0, rows, seg_rows):
        res = _mm(tri_ref[...], packed[r0:r0 + seg_rows])
        seg = res + pltpu.roll(res, LANES - FOX_HEADS, axis=1) + pltpu.roll(res, LANES - 2 * FOX_HEADS, axis=1)
        seg = jnp.where(seg_lane < FOX_HEADS, seg + running, 0.0)
        running = seg[seg_rows - 1:seg_rows, :]
        segments.append(seg)
    carry[0:1, :] = running
    cum = jnp.concatenate(segments, axis=0)
    cpack = _pack3(*_split3(cum * LOG2E))

    scale = LOG2E / math.sqrt(FOX_HEAD_DIM)
    for hp in range(FOX_HEADS // 2):
        if hp % 2 == 0:
            qquad = _mm(xb, w_ref[:, C_Q + hp * LANES:C_Q + (hp + 2) * LANES]) * scale
            kquad = _mm(xb, w_ref[:, C_K + hp * LANES:C_K + (hp + 2) * LANES])
        qpair = qquad[:, (hp % 2) * LANES:(hp % 2 + 1) * LANES]
        kpair = kquad[:, (hp % 2) * LANES:(hp % 2 + 1) * LANES]
        for sub in range(2):
            h = 2 * hp + sub
            aug0 = FOX_HEAD_DIM if sub == 0 else 0
            data = (lane < FOX_HEAD_DIM) if sub == 0 else (lane >= FOX_HEAD_DIM)
            slot_a = (lane == aug0) | (lane == aug0 + FOX_HEADS) | (lane == aug0 + 2 * FOX_HEADS)
            slot_b = (lane == aug0 + 1) | (lane == aug0 + FOX_HEADS + 1) | (lane == aug0 + 2 * FOX_HEADS + 1)
            ra = pltpu.roll(cpack, (aug0 - h) % LANES, axis=1)
            rb = pltpu.roll(cpack, (aug0 + 1 - h) % LANES, axis=1)
            qa = jnp.where(data, qpair, jnp.where(slot_a, ra, jnp.where(slot_b, 1.0, 0.0)))
            ka = jnp.where(data, kpair, jnp.where(slot_a, 1.0, jnp.where(slot_b, -rb, 0.0)))
            q_ref[0, h] = qa.astype(BF16)
            k_ref[0, h] = ka.astype(BF16)
    v_ref[0] = _mm(xb, w_ref[:, C_V:C_V + FOX_WIDTH]).astype(BF16)

    for src_ref, dst_ref in zip(cast_in, cast_out):
        dst_ref[...] = src_ref[...].astype(BF16)


def _slab_spec(rows, cols, steps, steps_per_seq):
    nslabs = steps
    while rows % nslabs or (rows // nslabs) % (2 * SUBLANES):
        nslabs //= 2
    hold = steps // nslabs
    return pl.BlockSpec((rows // nslabs, cols), lambda b, t: ((b * steps_per_seq + t) // hold, 0))


def _inproj(x, w_main, w_f, b_f, tri, to_cast):
    bsz, seq, _ = x.shape
    nt = seq // IN_ROW_TILE
    const2 = lambda b, t: (0, 0)
    slabs = [_slab_spec(w.shape[0], w.shape[1], bsz * nt, nt) for w in to_cast]
    outs = pl.pallas_call(
        functools.partial(_inproj_kernel, n_cast=len(to_cast)),
        grid=(bsz, nt),
        in_specs=[
            pl.BlockSpec((1, IN_ROW_TILE, D_MODEL), lambda b, t: (b, t, 0)),
            pl.BlockSpec(w_main.shape, const2),
            pl.BlockSpec(w_f.shape, const2),
            pl.BlockSpec(b_f.shape, const2),
            pl.BlockSpec(tri.shape, const2),
        ] + slabs,
        out_specs=[
            pl.BlockSpec((1, IN_ROW_TILE, CONV_WIDTH), lambda b, t: (b, t, 0)),
            pl.BlockSpec((1, FOX_HEADS, IN_ROW_TILE, LANES), lambda b, t: (b, 0, t, 0)),
            pl.BlockSpec((1, FOX_HEADS, IN_ROW_TILE, LANES), lambda b, t: (b, 0, t, 0)),
            pl.BlockSpec((1, IN_ROW_TILE, FOX_WIDTH), lambda b, t: (b, t, 0)),
        ] + slabs,
        out_shape=[
            jax.ShapeDtypeStruct((bsz, seq, CONV_WIDTH), F32),
            jax.ShapeDtypeStruct((bsz, FOX_HEADS, seq, LANES), BF16),
            jax.ShapeDtypeStruct((bsz, FOX_HEADS, seq, LANES), BF16),
            jax.ShapeDtypeStruct((bsz, seq, FOX_WIDTH), BF16),
        ] + [jax.ShapeDtypeStruct(w.shape, BF16) for w in to_cast],
        scratch_shapes=[pltpu.VMEM((SUBLANES, LANES), F32)],
        compiler_params=pltpu.CompilerParams(
            dimension_semantics=("arbitrary", "arbitrary"), vmem_limit_bytes=VMEM_LIMIT),
        name="inproj",
    )(x, w_main, w_f, b_f, tri, *to_cast)
    return outs[:4], outs[4:]


def _fox_kernel(q_ref, qn_ref, k_ref, v_ref, o_ref, vt_scr, qt_scr, qtn_scr, acc_scr, s_scr, m_scr):
    i = pl.program_id(1)
    nq = pl.num_programs(1)
    nk = vt_scr.shape[0]
    npairs = FOX_HEADS // 2
    units = [(h, qb) for h in range(2) for qb in range(Q_TILE // Q_SUB)]
    first_diag = DIAG * i

    def band_keys(qb, band):
        if band is None:
            return K_TILE
        return max(0, min(K_TILE, (qb + 1) * Q_SUB - band * K_TILE))

    def transpose_q(src_ref, head0, dst):
        for h in range(2):
            dst[h] = src_ref[0, head0 + h].astype(F32).T.astype(BF16)

    def score_unit(pair, j, u, band, fresh, qt):
        h, qb = units[u]
        nkeys = band_keys(qb, band)
        if nkeys == 0:
            return
        kt = k_ref[0, 2 * pair + h, pl.ds(pl.multiple_of(j * K_TILE, K_TILE), nkeys), :]
        st = _mm(kt, qt[h, :, qb * Q_SUB:(qb + 1) * Q_SUB])
        if band is not None and band * K_TILE + nkeys - 1 > qb * Q_SUB:
            kpos = lax.broadcasted_iota(jnp.int32, st.shape, 0) + band * K_TILE
            qpos = lax.broadcasted_iota(jnp.int32, st.shape, 1) + qb * Q_SUB
            st = jnp.where(kpos <= qpos, st, NEG_INF)
        s_scr[u, 0:nkeys, :] = st
        m_prev = jnp.full((1, Q_SUB), NEG_INF, F32) if fresh else m_scr[u, 1]
        m_scr[u, 0] = m_prev
        m_scr[u, 1] = jnp.maximum(m_prev, jnp.max(st, axis=0, keepdims=True))

    def value_unit(pair, j, u, band):
        h, qb = units[u]
        nkeys = band_keys(qb, band)
        if nkeys == 0:
            return
        m_new = m_scr[u, 1]
        alpha = jnp.exp2(m_scr[u, 0] - m_new)
        pt = jnp.exp2((s_scr[u, 0:nkeys, :] - m_new).astype(BF16))
        acc_scr[u] = alpha * acc_scr[u] + _mm(vt_scr[j, 2 * pair + h, :, 0:nkeys], pt)

    def stages(value=None, score=None, value_band=None, score_band=None, fresh=False, qt=None):
        for u in range(len(units)):
            if value is not None:
                value_unit(value[0], value[1], u, value_band)
            if score is not None:
                score_unit(score[0], score[1], u, score_band, fresh, qt_scr.at[score[0]] if qt is None else qt)

    @pl.when(i == 0)
    def _():
        tail = (lax.broadcasted_iota(jnp.int32, (VT_ROWS - FOX_HEAD_DIM, K_TILE), 0) == 0).astype(BF16)
        for c in range(nk):
            for pair in range(npairs):
                blk = v_ref[0, c * K_TILE:(c + 1) * K_TILE, pair * LANES:(pair + 1) * LANES]
                blk = blk.astype(F32).T.astype(BF16)
                for h in range(2):
                    vt_scr[c, 2 * pair + h, 0:FOX_HEAD_DIM, :] = blk[h * FOX_HEAD_DIM:(h + 1) * FOX_HEAD_DIM, :]
                    vt_scr[c, 2 * pair + h, FOX_HEAD_DIM:VT_ROWS, :] = tail

    for pair in range(npairs):
        transpose_q(q_ref, 2 * pair, qt_scr.at[pair])

    @pl.when(i == 0)
    def _():
        stages(score=(0, 0), score_band=0, fresh=True)

    def one_pair(pair, carry):
        acc_scr[...] = jnp.zeros(acc_scr.shape, F32)

        def trip(t, c):
            stages(value=(pair, t), score=(pair, t + 1))
            return c

        lax.fori_loop(0, first_diag - 1, trip, 0)

        @pl.when(i > 0)
        def _():
            stages(value=(pair, first_diag - 1), score=(pair, first_diag), score_band=0)

        for band in range(DIAG - 1):
            stages(value=(pair, first_diag + band), score=(pair, first_diag + band + 1),
                   value_band=band, score_band=band + 1)

        last = (pair, first_diag + DIAG - 1)
        more_pairs = pair + 1 < npairs

        @pl.when(more_pairs & (i == 0))
        def _():
            stages(value=last, score=(pair + 1, 0), value_band=DIAG - 1, score_band=0, fresh=True)

        @pl.when(more_pairs & (i > 0))
        def _():
            stages(value=last, score=(pair + 1, 0), value_band=DIAG - 1, fresh=True)

        @pl.when(jnp.logical_not(more_pairs) & (i < nq - 1))
        def _():
            transpose_q(qn_ref, 0, qtn_scr)
            stages(value=last, score=(0, 0), value_band=DIAG - 1, fresh=True, qt=qtn_scr)

        @pl.when(jnp.logical_not(more_pairs) & (i == nq - 1))
        def _():
            stages(value=last, value_band=DIAG - 1)

        rows = []
        for h in range(2):
            blocks = [acc_scr[u] for u in range(len(units)) if units[u][0] == h]
            rows.append(jnp.concatenate(
                [a[0:FOX_HEAD_DIM] / a[FOX_HEAD_DIM:FOX_HEAD_DIM + 1] for a in blocks], axis=1))
        o_ref[0, pair] = jnp.concatenate(rows, axis=0).T.astype(BF16)
        return carry

    lax.fori_loop(0, npairs, one_pair, 0)


def _fox_attention(q_aug, k_aug, v):
    bsz, _, seq, _ = q_aug.shape
    nq = seq // Q_TILE
    nk = seq // K_TILE
    return pl.pallas_call(
        _fox_kernel,
        grid=(bsz, nq),
        in_specs=[
            pl.BlockSpec((1, FOX_HEADS, Q_TILE, LANES), lambda b, i: (b, 0, i, 0)),
            pl.BlockSpec((1, 2, Q_TILE, LANES), lambda b, i: (b, 0, jnp.minimum(i + 1, nq - 1), 0)),
            pl.BlockSpec((1, FOX_HEADS, seq, LANES), lambda b, i: (b, 0, 0, 0), pipeline_mode=pl.Buffered(1)),
            pl.BlockSpec((1, seq, FOX_WIDTH), lambda b, i: (b, 0, 0), pipeline_mode=pl.Buffered(1)),
        ],
        out_specs=pl.BlockSpec((1, FOX_HEADS // 2, Q_TILE, LANES), lambda b, i: (b, 0, i, 0)),
        out_shape=jax.ShapeDtypeStruct((bsz, FOX_HEADS // 2, seq, LANES), BF16),
        scratch_shapes=[
            pltpu.VMEM((nk, FOX_HEADS, VT_ROWS, K_TILE), BF16),
            pltpu.VMEM((FOX_HEADS // 2, 2, LANES, Q_TILE), BF16),
            pltpu.VMEM((2, LANES, Q_TILE), BF16),
            pltpu.VMEM((2 * (Q_TILE // Q_SUB), VT_ROWS, Q_SUB), F32),
            pltpu.VMEM((2 * (Q_TILE // Q_SUB), K_TILE, Q_SUB), F32),
            pltpu.VMEM((2 * (Q_TILE // Q_SUB), 2, 1, Q_SUB), F32),
        ],
        compiler_params=pltpu.CompilerParams(
            dimension_semantics=("arbitrary", "arbitrary"), vmem_limit_bytes=VMEM_LIMIT),
        name="fox_attention",
    )(q_aug, q_aug, k_aug, v)


def _memkv_kernel(mem_ref, wk_ref, wv_ref, k_ref, v_ref):
    mb = mem_ref[0].astype(BF16)
    k_ref[0] = _mm(mb, wk_ref[...]).astype(BF16)
    v_ref[0] = _mm(mb, wv_ref[...]).astype(BF16)


def _mem_kv(mem, w_ck, w_cv):
    bsz = mem.shape[0]
    const2 = lambda b: (0, 0)
    blk = pl.BlockSpec((1, MEM_LEN, D_MODEL), lambda b: (b, 0, 0))
    return pl.pallas_call(
        _memkv_kernel,
        grid=(bsz,),
        in_specs=[blk, pl.BlockSpec(w_ck.shape, const2), pl.BlockSpec(w_cv.shape, const2)],
        out_specs=[blk, blk],
        out_shape=[jax.ShapeDtypeStruct((bsz, MEM_LEN, D_MODEL), BF16)] * 2,
        compiler_params=pltpu.CompilerParams(
            dimension_semantics=("arbitrary",), vmem_limit_bytes=VMEM_LIMIT),
        name="mem_kv",
    )(mem, w_ck, w_cv)


def _conv_branch(u_ref, cw_ref, cb_ref, cg_ref, cbeta_ref, ubuf, shifted, conv_scr):
    t = pl.program_id(1)
    rows = u_ref.shape[1]
    first = HALO - (CONV_K - 1)
    span = shifted.shape[1]

    @pl.when(t == 0)
    def _():
        ubuf[0:HALO, :] = jnp.zeros((HALO, CONV_WIDTH), F32)

    @pl.when(t > 0)
    def _():
        ubuf[0:HALO, :] = ubuf[rows:rows + HALO, :]

    ubuf[HALO:HALO + rows, :] = u_ref[0]
    for c0 in range(0, CONV_WIDTH, CONV_COLS):
        cols = slice(c0, c0 + CONV_COLS)
        for p in range(1, SUBLANES):
            shifted[p - 1] = ubuf[p:p + span, cols]
        cw = cw_ref[:, cols]
        cbias = cb_ref[:, cols]
        for r in range(rows // CONV_ROWS):
            base = r * CONV_ROWS
            acc = jnp.broadcast_to(cbias, (CONV_ROWS, CONV_COLS))
            for j in range(CONV_K):
                a, p = divmod(first + j, SUBLANES)
                lo = base + SUBLANES * a
                src = ubuf[lo:lo + CONV_ROWS, cols] if p == 0 else shifted[p - 1, lo:lo + CONV_ROWS, :]
                acc = acc + cw[j:j + 1, :] * src
            conv_scr[base:base + CONV_ROWS, cols] = acc

    gam = cg_ref[...]
    beta = cbeta_ref[...]
    outs = []
    for r in range(rows // LN_ROWS):
        y = _layer_norm(conv_scr[r * LN_ROWS:(r + 1) * LN_ROWS, :], gam, beta)
        outs.append((y * _sigmoid(y)).astype(BF16))
    return jnp.concatenate(outs, axis=0)


def _mix_cross_kernel(x_ref, u_ref, o_ref, cw_ref, cb_ref, cg_ref, cbeta_ref, wout_ref, g1_ref, b1_ref,
                      wcq_ref, kc_ref, vc_ref, wco_ref, g2_ref, b2_ref, h_ref, ubuf, shifted, conv_scr):
    uc = _conv_branch(u_ref, cw_ref, cb_ref, cg_ref, cbeta_ref, ubuf, shifted, conv_scr)
    o_fox = jnp.concatenate([o_ref[0, p] for p in range(o_ref.shape[1])], axis=1)
    rows = x_ref.shape[1]
    groups = [slice(g * rows // CHAIN_SPLIT, (g + 1) * rows // CHAIN_SPLIT) for g in range(CHAIN_SPLIT)]
    mix = [_mm(uc[r], wout_ref[0:CONV_WIDTH, :]) + _mm(o_fox[r], wout_ref[CONV_WIDTH:, :]) for r in groups]
    h1, q = [], []
    for g, r in enumerate(groups):
        h1.append(_layer_norm(DEEPNORM_ALPHA * x_ref[0, r, :] + mix[g], g1_ref[...], b1_ref[...]))
        q.append((_mm(h1[g].astype(BF16), wcq_ref[...]) * (1.0 / math.sqrt(MEM_HEAD_DIM))).astype(BF16))
    o = []
    for g in range(CHAIN_SPLIT):
        outs = []
        for hh in range(MEM_HEADS):
            cols = slice(hh * MEM_HEAD_DIM, (hh + 1) * MEM_HEAD_DIM)
            s = _mm_nt(q[g][:, cols], kc_ref[0, :, cols])
            p = jnp.exp(s - jnp.max(s, axis=-1, keepdims=True))
            l = jnp.sum(p, axis=-1, keepdims=True)
            outs.append((_mm(p.astype(BF16), vc_ref[0, :, cols]) / l).astype(BF16))
        o.append(jnp.concatenate(outs, axis=-1))
    y = [_mm(o[g], wco_ref[...]) for g in range(CHAIN_SPLIT)]
    for g, r in enumerate(groups):
        h_ref[0, r, :] = _layer_norm(DEEPNORM_ALPHA * h1[g] + y[g], g2_ref[...], b2_ref[...])


def _mix_cross(x, u, o, conv_w, conv_b, conv_g, conv_beta, w_out, g1, b1, w_cq, kc, vc, w_co, g2, b2):
    bsz, seq, _ = x.shape
    nt = seq // MIX_ROW_TILE
    const2 = lambda b, t: (0, 0)
    tok = lambda width: pl.BlockSpec((1, MIX_ROW_TILE, width), lambda b, t: (b, t, 0))
    memblk = pl.BlockSpec((1, MEM_LEN, D_MODEL), lambda b, t: (b, 0, 0))
    vec = pl.BlockSpec((1, D_MODEL), const2)
    cvec = pl.BlockSpec((1, CONV_WIDTH), const2)
    fox = pl.BlockSpec((1, FOX_HEADS // 2, MIX_ROW_TILE, LANES), lambda b, t: (b, 0, t, 0))
    mat = pl.BlockSpec((D_MODEL, D_MODEL), const2)
    return pl.pallas_call(
        _mix_cross_kernel,
        grid=(bsz, nt),
        in_specs=[tok(D_MODEL), tok(CONV_WIDTH), fox, pl.BlockSpec(conv_w.shape, const2), cvec, cvec,
                  cvec, mat, vec, vec, mat, memblk, memblk, mat, vec, vec],
        out_specs=tok(D_MODEL),
        out_shape=jax.ShapeDtypeStruct((bsz, seq, D_MODEL), F32),
        scratch_shapes=[
            pltpu.VMEM((MIX_ROW_TILE + HALO, CONV_WIDTH), F32),
            pltpu.VMEM((SUBLANES - 1, MIX_ROW_TILE + HALO - SUBLANES, CONV_COLS), F32),
            pltpu.VMEM((MIX_ROW_TILE, CONV_WIDTH), F32),
        ],
        compiler_params=pltpu.CompilerParams(
            dimension_semantics=("arbitrary", "arbitrary"), vmem_limit_bytes=VMEM_LIMIT),
        name="mix_cross",
    )(x, u, o, conv_w, conv_b, conv_g, conv_beta, w_out, g1, b1, w_cq, kc, vc, w_co, g2, b2)


def _swiglu_kernel(h_ref, wg_ref, wu_ref, wd_ref, g_ref, b_ref, out_ref):
    h = h_ref[...]
    hb = h.astype(BF16)
    acc = DEEPNORM_ALPHA * h
    for c in range(D_FF // FF_CHUNK):
        cols = slice(c * FF_CHUNK, (c + 1) * FF_CHUNK)
        gate = _mm(hb, wg_ref[:, cols])
        up = _mm(hb, wu_ref[:, cols])
        act = (gate * _sigmoid(gate) * up).astype(BF16)
        acc = acc + _mm(act, wd_ref[cols, :])
    out_ref[...] = _layer_norm(acc, g_ref[...], b_ref[...])


def _swiglu(h, w_gate, w_up, w_down, g, b):
    n = h.shape[0]
    const2 = lambda t: (0, 0)
    tok = pl.BlockSpec((FF_ROW_TILE, D_MODEL), lambda t: (t, 0))
    return pl.pallas_call(
        _swiglu_kernel,
        grid=(n // FF_ROW_TILE,),
        in_specs=[tok, pl.BlockSpec(w_gate.shape, const2), pl.BlockSpec(w_up.shape, const2),
                  pl.BlockSpec(w_down.shape, const2), pl.BlockSpec((1, D_MODEL), const2),
                  pl.BlockSpec((1, D_MODEL), const2)],
        out_specs=tok,
        out_shape=jax.ShapeDtypeStruct((n, D_MODEL), F32),
        compiler_params=pltpu.CompilerParams(
            dimension_semantics=("arbitrary",), vmem_limit_bytes=VMEM_LIMIT),
        name="swiglu",
    )(h, w_gate, w_up, w_down, g, b)


def _layer(h, mem, w_in, b_forget, conv_w, conv_b, conv_ln_g, conv_ln_b, w_out, ln_mix_g, ln_mix_b,
           w_cq, w_ck, w_cv, w_co, ln_cross_g, ln_cross_b, w_gate, w_up, w_down, ln_ffn_g, ln_ffn_b):
    bsz, seq, _ = h.shape
    row = lambda v: v.reshape(1, -1).astype(F32)
    w_main = w_in[:, :C_F].astype(BF16)
    w_f = jnp.pad(w_in[:, C_F:], ((0, 0), (0, LANES - FOX_HEADS))).astype(BF16)
    b_f = jnp.pad(b_forget.astype(F32), (0, LANES - FOX_HEADS)).reshape(1, LANES)
    tri = jnp.tri(TRI_ROWS, dtype=BF16)

    later = [w.astype(F32) for w in (w_ck, w_cv, w_out, w_cq, w_co, w_gate, w_up, w_down)]
    (u, q_aug, k_aug, v), later = _inproj(h, w_main, w_f, b_f, tri, later)
    w_ck, w_cv, w_out, w_cq, w_co, w_gate, w_up, w_down = later
    o = _fox_attention(q_aug, k_aug, v)
    kc, vc = _mem_kv(mem, w_ck, w_cv)
    h2 = _mix_cross(h, u, o, conv_w.astype(F32), row(conv_b), row(conv_ln_g), row(conv_ln_b),
                    w_out, row(ln_mix_g), row(ln_mix_b), w_cq, kc, vc, w_co, row(ln_cross_g), row(ln_cross_b))
    h3 = _swiglu(h2.reshape(bsz * seq, D_MODEL), w_gate, w_up, w_down, row(ln_ffn_g), row(ln_ffn_b))
    return h3.reshape(bsz, seq, D_MODEL)


def kernel(x, mem, w_in, b_forget, conv_w, conv_b, conv_ln_g, conv_ln_b, w_out, ln_mix_g, ln_mix_b,
           w_cq, w_ck, w_cv, w_co, ln_cross_g, ln_cross_b, w_gate, w_up, w_down, ln_ffn_g, ln_ffn_b):
    depth = w_in.shape[0]
    assert depth == 1, "DEEPNORM_ALPHA is fixed for a single layer"
    h = x
    for l in range(depth):
        h = _layer(h, mem, w_in[l], b_forget[l], conv_w[l], conv_b[l], conv_ln_g[l], conv_ln_b[l],
                   w_out[l], ln_mix_g[l], ln_mix_b[l], w_cq[l], w_ck[l], w_cv[l], w_co[l],
                   ln_cross_g[l], ln_cross_b[l], w_gate[l], w_up[l], w_down[l], ln_ffn_g[l], ln_ffn_b[l])
    return h
```

```python
import functools
import math

import jax
import jax.numpy as jnp
from jax import lax
from jax.experimental import pallas as pl
from jax.experimental.pallas import tpu as pltpu

D_MODEL = 1024
CONV_WIDTH = 512
CONV_K = 31
FOX_WIDTH = 512
FOX_HEADS = 8
FOX_HEAD_DIM = 64
MEM_LEN = 256
MEM_HEADS = 4
MEM_HEAD_DIM = 256
D_FF = 2816
LN_EPS = 1e-5
NEG_INF = -1e30
DEEPNORM_ALPHA = 2.0 ** 0.25
LOG2E = math.log2(math.e)

LANES = 128
SUBLANES = 8
HALO = 32
IN_ROW_TILE = 1024
TRI_ROWS = 512
FF_ROW_TILE = 1024
MIX_ROW_TILE = 1024
Q_TILE = 1024
K_TILE = 1024
DIAG = Q_TILE // K_TILE
Q_SUB = 256
VT_ROWS = FOX_HEAD_DIM + 2 * SUBLANES
FF_CHUNK = 256
CONV_ROWS = 32
CONV_COLS = 256
LN_ROWS = 64
CHAIN_SPLIT = 2
VMEM_LIMIT = 56 * 1024 * 1024

BF16 = jnp.bfloat16
F32 = jnp.float32

C_GLU_A = 0
C_GLU_B = CONV_WIDTH
C_Q = 2 * CONV_WIDTH
C_K = C_Q + FOX_WIDTH
C_V = C_K + FOX_WIDTH
C_F = C_V + FOX_WIDTH


def _mm(a, b):
    return jnp.dot(a, b, preferred_element_type=F32)


def _mm_nt(a, b):
    return lax.dot_general(a, b, (((1,), (1,)), ((), ())), preferred_element_type=F32)


def _layer_norm(x, g, b):
    mu = jnp.mean(x, axis=-1, keepdims=True)
    xc = x - mu
    var = jnp.mean(xc * xc, axis=-1, keepdims=True)
    return xc * lax.rsqrt(var + LN_EPS) * g + b


def _sigmoid(x):
    return 1.0 / (1.0 + jnp.exp(-x))


def _sigmoid_approx(x):
    return pl.reciprocal(1.0 + jnp.exp(-x), approx=True)


def _split3(x):
    hi = x.astype(BF16).astype(F32)
    r = x - hi
    mid = r.astype(BF16).astype(F32)
    lo = (r - mid).astype(BF16).astype(F32)
    return hi, mid, lo


def _pack3(hi, mid, lo):
    return hi + pltpu.roll(mid, FOX_HEADS, axis=1) + pltpu.roll(lo, 2 * FOX_HEADS, axis=1)


def _inproj_kernel(x_ref, w_ref, wf_ref, bf_ref, tri_ref, *refs, n_cast):
    cast_in, (u_ref, q_ref, k_ref, v_ref), cast_out = refs[:n_cast], refs[n_cast:n_cast + 4], refs[n_cast + 4:-1]
    carry = refs[-1]
    t = pl.program_id(1)
    rows = x_ref.shape[1]
    xb = x_ref[0].astype(BF16)

    @pl.when(t == 0)
    def _():
        carry[...] = jnp.zeros_like(carry)

    glu_a = _mm(xb, w_ref[:, C_GLU_A:C_GLU_A + CONV_WIDTH])
    glu_b = _mm(xb, w_ref[:, C_GLU_B:C_GLU_B + CONV_WIDTH])
    u_ref[0] = glu_a * _sigmoid(glu_b)

    lane = lax.broadcasted_iota(jnp.int32, (rows, LANES), 1)
    f = _mm(xb, wf_ref[...]) + bf_ref[...]
    logf = jnp.minimum(f, 0.0) - jnp.log(1.0 + jnp.exp(-jnp.abs(f)))
    logf = jnp.where(lane < FOX_HEADS, logf, 0.0)
    packed = _pack3(*_split3(logf)).astype(BF16)
    seg_rows = tri_ref.shape[0]
    seg_lane = lax.broadcasted_iota(jnp.int32, (seg_rows, LANES), 1)
    running = carry[0:1, :]
    segments = []
    for r0 in range(0, rows, seg_rows):
        res = _mm(tri_ref[...], packed[r0:r0 + seg_rows])
        seg = res + pltpu.roll(res, LANES - FOX_HEADS, axis=1) + pltpu.roll(res, LANES - 2 * FOX_HEADS, axis=1)
        seg = jnp.where(seg_lane < FOX_HEADS, seg + running, 0.0)
        running = seg[seg_rows - 1:seg_rows, :]
        segments.append(seg)
    carry[0:1, :] = running
    cum = jnp.concatenate(segments, axis=0)
    cpack = _pack3(*_split3(cum * LOG2E))

    scale = LOG2E / math.sqrt(FOX_HEAD_DIM)
    for hp in range(FOX_HEADS // 2):
        if hp % 2 == 0:
            qquad = _mm(xb, w_ref[:, C_Q + hp * LANES:C_Q + (hp + 2) * LANES]) * scale
            kquad = _mm(xb, w_ref[:, C_K + hp * LANES:C_K + (hp + 2) * LANES])
        qpair = qquad[:, (hp % 2) * LANES:(hp % 2 + 1) * LANES]
        kpair = kquad[:, (hp % 2) * LANES:(hp % 2 + 1) * LANES]
        for sub in range(2):
            h = 2 * hp + sub
            aug0 = FOX_HEAD_DIM if sub == 0 else 0
            data = (lane < FOX_HEAD_DIM) if sub == 0 else (lane >= FOX_HEAD_DIM)
            slot_a = (lane == aug0) | (lane == aug0 + FOX_HEADS) | (lane == aug0 + 2 * FOX_HEADS)
            slot_b = (lane == aug0 + 1) | (lane == aug0 + FOX_HEADS + 1) | (lane == aug0 + 2 * FOX_HEADS + 1)
            ra = pltpu.roll(cpack, (aug0 - h) % LANES, axis=1)
            rb = pltpu.roll(cpack, (aug0 + 1 - h) % LANES, axis=1)
            qa = jnp.where(data, qpair, jnp.where(slot_a, ra, jnp.where(slot_b, 1.0, 0.0)))
            ka = jnp.where(data, kpair, jnp.where(slot_a, 1.0, jnp.where(slot_b, -rb, 0.0)))
            q_ref[0, h] = qa.astype(BF16)
            k_ref[0, h] = ka.astype(BF16)
    v_ref[0] = _mm(xb, w_ref[:, C_V:C_V + FOX_WIDTH]).astype(BF16)

    for src_ref, dst_ref in zip(cast_in, cast_out):
        dst_ref[...] = src_ref[...].astype(BF16)


def _slab_spec(rows, cols, steps, steps_per_seq):
    nslabs = steps
    while rows % nslabs or (rows // nslabs) % (2 * SUBLANES):
        nslabs //= 2
    hold = steps // nslabs
    return pl.BlockSpec((rows // nslabs, cols), lambda b, t: ((b * steps_per_seq + t) // hold, 0))


def _inproj(x, w_main, w_f, b_f, tri, to_cast):
    bsz, seq, _ = x.shape
    nt = seq // IN_ROW_TILE
    const2 = lambda b, t: (0, 0)
    slabs = [_slab_spec(w.shape[0], w.shape[1], bsz * nt, nt) for w in to_cast]
    outs = pl.pallas_call(
        functools.partial(_inproj_kernel, n_cast=len(to_cast)),
        grid=(bsz, nt),
        in_specs=[
            pl.BlockSpec((1, IN_ROW_TILE, D_MODEL), lambda b, t: (b, t, 0)),
            pl.BlockSpec(w_main.shape, const2),
            pl.BlockSpec(w_f.shape, const2),
            pl.BlockSpec(b_f.shape, const2),
            pl.BlockSpec(tri.shape, const2),
        ] + slabs,
        out_specs=[
            pl.BlockSpec((1, IN_ROW_TILE, CONV_WIDTH), lambda b, t: (b, t, 0)),
            pl.BlockSpec((1, FOX_HEADS, IN_ROW_TILE, LANES), lambda b, t: (b, 0, t, 0)),
            pl.BlockSpec((1, FOX_HEADS, IN_ROW_TILE, LANES), lambda b, t: (b, 0, t, 0)),
            pl.BlockSpec((1, IN_ROW_TILE, FOX_WIDTH), lambda b, t: (b, t, 0)),
        ] + slabs,
        out_shape=[
            jax.ShapeDtypeStruct((bsz, seq, CONV_WIDTH), F32),
            jax.ShapeDtypeStruct((bsz, FOX_HEADS, seq, LANES), BF16),
            jax.ShapeDtypeStruct((bsz, FOX_HEADS, seq, LANES), BF16),
            jax.ShapeDtypeStruct((bsz, seq, FOX_WIDTH), BF16),
        ] + [jax.ShapeDtypeStruct(w.shape, BF16) for w in to_cast],
        scratch_shapes=[pltpu.VMEM((SUBLANES, LANES), F32)],
        compiler_params=pltpu.CompilerParams(
            dimension_semantics=("arbitrary", "arbitrary"), vmem_limit_bytes=VMEM_LIMIT),
        name="inproj",
    )(x, w_main, w_f, b_f, tri, *to_cast)
    return outs[:4], outs[4:]


def _fox_kernel(q_ref, qn_ref, k_ref, v_ref, o_ref, vt_scr, qt_scr, qtn_scr, acc_scr, s_scr, m_scr):
    i = pl.program_id(1)
    nq = pl.num_programs(1)
    nk = vt_scr.shape[0]
    npairs = FOX_HEADS // 2
    units = [(h, qb) for h in range(2) for qb in range(Q_TILE // Q_SUB)]
    first_diag = DIAG * i

    def band_keys(qb, band):
        if band is None:
            return K_TILE
        return max(0, min(K_TILE, (qb + 1) * Q_SUB - band * K_TILE))

    def transpose_q(src_ref, head0, dst):
        for h in range(2):
            dst[h] = src_ref[0, head0 + h].astype(F32).T.astype(BF16)

    def score_unit(pair, j, u, band, fresh, qt):
        h, qb = units[u]
        nkeys = band_keys(qb, band)
        if nkeys == 0:
            return
        kt = k_ref[0, 2 * pair + h, pl.ds(pl.multiple_of(j * K_TILE, K_TILE), nkeys), :]
        st = _mm(kt, qt[h, :, qb * Q_SUB:(qb + 1) * Q_SUB])
        if band is not None and band * K_TILE + nkeys - 1 > qb * Q_SUB:
            kpos = lax.broadcasted_iota(jnp.int32, st.shape, 0) + band * K_TILE
            qpos = lax.broadcasted_iota(jnp.int32, st.shape, 1) + qb * Q_SUB
            st = jnp.where(kpos <= qpos, st, NEG_INF)
        s_scr[u, 0:nkeys, :] = st
        m_prev = jnp.full((1, Q_SUB), NEG_INF, F32) if fresh else m_scr[u, 1]
        m_scr[u, 0] = m_prev
        m_scr[u, 1] = jnp.maximum(m_prev, jnp.max(st, axis=0, keepdims=True))

    def value_unit(pair, j, u, band):
        h, qb = units[u]
        nkeys = band_keys(qb, band)
        if nkeys == 0:
            return
        m_new = m_scr[u, 1]
        alpha = jnp.exp2(m_scr[u, 0] - m_new)
        pt = jnp.exp2((s_scr[u, 0:nkeys, :] - m_new).astype(BF16))
        acc_scr[u] = alpha * acc_scr[u] + _mm(vt_scr[j, 2 * pair + h, :, 0:nkeys], pt)

    def stages(value=None, score=None, value_band=None, score_band=None, fresh=False, qt=None):
        for u in range(len(units)):
            if value is not None:
                value_unit(value[0], value[1], u, value_band)
            if score is not None:
                score_unit(score[0], score[1], u, score_band, fresh, qt_scr.at[score[0]] if qt is None else qt)

    @pl.when(i == 0)
    def _():
        tail = (lax.broadcasted_iota(jnp.int32, (VT_ROWS - FOX_HEAD_DIM, K_TILE), 0) == 0).astype(BF16)
        for c in range(nk):
            for pair in range(npairs):
                blk = v_ref[0, c * K_TILE:(c + 1) * K_TILE, pair * LANES:(pair + 1) * LANES]
                blk = blk.astype(F32).T.astype(BF16)
                for h in range(2):
                    vt_scr[c, 2 * pair + h, 0:FOX_HEAD_DIM, :] = blk[h * FOX_HEAD_DIM:(h + 1) * FOX_HEAD_DIM, :]
                    vt_scr[c, 2 * pair + h, FOX_HEAD_DIM:VT_ROWS, :] = tail

    for pair in range(npairs):
        transpose_q(q_ref, 2 * pair, qt_scr.at[pair])

    @pl.when(i == 0)
    def _():
        stages(score=(0, 0), score_band=0, fresh=True)

    def one_pair(pair, carry):
        acc_scr[...] = jnp.zeros(acc_scr.shape, F32)

        def trip(t, c):
            stages(value=(pair, t), score=(pair, t + 1))
            return c

        lax.fori_loop(0, first_diag - 1, trip, 0)

        @pl.when(i > 0)
        def _():
            stages(value=(pair, first_diag - 1), score=(pair, first_diag), score_band=0)

        for band in range(DIAG - 1):
            stages(value=(pair, first_diag + band), score=(pair, first_diag + band + 1),
                   value_band=band, score_band=band + 1)

        last = (pair, first_diag + DIAG - 1)
        more_pairs = pair + 1 < npairs

        @pl.when(more_pairs & (i == 0))
        def _():
            stages(value=last, score=(pair + 1, 0), value_band=DIAG - 1, score_band=0, fresh=True)

        @pl.when(more_pairs & (i > 0))
        def _():
            stages(value=last, score=(pair + 1, 0), value_band=DIAG - 1, fresh=True)

        @pl.when(jnp.logical_not(more_pairs) & (i < nq - 1))
        def _():
            transpose_q(qn_ref, 0, qtn_scr)
            stages(value=last, score=(0, 0), value_band=DIAG - 1, fresh=True, qt=qtn_scr)

        @pl.when(jnp.logical_not(more_pairs) & (i == nq - 1))
        def _():
            stages(value=last, value_band=DIAG - 1)

        rows = []
        for h in range(2):
            blocks = [acc_scr[u] for u in range(len(units)) if units[u][0] == h]
            rows.append(jnp.concatenate(
                [a[0:FOX_HEAD_DIM] * (1.0 / a[FOX_HEAD_DIM:FOX_HEAD_DIM + 1]) for a in blocks], axis=1))
        o_ref[0, pair] = jnp.concatenate(rows, axis=0).T.astype(BF16)
        return carry

    lax.fori_loop(0, npairs, one_pair, 0)


def _fox_attention(q_aug, k_aug, v):
    bsz, _, seq, _ = q_aug.shape
    nq = seq // Q_TILE
    nk = seq // K_TILE
    return pl.pallas_call(
        _fox_kernel,
        grid=(bsz, nq),
        in_specs=[
            pl.BlockSpec((1, FOX_HEADS, Q_TILE, LANES), lambda b, i: (b, 0, i, 0)),
            pl.BlockSpec((1, 2, Q_TILE, LANES), lambda b, i: (b, 0, jnp.minimum(i + 1, nq - 1), 0)),
            pl.BlockSpec((1, FOX_HEADS, seq, LANES), lambda b, i: (b, 0, 0, 0)),
            pl.BlockSpec((1, seq, FOX_WIDTH), lambda b, i: (b, 0, 0)),
        ],
        out_specs=pl.BlockSpec((1, FOX_HEADS // 2, Q_TILE, LANES), lambda b, i: (b, 0, i, 0)),
        out_shape=jax.ShapeDtypeStruct((bsz, FOX_HEADS // 2, seq, LANES), BF16),
        scratch_shapes=[
            pltpu.VMEM((nk, FOX_HEADS, VT_ROWS, K_TILE), BF16),
            pltpu.VMEM((FOX_HEADS // 2, 2, LANES, Q_TILE), BF16),
            pltpu.VMEM((2, LANES, Q_TILE), BF16),
            pltpu.VMEM((2 * (Q_TILE // Q_SUB), VT_ROWS, Q_SUB), F32),
            pltpu.VMEM((2 * (Q_TILE // Q_SUB), K_TILE, Q_SUB), F32),
            pltpu.VMEM((2 * (Q_TILE // Q_SUB), 2, 1, Q_SUB), F32),
        ],
        compiler_params=pltpu.CompilerParams(
            dimension_semantics=("arbitrary", "arbitrary"), vmem_limit_bytes=VMEM_LIMIT),
        name="fox_attention",
    )(q_aug, q_aug, k_aug, v)


def _memkv_kernel(mem_ref, wk_ref, wv_ref, k_ref, v_ref):
    mb = mem_ref[0].astype(BF16)
    k_ref[0] = _mm(mb, wk_ref[...]).astype(BF16)
    v_ref[0] = _mm(mb, wv_ref[...]).astype(BF16)


def _mem_kv(mem, w_ck, w_cv):
    bsz = mem.shape[0]
    const2 = lambda b: (0, 0)
    blk = pl.BlockSpec((1, MEM_LEN, D_MODEL), lambda b: (b, 0, 0))
    return pl.pallas_call(
        _memkv_kernel,
        grid=(bsz,),
        in_specs=[blk, pl.BlockSpec(w_ck.shape, const2), pl.BlockSpec(w_cv.shape, const2)],
        out_specs=[blk, blk],
        out_shape=[jax.ShapeDtypeStruct((bsz, MEM_LEN, D_MODEL), BF16)] * 2,
        compiler_params=pltpu.CompilerParams(
            dimension_semantics=("arbitrary",), vmem_limit_bytes=VMEM_LIMIT),
        name="mem_kv",
    )(mem, w_ck, w_cv)


def _conv_branch(u_ref, cw_ref, cb_ref, cg_ref, cbeta_ref, ubuf, shifted, conv_scr):
    t = pl.program_id(1)
    rows = u_ref.shape[1]
    first = HALO - (CONV_K - 1)
    span = shifted.shape[1]

    @pl.when(t == 0)
    def _():
        ubuf[0:HALO, :] = jnp.zeros((HALO, CONV_WIDTH), F32)

    @pl.when(t > 0)
    def _():
        ubuf[0:HALO, :] = ubuf[rows:rows + HALO, :]

    ubuf[HALO:HALO + rows, :] = u_ref[0]
    for c0 in range(0, CONV_WIDTH, CONV_COLS):
        cols = slice(c0, c0 + CONV_COLS)
        for p in range(1, SUBLANES):
            shifted[p - 1] = ubuf[p:p + span, cols]
        cw = cw_ref[:, cols]
        cbias = cb_ref[:, cols]
        for r in range(rows // CONV_ROWS):
            base = r * CONV_ROWS
            acc = jnp.broadcast_to(cbias, (CONV_ROWS, CONV_COLS))
            for j in range(CONV_K):
                a, p = divmod(first + j, SUBLANES)
                lo = base + SUBLANES * a
                src = ubuf[lo:lo + CONV_ROWS, cols] if p == 0 else shifted[p - 1, lo:lo + CONV_ROWS, :]
                acc = acc + cw[j:j + 1, :] * src
            conv_scr[base:base + CONV_ROWS, cols] = acc

    gam = cg_ref[...]
    beta = cbeta_ref[...]
    outs = []
    for r in range(rows // LN_ROWS):
        y = _layer_norm(conv_scr[r * LN_ROWS:(r + 1) * LN_ROWS, :], gam, beta)
        outs.append((y * _sigmoid_approx(y)).astype(BF16))
    return jnp.concatenate(outs, axis=0)


def _mix_cross_kernel(x_ref, u_ref, o_ref, cw_ref, cb_ref, cg_ref, cbeta_ref, wout_ref, g1_ref, b1_ref,
                      wcq_ref, kc_ref, vc_ref, wco_ref, g2_ref, b2_ref, h_ref, ubuf, shifted, conv_scr):
    uc = _conv_branch(u_ref, cw_ref, cb_ref, cg_ref, cbeta_ref, ubuf, shifted, conv_scr)
    o_fox = jnp.concatenate([o_ref[0, p] for p in range(o_ref.shape[1])], axis=1)
    rows = x_ref.shape[1]
    groups = [slice(g * rows // CHAIN_SPLIT, (g + 1) * rows // CHAIN_SPLIT) for g in range(CHAIN_SPLIT)]
    mix = [_mm(uc[r], wout_ref[0:CONV_WIDTH, :]) + _mm(o_fox[r], wout_ref[CONV_WIDTH:, :]) for r in groups]
    h1, q = [], []
    for g, r in enumerate(groups):
        h1.append(_layer_norm(DEEPNORM_ALPHA * x_ref[0, r, :] + mix[g], g1_ref[...], b1_ref[...]))
        q.append((_mm(h1[g].astype(BF16), wcq_ref[...]) * (1.0 / math.sqrt(MEM_HEAD_DIM))).astype(BF16))
    o = []
    for g in range(CHAIN_SPLIT):
        outs = []
        for hh in range(MEM_HEADS):
            cols = slice(hh * MEM_HEAD_DIM, (hh + 1) * MEM_HEAD_DIM)
            s = _mm_nt(q[g][:, cols], kc_ref[0, :, cols])
            p = jnp.exp(s - jnp.max(s, axis=-1, keepdims=True))
            l = jnp.sum(p, axis=-1, keepdims=True)
            outs.append((_mm(p.astype(BF16), vc_ref[0, :, cols]) / l).astype(BF16))
        o.append(jnp.concatenate(outs, axis=-1))
    y = [_mm(o[g], wco_ref[...]) for g in range(CHAIN_SPLIT)]
    for g, r in enumerate(groups):
        h_ref[0, r, :] = _layer_norm(DEEPNORM_ALPHA * h1[g] + y[g], g2_ref[...], b2_ref[...])


def _mix_cross(x, u, o, conv_w, conv_b, conv_g, conv_beta, w_out, g1, b1, w_cq, kc, vc, w_co, g2, b2):
    bsz, seq, _ = x.shape
    nt = seq // MIX_ROW_TILE
    const2 = lambda b, t: (0, 0)
    tok = lambda width: pl.BlockSpec((1, MIX_ROW_TILE, width), lambda b, t: (b, t, 0))
    memblk = pl.BlockSpec((1, MEM_LEN, D_MODEL), lambda b, t: (b, 0, 0))
    vec = pl.BlockSpec((1, D_MODEL), const2)
    cvec = pl.BlockSpec((1, CONV_WIDTH), const2)
    fox = pl.BlockSpec((1, FOX_HEADS // 2, MIX_ROW_TILE, LANES), lambda b, t: (b, 0, t, 0))
    mat = pl.BlockSpec((D_MODEL, D_MODEL), const2)
    return pl.pallas_call(
        _mix_cross_kernel,
        grid=(bsz, nt),
        in_specs=[tok(D_MODEL), tok(CONV_WIDTH), fox, pl.BlockSpec(conv_w.shape, const2), cvec, cvec,
                  cvec, mat, vec, vec, mat, memblk, memblk, mat, vec, vec],
        out_specs=tok(D_MODEL),
        out_shape=jax.ShapeDtypeStruct((bsz, seq, D_MODEL), F32),
        scratch_shapes=[
            pltpu.VMEM((MIX_ROW_TILE + HALO, CONV_WIDTH), F32),
            pltpu.VMEM((SUBLANES - 1, MIX_ROW_TILE + HALO - SUBLANES, CONV_COLS), F32),
            pltpu.VMEM((MIX_ROW_TILE, CONV_WIDTH), F32),
        ],
        compiler_params=pltpu.CompilerParams(
            dimension_semantics=("arbitrary", "arbitrary"), vmem_limit_bytes=VMEM_LIMIT),
        name="mix_cross",
    )(x, u, o, conv_w, conv_b, conv_g, conv_beta, w_out, g1, b1, w_cq, kc, vc, w_co, g2, b2)


def _swiglu_kernel(h_ref, wg_ref, wu_ref, wd_ref, g_ref, b_ref, out_ref):
    h = h_ref[...]
    hb = h.astype(BF16)
    acc = DEEPNORM_ALPHA * h
    for c in range(D_FF // FF_CHUNK):
        cols = slice(c * FF_CHUNK, (c + 1) * FF_CHUNK)
        gate = _mm(hb, wg_ref[:, cols])
        up = _mm(hb, wu_ref[:, cols])
        act = (gate * _sigmoid_approx(gate) * up).astype(BF16)
        acc = acc + _mm(act, wd_ref[cols, :])
    out_ref[...] = _layer_norm(acc, g_ref[...], b_ref[...])


def _swiglu(h, w_gate, w_up, w_down, g, b):
    n = h.shape[0]
    const2 = lambda t: (0, 0)
    tok = pl.BlockSpec((FF_ROW_TILE, D_MODEL), lambda t: (t, 0))
    return pl.pallas_call(
        _swiglu_kernel,
        grid=(n // FF_ROW_TILE,),
        in_specs=[tok, pl.BlockSpec(w_gate.shape, const2), pl.BlockSpec(w_up.shape, const2),
                  pl.BlockSpec(w_down.shape, const2), pl.BlockSpec((1, D_MODEL), const2),
                  pl.BlockSpec((1, D_MODEL), const2)],
        out_specs=tok,
        out_shape=jax.ShapeDtypeStruct((n, D_MODEL), F32),
        compiler_params=pltpu.CompilerParams(
            dimension_semantics=("arbitrary",), vmem_limit_bytes=VMEM_LIMIT),
        name="swiglu",
    )(h, w_gate, w_up, w_down, g, b)


def _layer(h, mem, w_in, b_forget, conv_w, conv_b, conv_ln_g, conv_ln_b, w_out, ln_mix_g, ln_mix_b,
           w_cq, w_ck, w_cv, w_co, ln_cross_g, ln_cross_b, w_gate, w_up, w_down, ln_ffn_g, ln_ffn_b):
    bsz, seq, _ = h.shape
    row = lambda v: v.reshape(1, -1).astype(F32)
    w_main = w_in[:, :C_F].astype(BF16)
    w_f = jnp.pad(w_in[:, C_F:], ((0, 0), (0, LANES - FOX_HEADS))).astype(BF16)
    b_f = jnp.pad(b_forget.astype(F32), (0, LANES - FOX_HEADS)).reshape(1, LANES)
    tri = jnp.tri(TRI_ROWS, dtype=BF16)

    later = [w.astype(F32) for w in (w_ck, w_cv, w_out, w_cq, w_co, w_gate, w_up, w_down)]
    (u, q_aug, k_aug, v), later = _inproj(h, w_main, w_f, b_f, tri, later)
    w_ck, w_cv, w_out, w_cq, w_co, w_gate, w_up, w_down = later
    o = _fox_attention(q_aug, k_aug, v)
    kc, vc = _mem_kv(mem, w_ck, w_cv)
    h2 = _mix_cross(h, u, o, conv_w.astype(F32), row(conv_b), row(conv_ln_g), row(conv_ln_b),
                    w_out, row(ln_mix_g), row(ln_mix_b), w_cq, kc, vc, w_co, row(ln_cross_g), row(ln_cross_b))
    h3 = _swiglu(h2.reshape(bsz * seq, D_MODEL), w_gate, w_up, w_down, row(ln_ffn_g), row(ln_ffn_b))
    return h3.reshape(bsz, seq, D_MODEL)


def kernel(x, mem, w_in, b_forget, conv_w, conv_b, conv_ln_g, conv_ln_b, w_out, ln_mix_g, ln_mix_b,
           w_cq, w_ck, w_cv, w_co, ln_cross_g, ln_cross_b, w_gate, w_up, w_down, ln_ffn_g, ln_ffn_b):
    depth = w_in.shape[0]
    assert depth == 1, "DEEPNORM_ALPHA is fixed for a single layer"
    h = x
    for l in range(depth):
        h = _layer(h, mem, w_in[l], b_forget[l], conv_w[l], conv_b[l], conv_ln_g[l], conv_ln_b[l],
                   w_out[l], ln_mix_g[l], ln_mix_b[l], w_cq[l], w_ck[l], w_cv[l], w_co[l],
                   ln_cross_g[l], ln_cross_b[l], w_gate[l], w_up[l], w_down[l], ln_ffn_g[l], ln_ffn_b[l])
    return h
```
